```python
import math
import jax, jax.numpy as jnp
from jax import lax
import numpy as np

D_MODEL = 1024
BATCH = 8
SEQ = 2048
DEPTH = 2
DEC_BATCH = 32
DEC_SEQ = 1
PAST_LEN = 8192
PAGE_SIZE = 128

N_MIXERS = 2
N_META = 16
GLA_HEADS = 4
GLA_DK = D_MODEL // 2 // GLA_HEADS
GLA_DV = D_MODEL // GLA_HEADS
GLA_RANK = 16
GLA_TAU = 16.0
GLA_CHUNK = 64
GLA_IN = 2 * GLA_HEADS * GLA_DK + 2 * GLA_HEADS * GLA_DV + GLA_RANK
DIFF_HEADS = 8
DIFF_DH = D_MODEL // (2 * DIFF_HEADS)
DIFF_DV = 2 * DIFF_DH
DIFF_IN = 2 * (DIFF_HEADS * 2 * DIFF_DH) + DIFF_HEADS * DIFF_DV
ROT_DIM = DIFF_DH // 4
ROPE_THETA = 500000.0
Q_BLOCK = 128
D_FF = 4 * D_MODEL
LN_EPS = 1e-5
DEEPNORM_ALPHA = (2 * DEPTH) ** 0.25
DEEPNORM_BETA = (8 * DEPTH) ** -0.25
N_GLA_LAYERS = (DEPTH + 1) // 2
N_DIFF_LAYERS = DEPTH // 2

kernel_name = "gla_diffattn_hybrid_step"

F32 = jnp.float32


def layer_norm(x, g, b):
    xf = x.astype(F32)
    mu = jnp.mean(xf, axis=-1, keepdims=True)
    xc = xf - mu
    var = jnp.mean(xc * xc, axis=-1, keepdims=True)
    return (xc * lax.rsqrt(var + LN_EPS) * g.astype(F32) + b.astype(F32)).astype(x.dtype)


def rms_norm(x, g):
    xf = x.astype(F32)
    return xf * lax.rsqrt(jnp.mean(xf * xf, axis=-1, keepdims=True) + LN_EPS) * g.astype(F32)


def post_norm(x, f, g, b):
    return layer_norm(DEEPNORM_ALPHA * x + f, g, b)


def sq_relu_mlp(x, w1, w2):
    h = jax.nn.relu(x @ w1)
    return (h * h) @ w2


def gla_project(x, w_in, w_gate, b_gate):
    B, T, _ = x.shape
    h = x @ w_in
    qk = GLA_HEADS * GLA_DK
    vv = GLA_HEADS * GLA_DV
    q, k, v, r, g_low = jnp.split(h, [qk, 2 * qk, 2 * qk + vv, 2 * qk + 2 * vv], axis=-1)
    q = q.reshape(B, T, GLA_HEADS, GLA_DK).astype(F32) * (GLA_DK ** -0.5)
    k = k.reshape(B, T, GLA_HEADS, GLA_DK).astype(F32)
    v = v.reshape(B, T, GLA_HEADS, GLA_DV).astype(F32)
    lg = jax.nn.log_sigmoid((g_low @ w_gate + b_gate).astype(F32)) / GLA_TAU
    lg = lg.reshape(B, T, GLA_HEADS, GLA_DK)
    return q, k, v, lg, r


def gla_chunk(state, q, k, v, lg):
    L = q.shape[1]
    b = jnp.cumsum(lg, axis=1)
    o_inter = jnp.einsum('blhd,bhde->blhe', q * jnp.exp(b), state)
    causal = jnp.tril(jnp.ones((L, L), bool))[None, :, :, None, None]
    diff = b[:, :, None] - b[:, None, :]
    decay = jnp.where(causal, jnp.exp(jnp.where(causal, diff, 0.0)), 0.0)
    scores = jnp.einsum('bihd,bjhd,bijhd->bhij', q, k, decay)
    o_intra = jnp.einsum('bhij,bjhe->bihe', scores, v)
    b_last = b[:, -1]
    new_state = jnp.exp(b_last)[..., None] * state + jnp.einsum(
        'blhd,blhe->bhde', k * jnp.exp(b_last[:, None] - b), v)
    return new_state, o_inter + o_intra


def gla_output(o, r, g_norm, w_out, dtype):
    B, T = o.shape[:2]
    o = rms_norm(o, g_norm).reshape(B, T, GLA_HEADS * GLA_DV)
    o = o * jax.nn.silu(r.astype(F32))
    return o.astype(dtype) @ w_out


def gla_prompt(x, w_in, w_gate, b_gate, g_norm, w_out):
    B, T, _ = x.shape
    q, k, v, lg, r = gla_project(x, w_in, w_gate, b_gate)
    s0 = jnp.zeros((B, GLA_HEADS, GLA_DK, GLA_DV), F32)
    s_meta, o_meta = gla_chunk(s0, q[:, :N_META], k[:, :N_META], v[:, :N_META], lg[:, :N_META])
    n_chunks = (T - N_META) // GLA_CHUNK

    def to_chunks(a):
        return a[:, N_META:].reshape(B, n_chunks, GLA_CHUNK, *a.shape[2:]).swapaxes(0, 1)

    def step(s, c):
        return gla_chunk(s, *c)

    s_fin, o_rest = lax.scan(step, s_meta, (to_chunks(q), to_chunks(k), to_chunks(v), to_chunks(lg)))
    o_rest = o_rest.swapaxes(0, 1).reshape(B, T - N_META, GLA_HEADS, GLA_DV)
    o = jnp.concatenate([o_meta, o_rest], axis=1)
    return gla_output(o, r, g_norm, w_out, x.dtype), s_fin


def gla_sample(x, state, w_in, w_gate, b_gate, g_norm, w_out):
    q, k, v, lg, r = gla_project(x, w_in, w_gate, b_gate)
    s_new, o = gla_chunk(state.astype(F32), q, k, v, lg)
    return gla_output(o, r, g_norm, w_out, x.dtype), s_new


def rope_partial(x, pos):
    inv = ROPE_THETA ** (-jnp.arange(0, ROT_DIM, 2, dtype=F32) / ROT_DIM)
    ang = pos.astype(F32)[:, None] * inv[None, :]
    cos = jnp.cos(ang)[None, :, None, None, :]
    sin = jnp.sin(ang)[None, :, None, None, :]
    xr = x[..., :ROT_DIM].astype(F32)
    x1, x2 = xr[..., :ROT_DIM // 2], xr[..., ROT_DIM // 2:]
    rot = jnp.concatenate([x1 * cos - x2 * sin, x2 * cos + x1 * sin], axis=-1)
    return jnp.concatenate([rot.astype(x.dtype), x[..., ROT_DIM:]], axis=-1)


def diff_project(x, w_in, pos):
    B, T, _ = x.shape
    nq = DIFF_HEADS * 2 * DIFF_DH
    q, k, v = jnp.split(x @ w_in, [nq, 2 * nq], axis=-1)
    q = rope_partial(q.reshape(B, T, DIFF_HEADS, 2, DIFF_DH), pos)
    k = rope_partial(k.reshape(B, T, DIFF_HEADS, 2, DIFF_DH), pos)
    v = v.reshape(B, T, DIFF_HEADS, DIFF_DV)
    return q, k, v


def diff_core(q, k, v, q_pos, k_pos, lam):
    s = jnp.einsum('bqhmd,bkhmd->bhmqk', q, k).astype(F32) * (DIFF_DH ** -0.5)
    mask = k_pos[None, :] <= q_pos[:, None]
    s = jnp.where(mask, s, -jnp.inf)
    p = jax.nn.softmax(s, axis=-1)
    a = p[:, :, 0] - lam * p[:, :, 1]
    return jnp.einsum('bhqk,bkhe->bqhe', a, v.astype(F32))


def diff_output(o, g_sub, lam_init, w_out, dtype):
    B, T = o.shape[:2]
    o = rms_norm(o, g_sub) * (1.0 - lam_init)
    return o.reshape(B, T, DIFF_HEADS * DIFF_DV).astype(dtype) @ w_out


def diff_lambda_value(lam_params, lam_init):
    lp = lam_params.astype(F32)
    return jnp.exp(jnp.sum(lp[0] * lp[1])) - jnp.exp(jnp.sum(lp[2] * lp[3])) + lam_init


def pad_time(a, n):
    return jnp.pad(a, [(0, 0), (0, n)] + [(0, 0)] * (a.ndim - 2))


def diff_prompt(x, w_in, lam, lam_init, g_sub, w_out):
    B, T, _ = x.shape
    pos = jnp.arange(T)
    q, k, v = diff_project(x, w_in, pos)
    n_blocks = -(-T // Q_BLOCK)
    t_pad = n_blocks * Q_BLOCK
    qp, kp, vp = pad_time(q, t_pad - T), pad_time(k, t_pad - T), pad_time(v, t_pad - T)
    all_pos = jnp.arange(t_pad)
    q_blocks = qp.reshape(B, n_blocks, Q_BLOCK, DIFF_HEADS, 2, DIFF_DH).swapaxes(0, 1)
    pos_blocks = all_pos.reshape(n_blocks, Q_BLOCK)
    o = lax.map(lambda c: diff_core(c[0], kp, vp, c[1], all_pos, lam), (q_blocks, pos_blocks))
    o = o.swapaxes(0, 1).reshape(B, t_pad, DIFF_HEADS, DIFF_DV)[:, :T]
    out = diff_output(o, g_sub, lam_init, w_out, x.dtype)
    return out, k.reshape(B, T, DIFF_HEADS, 2 * DIFF_DH), v


def diff_sample(x, cache_k, cache_v, page_table, w_in, lam, lam_init, g_sub, w_out):
    B, T, _ = x.shape
    past = page_table.shape[1] * PAGE_SIZE
    pos = past + jnp.arange(T)
    q, k, v = diff_project(x, w_in, pos)
    k_past = cache_k[page_table].reshape(B, past, DIFF_HEADS, 2, DIFF_DH).astype(k.dtype)
    v_past = cache_v[page_table].reshape(B, past, DIFF_HEADS, DIFF_DV).astype(v.dtype)
    k_all = jnp.concatenate([k_past, k], axis=1)
    v_all = jnp.concatenate([v_past, v], axis=1)
    o = diff_core(q, k_all, v_all, pos, jnp.arange(past + T), lam)
    out = diff_output(o, g_sub, lam_init, w_out, x.dtype)
    return out, k.reshape(B, T, DIFF_HEADS, 2 * DIFF_DH), v


def setup_inputs(seed: int = 0) -> dict:
    key = jax.random.key(seed)
    ks = jax.random.split(key, 24)
    n_pages = PAST_LEN // PAGE_SIZE
    n_used = DEC_BATCH * n_pages
    n_pool = (n_used * 5) // 4
    nrm = jax.random.normal
    x_prompt = nrm(ks[0], (BATCH, SEQ, D_MODEL), F32)
    x_sample = nrm(ks[1], (DEC_BATCH, DEC_SEQ, D_MODEL), F32)
    state_gla = nrm(ks[2], (N_GLA_LAYERS, DEC_BATCH, GLA_HEADS, GLA_DK, GLA_DV), F32)
    cache_k = nrm(ks[3], (N_DIFF_LAYERS, n_pool, PAGE_SIZE, DIFF_HEADS, 2 * DIFF_DH), F32)
    cache_v = nrm(ks[4], (N_DIFF_LAYERS, n_pool, PAGE_SIZE, DIFF_HEADS, DIFF_DV), F32)
    page_table = jax.random.permutation(ks[5], n_pool)[:n_used].reshape(DEC_BATCH, n_pages).astype(jnp.int32)
    meta_tokens = nrm(ks[6], (N_META, D_MODEL), F32)
    gla_w_in = nrm(ks[7], (N_GLA_LAYERS, D_MODEL, GLA_IN), F32) * D_MODEL ** -0.5
    gla_w_gate = nrm(ks[8], (N_GLA_LAYERS, GLA_RANK, GLA_HEADS * GLA_DK), F32) * GLA_RANK ** -0.5
    gla_b_gate = 0.1 * nrm(ks[9], (N_GLA_LAYERS, GLA_HEADS * GLA_DK), F32)
    gla_norm = 1.0 + 0.01 * nrm(ks[10], (N_GLA_LAYERS, GLA_DV), F32)
    gla_w_out = nrm(ks[11], (N_GLA_LAYERS, GLA_HEADS * GLA_DV, D_MODEL), F32) * (GLA_HEADS * GLA_DV) ** -0.5 * DEEPNORM_BETA
    diff_w_in = nrm(ks[12], (N_DIFF_LAYERS, D_MODEL, DIFF_IN), F32) * D_MODEL ** -0.5
    diff_lambda = 0.1 * nrm(ks[13], (N_DIFF_LAYERS, 4, DIFF_DH), F32)
    diff_norm = 1.0 + 0.01 * nrm(ks[14], (N_DIFF_LAYERS, DIFF_DV), F32)
    diff_w_out = nrm(ks[15], (N_DIFF_LAYERS, DIFF_HEADS * DIFF_DV, D_MODEL), F32) * (DIFF_HEADS * DIFF_DV) ** -0.5 * DEEPNORM_BETA
    mlp_w1 = nrm(ks[16], (DEPTH, D_MODEL, D_FF), F32) * D_MODEL ** -0.5
    mlp_w2 = nrm(ks[17], (DEPTH, D_FF, D_MODEL), F32) * D_FF ** -0.5 * DEEPNORM_BETA
    ln_mix_g = 1.0 + 0.01 * nrm(ks[18], (DEPTH, D_MODEL), F32)
    ln_mix_b = 0.01 * nrm(ks[19], (DEPTH, D_MODEL), F32)
    ln_mlp_g = 1.0 + 0.01 * nrm(ks[20], (DEPTH, D_MODEL), F32)
    ln_mlp_b = 0.01 * nrm(ks[21], (DEPTH, D_MODEL), F32)
    return {"x_prompt": x_prompt, "x_sample": x_sample, "state_gla": state_gla,
            "cache_k": cache_k, "cache_v": cache_v, "page_table": page_table,
            "meta_tokens": meta_tokens, "gla_w_in": gla_w_in, "gla_w_gate": gla_w_gate,
            "gla_b_gate": gla_b_gate, "gla_norm": gla_norm, "gla_w_out": gla_w_out,
            "diff_w_in": diff_w_in, "diff_lambda": diff_lambda, "diff_norm": diff_norm,
            "diff_w_out": diff_w_out, "mlp_w1": mlp_w1, "mlp_w2": mlp_w2,
            "ln_mix_g": ln_mix_g, "ln_mix_b": ln_mix_b, "ln_mlp_g": ln_mlp_g, "ln_mlp_b": ln_mlp_b}


def reference(x_prompt, x_sample, state_gla, cache_k, cache_v, page_table, meta_tokens,
              gla_w_in, gla_w_gate, gla_b_gate, gla_norm, gla_w_out,
              diff_w_in, diff_lambda, diff_norm, diff_w_out,
              mlp_w1, mlp_w2, ln_mix_g, ln_mix_b, ln_mlp_g, ln_mlp_b):
    B = x_prompt.shape[0]
    meta = jnp.broadcast_to(meta_tokens[None].astype(x_prompt.dtype), (B, N_META, D_MODEL))
    x_p = jnp.concatenate([meta, x_prompt], axis=1)
    x_s = x_sample
    gla_sp, gla_ss, k_p, v_p, k_s, v_s = [], [], [], [], [], []
    for layer in range(DEPTH):
        j = layer // N_MIXERS
        if layer % N_MIXERS == 0:
            f_p, st_p = gla_prompt(x_p, gla_w_in[j], gla_w_gate[j], gla_b_gate[j], gla_norm[j], gla_w_out[j])
            f_s, st_s = gla_sample(x_s, state_gla[j], gla_w_in[j], gla_w_gate[j], gla_b_gate[j], gla_norm[j], gla_w_out[j])
            gla_sp.append(st_p)
            gla_ss.append(st_s)
        else:
            lam_init = 0.8 - 0.6 * math.exp(-0.3 * layer)
            lam = diff_lambda_value(diff_lambda[j], lam_init)
            f_p, kp_rows, vp_rows = diff_prompt(x_p, diff_w_in[j], lam, lam_init, diff_norm[j], diff_w_out[j])
            f_s, ks_rows, vs_rows = diff_sample(x_s, cache_k[j], cache_v[j], page_table, diff_w_in[j],
                                                lam, lam_init, diff_norm[j], diff_w_out[j])
            k_p.append(kp_rows)
            v_p.append(vp_rows)
            k_s.append(ks_rows)
            v_s.append(vs_rows)
        x_p = post_norm(x_p, f_p, ln_mix_g[layer], ln_mix_b[layer])
        x_s = post_norm(x_s, f_s, ln_mix_g[layer], ln_mix_b[layer])
        x_p = post_norm(x_p, sq_relu_mlp(x_p, mlp_w1[layer], mlp_w2[layer]), ln_mlp_g[layer], ln_mlp_b[layer])
        x_s = post_norm(x_s, sq_relu_mlp(x_s, mlp_w1[layer], mlp_w2[layer]), ln_mlp_g[layer], ln_mlp_b[layer])
    y_prompt = x_p[:, N_META:]
    return (y_prompt, x_s, jnp.stack(gla_sp), jnp.stack(gla_ss),
            jnp.stack(k_p), jnp.stack(v_p), jnp.stack(k_s), jnp.stack(v_s))
```

```python
import functools
import math

import jax
import jax.numpy as jnp
from jax import lax
from jax.experimental import pallas as pl
from jax.experimental.pallas import tpu as pltpu

F32 = jnp.float32
BF16 = jnp.bfloat16

DEPTH = 2
GLA_HEADS = 4
GLA_TAU = 16.0
DIFF_HEADS = 8
ROPE_THETA = 500000.0
PAGE_SIZE = 128
LN_EPS = 1e-5
DEEPNORM_ALPHA = (2 * DEPTH) ** 0.25
LAM_INIT = 0.8 - 0.6 * math.exp(-0.3 * 1)

LANES = 128
SUBLANES = 8
BF16_ROWS = 16
VMEM_LIMIT = 48 * 1024 * 1024

GLA_CHUNK = 128
ATTN_TILE = 256
DECODE_PAGES_PER_STEP = 8
MASK_VALUE = -1e30

_NT = (((1,), (1,)), ((), ()))
_TN = (((0,), (0,)), ((), ()))


def _row_tile(m):
    for cand in (688, 512, 256, 128, 64, 32, 16, 8):
        if m % cand == 0:
            return cand
    raise ValueError(f"row count {m} is not a multiple of 8")


def _col_tile(n):
    for cand in (512, 256, 128):
        if n % cand == 0:
            return cand
    raise ValueError(f"column count {n} is not a multiple of 128")


def _params(*sem):
    return pltpu.CompilerParams(dimension_semantics=sem, vmem_limit_bytes=VMEM_LIMIT)


def _layer_norm(y, g, b):
    mu = jnp.mean(y, axis=-1, keepdims=True)
    yc = y - mu
    var = jnp.mean(yc * yc, axis=-1, keepdims=True)
    return yc * lax.rsqrt(var + LN_EPS) * g + b


def _rms_norm(y, g):
    return y * lax.rsqrt(jnp.mean(y * y, axis=-1, keepdims=True) + LN_EPS) * g


def _lane_col(row, n):
    return jnp.transpose(jnp.broadcast_to(row, (n, n)))


def _gla_proj_kernel(x_ref, w_ref, wg_ref, wgate_ref, bgate_ref, h_ref, lg_ref, xb_ref):
    @pl.when(pl.program_id(1) == 0)
    def _():
        xb = x_ref[...].astype(BF16)
        xb_ref[...] = xb
        g_low = jnp.dot(xb, wg_ref[...], preferred_element_type=F32)
        z = jnp.dot(g_low.astype(BF16), wgate_ref[...], preferred_element_type=F32) + bgate_ref[...]
        log_sig = jnp.minimum(z, 0.0) - jnp.log(1.0 + jnp.exp(-jnp.abs(z)))
        lg_ref[...] = log_sig * (1.0 / GLA_TAU)

    h_ref[...] = jnp.dot(xb_ref[...], w_ref[...], preferred_element_type=F32)


def _gla_project(x, w_main, w_glow, w_gate, b_gate):
    m, d = x.shape
    n = w_main.shape[1]
    ng = w_gate.shape[1]
    tm, tn = _row_tile(m), _col_tile(n)
    return pl.pallas_call(
        _gla_proj_kernel,
        grid=(m // tm, n // tn),
        in_specs=[
            pl.BlockSpec((tm, d), lambda i, j: (i, 0)),
            pl.BlockSpec((d, tn), lambda i, j: (0, j)),
            pl.BlockSpec((d, LANES), lambda i, j: (0, 0)),
            pl.BlockSpec((LANES, ng), lambda i, j: (0, 0)),
            pl.BlockSpec((1, ng), lambda i, j: (0, 0)),
        ],
        out_specs=[
            pl.BlockSpec((tm, tn), lambda i, j: (i, j)),
            pl.BlockSpec((tm, ng), lambda i, j: (i, 0)),
        ],
        out_shape=[jax.ShapeDtypeStruct((m, n), F32), jax.ShapeDtypeStruct((m, ng), F32)],
        scratch_shapes=[pltpu.VMEM((tm, d), BF16)],
        compiler_params=_params("arbitrary", "arbitrary"),
        name="gla_proj",
    )(x, w_main, w_glow, w_gate, b_gate)


def _gla_core_kernel(q_ref, k_ref, v_ref, r_ref, lg_ref, gn_ref, o_ref, st_ref, s_scr,
                     *, n_meta, chunk, n_chunks):
    dk = q_ref.shape[1]
    scale = dk ** -0.5
    s_scr[...] = jnp.zeros_like(s_scr)

    def do_chunk(off, length):
        rows = pl.ds(off, length)
        ii = lax.broadcasted_iota(jnp.int32, (length, length), 0)
        jj = lax.broadcasted_iota(jnp.int32, (length, length), 1)
        causal = ii >= jj
        lg = lg_ref[rows, :]
        b = jnp.dot(causal.astype(F32), lg, precision=lax.Precision.HIGHEST,
                    preferred_element_type=F32)
        q = q_ref[rows, :] * scale
        k = k_ref[rows, :]
        vb = v_ref[rows, :].astype(BF16)
        st = s_scr[...]
        o = lax.dot_general((q * jnp.exp(b)).astype(BF16), st.astype(BF16), _NT,
                            preferred_element_type=F32)
        b_mid = b[length // 2:length // 2 + 1, :]
        qs = (q * jnp.exp(b - b_mid)).astype(BF16)
        ks = (k * jnp.exp(b_mid - b)).astype(BF16)
        scores = lax.dot_general(qs, ks, _NT, preferred_element_type=F32)
        scores = jnp.where(causal, scores, 0.0)
        o = o + jnp.dot(scores.astype(BF16), vb, preferred_element_type=F32)
        b_last = b[length - 1:length, :]
        kd = (k * jnp.exp(b_last - b)).astype(BF16)
        s_scr[...] = st * jnp.exp(b_last) + lax.dot_general(vb, kd, _TN, preferred_element_type=F32)
        r = r_ref[rows, :]
        gated = _rms_norm(o, gn_ref[...]) * (r / (1.0 + jnp.exp(-r)))
        o_ref[rows, :] = gated.astype(o_ref.dtype)

    do_chunk(0, n_meta)

    def body(c, carry):
        do_chunk(pl.multiple_of(n_meta + c * chunk, BF16_ROWS), chunk)
        return carry

    lax.fori_loop(0, n_chunks, body, 0)
    st_ref[0, 0] = jnp.transpose(s_scr[...])


def _gla_core(h, lg, g_norm, batch, seq, n_meta):
    m = h.shape[0]
    hk = lg.shape[1]
    dk = hk // GLA_HEADS
    dv = g_norm.shape[1]
    hv = GLA_HEADS * dv
    assert h.shape[1] == 2 * hk + 2 * hv and m == batch * seq
    chunk = GLA_CHUNK
    n_chunks = (seq - n_meta) // chunk
    assert n_meta + n_chunks * chunk == seq
    kq = hk // dk
    kern = functools.partial(_gla_core_kernel, n_meta=n_meta, chunk=chunk, n_chunks=n_chunks)
    return pl.pallas_call(
        kern,
        grid=(batch, GLA_HEADS),
        in_specs=[
            pl.BlockSpec((seq, dk), lambda b, hh: (b, hh)),
            pl.BlockSpec((seq, dk), lambda b, hh: (b, kq + hh)),
            pl.BlockSpec((seq, dv), lambda b, hh: (b, 2 * hk // dv + hh)),
            pl.BlockSpec((seq, dv), lambda b, hh: (b, (2 * hk + hv) // dv + hh)),
            pl.BlockSpec((seq, dk), lambda b, hh: (b, hh)),
            pl.BlockSpec((1, dv), lambda b, hh: (0, 0)),
        ],
        out_specs=[
            pl.BlockSpec((seq, dv), lambda b, hh: (b, hh)),
            pl.BlockSpec((1, 1, dk, dv), lambda b, hh: (b, hh, 0, 0)),
        ],
        out_shape=[jax.ShapeDtypeStruct((m, hv), BF16),
                   jax.ShapeDtypeStruct((batch, GLA_HEADS, dk, dv), F32)],
        scratch_shapes=[pltpu.VMEM((dv, dk), F32)],
        compiler_params=_params("arbitrary", "arbitrary"),
        name="gla_core",
    )(h, h, h, h, lg, g_norm)


def _gla_step_kernel(h_ref, lg_ref, gn_ref, s_ref, o_ref, sn_ref):
    dk, dv = s_ref.shape[2], s_ref.shape[3]
    hk, hv = GLA_HEADS * dk, GLA_HEADS * dv
    scale = dk ** -0.5
    for hh in range(GLA_HEADS):
        q = h_ref[0, :, hh * dk:(hh + 1) * dk] * scale
        k = h_ref[0, :, hk + hh * dk:hk + (hh + 1) * dk]
        v = h_ref[0, :, 2 * hk + hh * dv:2 * hk + (hh + 1) * dv]
        r = h_ref[0, :, 2 * hk + hv + hh * dv:2 * hk + hv + (hh + 1) * dv]
        a = jnp.exp(lg_ref[0, :, hh * dk:(hh + 1) * dk])
        reps = dv // dk
        a_c = jnp.tile(_lane_col(a, dk), (1, reps))
        k_c = jnp.tile(_lane_col(k, dk), (1, reps))
        q_c = jnp.tile(_lane_col(q, dk), (1, reps))
        s_new = s_ref[0, hh] * a_c + k_c * v
        sn_ref[0, hh] = s_new
        o = jnp.sum(q_c * s_new, axis=0, keepdims=True)
        gated = _rms_norm(o, gn_ref[...]) * (r / (1.0 + jnp.exp(-r)))
        o_ref[0, :, hh * dv:(hh + 1) * dv] = gated.astype(o_ref.dtype)


def _gla_step(h, lg, g_norm, state):
    bs, n = h.shape
    _, _, dk, dv = state.shape
    hv = GLA_HEADS * dv
    o, s_new = pl.pallas_call(
        _gla_step_kernel,
        grid=(bs,),
        in_specs=[
            pl.BlockSpec((1, 1, n), lambda b: (b, 0, 0)),
            pl.BlockSpec((1, 1, lg.shape[1]), lambda b: (b, 0, 0)),
            pl.BlockSpec((1, dv), lambda b: (0, 0)),
            pl.BlockSpec((1, GLA_HEADS, dk, dv), lambda b: (b, 0, 0, 0)),
        ],
        out_specs=[
            pl.BlockSpec((1, 1, hv), lambda b: (b, 0, 0)),
            pl.BlockSpec((1, GLA_HEADS, dk, dv), lambda b: (b, 0, 0, 0)),
        ],
        out_shape=[jax.ShapeDtypeStruct((bs, 1, hv), BF16), jax.ShapeDtypeStruct(state.shape, F32)],
        compiler_params=_params("arbitrary"),
        name="gla_step",
    )(h.reshape(bs, 1, n), lg.reshape(bs, 1, -1), g_norm, state)
    return o.reshape(bs, hv), s_new


def _proj_ln_kernel(a_ref, w_ref, x_ref, g_ref, b_ref, o_ref):
    f = jnp.dot(a_ref[...], w_ref[...], preferred_element_type=F32)
    o_ref[...] = _layer_norm(DEEPNORM_ALPHA * x_ref[...] + f, g_ref[...], b_ref[...])


def _proj_ln(a, w, x, g, b):
    m, kk = a.shape
    d = w.shape[1]
    tm = _row_tile(m)
    return pl.pallas_call(
        _proj_ln_kernel,
        grid=(m // tm,),
        in_specs=[
            pl.BlockSpec((tm, kk), lambda i: (i, 0)),
            pl.BlockSpec((kk, d), lambda i: (0, 0)),
            pl.BlockSpec((tm, d), lambda i: (i, 0)),
            pl.BlockSpec((1, d), lambda i: (0, 0)),
            pl.BlockSpec((1, d), lambda i: (0, 0)),
        ],
        out_specs=pl.BlockSpec((tm, d), lambda i: (i, 0)),
        out_shape=jax.ShapeDtypeStruct((m, d), F32),
        compiler_params=_params("arbitrary"),
        name="proj_ln",
    )(a, w, x, g, b)


def _mlp_ln_kernel(x_ref, w1_ref, w2_ref, g_ref, b_ref, o_ref, xb_ref, acc_ref):
    f = pl.program_id(1)

    @pl.when(f == 0)
    def _():
        xb_ref[...] = x_ref[...].astype(BF16)
        acc_ref[...] = jnp.zeros_like(acc_ref)

    hid = jnp.maximum(jnp.dot(xb_ref[...], w1_ref[...], preferred_element_type=F32), 0.0)
    acc_ref[...] += jnp.dot((hid * hid).astype(BF16), w2_ref[...], preferred_element_type=F32)

    @pl.when(f == pl.num_programs(1) - 1)
    def _():
        o_ref[...] = _layer_norm(DEEPNORM_ALPHA * x_ref[...] + acc_ref[...], g_ref[...], b_ref[...])


def _mlp_ln(x, w1, w2, g, b):
    m, d = x.shape
    ff = w1.shape[1]
    tm, tf = _row_tile(m), _col_tile(ff)
    return pl.pallas_call(
        _mlp_ln_kernel,
        grid=(m // tm, ff // tf),
        in_specs=[
            pl.BlockSpec((tm, d), lambda i, f: (i, 0)),
            pl.BlockSpec((d, tf), lambda i, f: (0, f)),
            pl.BlockSpec((tf, d), lambda i, f: (f, 0)),
            pl.BlockSpec((1, d), lambda i, f: (0, 0)),
            pl.BlockSpec((1, d), lambda i, f: (0, 0)),
        ],
        out_specs=pl.BlockSpec((tm, d), lambda i, f: (i, 0)),
        out_shape=jax.ShapeDtypeStruct((m, d), F32),
        scratch_shapes=[pltpu.VMEM((tm, d), BF16), pltpu.VMEM((tm, d), F32)],
        compiler_params=_params("arbitrary", "arbitrary"),
        name="mlp_ln",
    )(x, w1, w2, g, b)


def _diff_proj_kernel(x_ref, w_ref, cos_ref, sin_up_ref, sin_dn_ref,
                      q_ref, k_ref, kb_ref, v_ref, vb_ref, xb_ref, *, nq_tiles, q_scale, rot_half):
    j = pl.program_id(1)

    @pl.when(j == 0)
    def _():
        xb_ref[...] = x_ref[...].astype(BF16)

    y = jnp.dot(xb_ref[...], w_ref[...], preferred_element_type=F32)
    tn = y.shape[1]

    def rope(t):
        reps = tn // LANES
        c = jnp.tile(cos_ref[...], (1, reps))
        s_up = jnp.tile(sin_up_ref[...], (1, reps))
        s_dn = jnp.tile(sin_dn_ref[...], (1, reps))
        return t * c + pltpu.roll(t, tn - rot_half, 1) * s_up + pltpu.roll(t, rot_half, 1) * s_dn

    @pl.when(j < nq_tiles)
    def _():
        q_ref[...] = (rope(y) * q_scale).astype(q_ref.dtype)

    @pl.when(jnp.logical_and(j >= nq_tiles, j < 2 * nq_tiles))
    def _():
        kr = rope(y)
        k_ref[...] = kr
        kb_ref[...] = kr.astype(kb_ref.dtype)

    @pl.when(j >= 2 * nq_tiles)
    def _():
        v_ref[...] = y
        vb_ref[...] = y.astype(vb_ref.dtype)


def _rope_tables(pos, dh):
    rot = dh // 4
    inv = ROPE_THETA ** (-jnp.arange(0, rot, 2, dtype=F32) / rot)
    ang = pos.astype(F32)[:, None] * inv[None, :]
    n = pos.shape[0]

    def lanes(first, second, fill):
        sub = jnp.concatenate([first, second, jnp.full((n, dh - rot), fill, F32)], axis=1)
        return jnp.tile(sub, (1, LANES // dh))

    zeros = jnp.zeros_like(ang)
    return (lanes(jnp.cos(ang), jnp.cos(ang), 1.0),
            lanes(-jnp.sin(ang), zeros, 0.0),
            lanes(zeros, jnp.sin(ang), 0.0))


def _diff_project(x, w, pos):
    m, d = x.shape
    n = w.shape[1]
    nq = n // 3
    dh = nq // (2 * DIFF_HEADS)
    assert m % pos.shape[0] == 0
    tm, tn = _row_tile(pos.shape[0]), _col_tile(nq)
    nq_tiles = nq // tn
    pos_tiles = pos.shape[0] // tm
    cos_t, sin_up, sin_dn = _rope_tables(pos, dh)
    kern = functools.partial(_diff_proj_kernel, nq_tiles=nq_tiles, q_scale=dh ** -0.5, rot_half=dh // 8)
    tab_spec = pl.BlockSpec((tm, LANES), lambda i, j: (i % pos_tiles, 0))

    def seg_spec(seg):
        return pl.BlockSpec((tm, tn), lambda i, j: (i, jnp.clip(j - seg * nq_tiles, 0, nq_tiles - 1)))

    return pl.pallas_call(
        kern,
        grid=(m // tm, 3 * nq_tiles),
        in_specs=[
            pl.BlockSpec((tm, d), lambda i, j: (i, 0)),
            pl.BlockSpec((d, tn), lambda i, j: (0, j)),
            tab_spec, tab_spec, tab_spec,
        ],
        out_specs=[seg_spec(0), seg_spec(1), seg_spec(1), seg_spec(2), seg_spec(2)],
        out_shape=[jax.ShapeDtypeStruct((m, nq), BF16),
                   jax.ShapeDtypeStruct((m, nq), F32), jax.ShapeDtypeStruct((m, nq), BF16),
                   jax.ShapeDtypeStruct((m, nq), F32), jax.ShapeDtypeStruct((m, nq), BF16)],
        scratch_shapes=[pltpu.VMEM((tm, d), BF16)],
        compiler_params=_params("arbitrary", "arbitrary"),
        name="diff_proj",
    )(x, w, cos_t, sin_up, sin_dn)


def _lambda_value(lp):
    e1 = jnp.exp(jnp.sum(lp[0:1] * lp[1:2], axis=1, keepdims=True))
    e2 = jnp.exp(jnp.sum(lp[2:3] * lp[3:4], axis=1, keepdims=True))
    return e1 - e2 + LAM_INIT


def _diff_attn_kernel(lam_ref, q_ref, k_ref, v_ref, g_ref, o_ref, m_scr, l_scr, acc_scr,
                      *, n_meta, tile, n_tiles):
    lam = _lambda_value(lam_ref[...])
    dv = v_ref.shape[1]
    first_half = lax.broadcasted_iota(jnp.int32, (1, q_ref.shape[1]), 1) < q_ref.shape[1] // 2

    def update(qs, koff, klen, mask):
        nrows = qs.shape[0]
        kt = k_ref[pl.ds(koff, klen), :]
        vt = v_ref[pl.ds(koff, klen), :]
        s = lax.dot_general(qs, kt, _NT, preferred_element_type=F32)
        if mask is not None:
            s = jnp.where(mask, s, MASK_VALUE)
        m_old = m_scr[0:nrows, :]
        m_new = jnp.maximum(m_old, jnp.max(s, axis=1, keepdims=True))
        alpha = jnp.exp(m_old - m_new)
        p = jnp.exp(s - m_new)
        m_scr[0:nrows, :] = m_new
        l_scr[0:nrows, :] = alpha * l_scr[0:nrows, :] + jnp.sum(p, axis=1, keepdims=True)
        acc_scr[0:nrows, :] = alpha * acc_scr[0:nrows, :] + jnp.dot(
            p.astype(BF16), vt, preferred_element_type=F32)

    def q_tile(qoff, length, n_full):
        nrows = 2 * length
        q = q_ref[pl.ds(qoff, length), :]
        zero = jnp.zeros_like(q)
        qs = jnp.concatenate([jnp.where(first_half, q, zero), jnp.where(first_half, zero, q)], axis=0)
        m_scr[0:nrows, :] = jnp.full((nrows, 1), MASK_VALUE, F32)
        l_scr[0:nrows, :] = jnp.zeros((nrows, 1), F32)
        acc_scr[0:nrows, :] = jnp.zeros((nrows, dv), F32)
        ri = lax.broadcasted_iota(jnp.int32, (nrows, length), 0)
        ri = jnp.where(ri >= length, ri - length, ri)
        ci = lax.broadcasted_iota(jnp.int32, (nrows, length), 1)
        update(qs, qoff, length, ci <= ri)
        if n_full is not None:
            update(qs, 0, n_meta, None)

            def body(jt, carry):
                update(qs, pl.multiple_of(n_meta + jt * tile, BF16_ROWS), tile, None)
                return carry

            lax.fori_loop(0, n_full, body, 0)
        l = l_scr[0:nrows, :]
        acc = acc_scr[0:nrows, :]
        o = acc[:length] / l[:length] - lam * (acc[length:] / l[length:])
        o = _rms_norm(o, g_ref[...]) * (1.0 - LAM_INIT)
        o_ref[pl.ds(qoff, length), :] = o.astype(o_ref.dtype)

    q_tile(0, n_meta, None)

    def outer(it, carry):
        q_tile(pl.multiple_of(n_meta + it * tile, BF16_ROWS), tile, it)
        return carry

    lax.fori_loop(0, n_tiles, outer, 0)


def _diff_attention(q, k, v, lam_params, g_sub, batch, seq, n_meta):
    m, n = q.shape
    dv = g_sub.shape[1]
    tile = min(ATTN_TILE, seq - n_meta)
    n_tiles = (seq - n_meta) // tile
    assert n_meta + n_tiles * tile == seq and n == DIFF_HEADS * dv
    kern = functools.partial(_diff_attn_kernel, n_meta=n_meta, tile=tile, n_tiles=n_tiles)
    head_spec = pl.BlockSpec((seq, dv), lambda b, hh: (b, hh))
    return pl.pallas_call(
        kern,
        grid=(batch, DIFF_HEADS),
        in_specs=[
            pl.BlockSpec(lam_params.shape, lambda b, hh: (0, 0)),
            head_spec, head_spec, head_spec,
            pl.BlockSpec((1, dv), lambda b, hh: (0, 0)),
        ],
        out_specs=head_spec,
        out_shape=jax.ShapeDtypeStruct((m, n), BF16),
        scratch_shapes=[pltpu.VMEM((2 * tile, 1), F32), pltpu.VMEM((2 * tile, 1), F32),
                        pltpu.VMEM((2 * tile, dv), F32)],
        compiler_params=_params("arbitrary", "arbitrary"),
        name="diff_attn",
    )(lam_params, q, k, v, g_sub)


def _decode_attn_kernel(pt_ref, lam_ref, q_ref, kc_ref, vc_ref, g_ref, *refs, pages_per_step):
    del pt_ref
    k_refs = refs[:pages_per_step]
    v_refs = refs[pages_per_step:2 * pages_per_step]
    o_ref, m_scr, l_scr, acc_scr = refs[2 * pages_per_step:]
    step = pl.program_id(1)
    _, n_head, dv = q_ref.shape
    qt = q_ref[0]
    same_half = (lax.broadcasted_iota(jnp.int32, (dv, 2 * dv), 0) < dv // 2) == (
        lax.broadcasted_iota(jnp.int32, (dv, 2 * dv), 1) < dv)
    half_sum = same_half.astype(BF16)

    def scores(k3):
        t = k3.shape[0]
        prod = (k3 * qt[None]).astype(BF16).reshape(t * n_head, dv)
        s = jnp.dot(prod, half_sum, preferred_element_type=F32).reshape(t, n_head, 2 * dv)
        return s[:, :, :dv], s[:, :, dv:]

    @pl.when(step == 0)
    def _():
        m_scr[...] = jnp.full(m_scr.shape, MASK_VALUE, F32)
        l_scr[...] = jnp.zeros_like(l_scr)
        acc_scr[...] = jnp.zeros_like(acc_scr)

    for pg in range(pages_per_step):
        v3 = v_refs[pg][0]
        for mi, sm in enumerate(scores(k_refs[pg][0])):
            m_old = m_scr[mi]
            m_new = jnp.maximum(m_old, jnp.max(sm, axis=0))
            alpha = jnp.exp(m_old - m_new)
            p = jnp.exp(sm - m_new[None])
            m_scr[mi] = m_new
            l_scr[mi] = alpha * l_scr[mi] + jnp.sum(p, axis=0)
            acc_scr[mi] = alpha * acc_scr[mi] + jnp.sum(p * v3, axis=0)

    @pl.when(step == pl.num_programs(1) - 1)
    def _():
        lam = _lambda_value(lam_ref[...])
        outs = []
        for mi, sc in enumerate(scores(kc_ref[...])):
            m_old = m_scr[mi]
            m_new = jnp.maximum(m_old, sc[0])
            alpha = jnp.exp(m_old - m_new)
            pc = jnp.exp(sc[0] - m_new)
            outs.append((alpha * acc_scr[mi] + pc * vc_ref[0]) / (alpha * l_scr[mi] + pc))
        o = outs[0] - lam * outs[1]
        o_ref[0] = _rms_norm(o, g_ref[...]) * (1.0 - LAM_INIT)


def _decode_attention(q, k_cur, v_cur, cache_k, cache_v, page_table, lam_params, g_sub):
    bs, n_head, dv = q.shape
    n_pages = page_table.shape[1]
    pps = DECODE_PAGES_PER_STEP
    while n_pages % pps:
        pps //= 2
    page = cache_k.shape[1]
    row_spec = pl.BlockSpec((1, n_head, dv), lambda b, s, pt: (b, 0, 0))

    def page_spec(i):
        return pl.BlockSpec((1, page, n_head, dv), lambda b, s, pt: (pt[b, s * pps + i], 0, 0, 0))

    grid_spec = pltpu.PrefetchScalarGridSpec(
        num_scalar_prefetch=1,
        grid=(bs, n_pages // pps),
        in_specs=[
            pl.BlockSpec(lam_params.shape, lambda b, s, pt: (0, 0)),
            row_spec, row_spec, row_spec,
            pl.BlockSpec((1, dv), lambda b, s, pt: (0, 0)),
        ] + [page_spec(i) for i in range(pps)] + [page_spec(i) for i in range(pps)],
        out_specs=row_spec,
        scratch_shapes=[pltpu.VMEM((2, n_head, dv), F32)] * 3,
    )
    return pl.pallas_call(
        functools.partial(_decode_attn_kernel, pages_per_step=pps),
        grid_spec=grid_spec,
        out_shape=jax.ShapeDtypeStruct((bs, n_head, dv), F32),
        compiler_params=_params("arbitrary", "arbitrary"),
        name="decode_attn",
    )(page_table, lam_params, q, k_cur, v_cur, g_sub, *([cache_k] * pps), *([cache_v] * pps))


def kernel(x_prompt, x_sample, state_gla, cache_k, cache_v, page_table, meta_tokens,
           gla_w_in, gla_w_gate, gla_b_gate, gla_norm, gla_w_out,
           diff_w_in, diff_lambda, diff_norm, diff_w_out,
           mlp_w1, mlp_w2, ln_mix_g, ln_mix_b, ln_mlp_g, ln_mlp_b):
    batch, seq_new, d = x_prompt.shape
    n_meta = meta_tokens.shape[0]
    seq = n_meta + seq_new
    bs = x_sample.shape[0]
    assert x_sample.shape[1] == 1

    meta = jnp.broadcast_to(meta_tokens[None].astype(x_prompt.dtype), (batch, n_meta, d))
    x_p = jnp.concatenate([meta, x_prompt], axis=1).reshape(batch * seq, d)
    x_s = x_sample.reshape(bs, d)

    def row(vec):
        return vec.reshape(1, -1).astype(F32)

    hk = gla_w_gate.shape[2]
    rank = gla_w_gate.shape[1]
    n_main = gla_w_in.shape[2] - rank
    w_main = gla_w_in[0, :, :n_main].astype(BF16)
    w_glow = jnp.pad(gla_w_in[0, :, n_main:], ((0, 0), (0, LANES - rank))).astype(BF16)
    w_gate = jnp.pad(gla_w_gate[0], ((0, LANES - rank), (0, 0))).astype(BF16)
    b_gate = row(gla_b_gate[0])
    g_norm = row(gla_norm[0])
    w_out0 = gla_w_out[0].astype(BF16)

    h_p, lg_p = _gla_project(x_p, w_main, w_glow, w_gate, b_gate)
    h_s, lg_s = _gla_project(x_s, w_main, w_glow, w_gate, b_gate)
    o_p, st_p = _gla_core(h_p, lg_p, g_norm, batch, seq, n_meta)
    o_s, st_s = _gla_step(h_s, lg_s, g_norm, state_gla[0].astype(F32))

    def finish_layer(layer, o, x, w_out):
        x = _proj_ln(o, w_out, x, row(ln_mix_g[layer]), row(ln_mix_b[layer]))
        return _mlp_ln(x, mlp_w1[layer].astype(BF16), mlp_w2[layer].astype(BF16),
                       row(ln_mlp_g[layer]), row(ln_mlp_b[layer]))

    x_p = finish_layer(0, o_p, x_p, w_out0)
    x_s = finish_layer(0, o_s, x_s, w_out0)

    w_in1 = diff_w_in[0].astype(BF16)
    w_out1 = diff_w_out[0].astype(BF16)
    g_sub = row(diff_norm[0])
    lam_params = diff_lambda[0].astype(F32)
    past = page_table.shape[1] * PAGE_SIZE
    dv = g_sub.shape[1]

    q_p, k_p, kb_p, v_p, vb_p = _diff_project(x_p, w_in1, jnp.arange(seq))
    q_s, k_s, _, v_s, _ = _diff_project(x_s, w_in1, jnp.full((bs,), past))
    a_p = _diff_attention(q_p, kb_p, vb_p, lam_params, g_sub, batch, seq, n_meta)
    heads = (bs, DIFF_HEADS, dv)
    a_s = _decode_attention(q_s.astype(F32).reshape(heads), k_s.reshape(heads), v_s.reshape(heads),
                            cache_k[0], cache_v[0], page_table, lam_params, g_sub)
    a_s = a_s.reshape(bs, DIFF_HEADS * dv).astype(BF16)

    x_p = finish_layer(1, a_p, x_p, w_out1)
    x_s = finish_layer(1, a_s, x_s, w_out1)

    y_prompt = x_p.reshape(batch, seq, d)[:, n_meta:]
    y_sample = x_s.reshape(bs, 1, d)
    return (y_prompt, y_sample, st_p[None], st_s[None],
            k_p.reshape(1, batch, seq, DIFF_HEADS, dv), v_p.reshape(1, batch, seq, DIFF_HEADS, dv),
            k_s.reshape(1, bs, 1, DIFF_HEADS, dv), v_s.reshape(1, bs, 1, DIFF_HEADS, dv))
```

```python
import functools
import math

import jax
import jax.numpy as jnp
from jax import lax
from jax.experimental import pallas as pl
from jax.experimental.pallas import tpu as pltpu

F32 = jnp.float32
BF16 = jnp.bfloat16

DEPTH = 2
GLA_HEADS = 4
GLA_TAU = 16.0
DIFF_HEADS = 8
ROPE_THETA = 500000.0
PAGE_SIZE = 128
LN_EPS = 1e-5
DEEPNORM_ALPHA = (2 * DEPTH) ** 0.25
LAM_INIT = 0.8 - 0.6 * math.exp(-0.3 * 1)

LANES = 128
SUBLANES = 8
BF16_ROWS = 16
VMEM_LIMIT = 48 * 1024 * 1024

GLA_CHUNK = 128
ATTN_TILE = 256
DECODE_PAGES_PER_STEP = 8
MASK_VALUE = -1e30
LOG2_E = 1.4426950408889634

_NT = (((1,), (1,)), ((), ()))
_TN = (((0,), (0,)), ((), ()))


def _row_tile(m):
    for cand in (688, 512, 256, 128, 64, 32, 16, 8):
        if m % cand == 0:
            return cand
    raise ValueError(f"row count {m} is not a multiple of 8")


def _col_tile(n):
    for cand in (512, 256, 128):
        if n % cand == 0:
            return cand
    raise ValueError(f"column count {n} is not a multiple of 128")


def _params(*sem):
    return pltpu.CompilerParams(dimension_semantics=sem, vmem_limit_bytes=VMEM_LIMIT)


def _layer_norm(y, g, b):
    mu = jnp.mean(y, axis=-1, keepdims=True)
    yc = y - mu
    var = jnp.mean(yc * yc, axis=-1, keepdims=True)
    return yc * lax.rsqrt(var + LN_EPS) * g + b


def _rms_norm(y, g):
    return y * lax.rsqrt(jnp.mean(y * y, axis=-1, keepdims=True) + LN_EPS) * g


def _lane_col(row, n):
    return jnp.transpose(jnp.broadcast_to(row, (n, n)))


def _gla_proj_kernel(x_ref, w_ref, wg_ref, wgate_ref, bgate_ref, h_ref, lg_ref, *, tn):
    xb = x_ref[...].astype(BF16)
    g_low = jnp.dot(xb, wg_ref[...], preferred_element_type=F32)
    z = jnp.dot(g_low.astype(BF16), wgate_ref[...], preferred_element_type=F32) + bgate_ref[...]
    log_sig = jnp.minimum(z, 0.0) - jnp.log(1.0 + jnp.exp(-jnp.abs(z)))
    lg_ref[...] = log_sig * (1.0 / GLA_TAU)
    for j in range(w_ref.shape[1] // tn):
        cols = slice(j * tn, (j + 1) * tn)
        h_ref[:, cols] = jnp.dot(xb, w_ref[:, cols], preferred_element_type=F32)


def _gla_project(x, w_main, w_glow, w_gate, b_gate):
    m, d = x.shape
    n = w_main.shape[1]
    ng = w_gate.shape[1]
    tm = _row_tile(m)
    return pl.pallas_call(
        functools.partial(_gla_proj_kernel, tn=_col_tile(n)),
        grid=(m // tm,),
        in_specs=[
            pl.BlockSpec((tm, d), lambda i: (i, 0)),
            pl.BlockSpec((d, n), lambda i: (0, 0)),
            pl.BlockSpec((d, LANES), lambda i: (0, 0)),
            pl.BlockSpec((LANES, ng), lambda i: (0, 0)),
            pl.BlockSpec((1, ng), lambda i: (0, 0)),
        ],
        out_specs=[
            pl.BlockSpec((tm, n), lambda i: (i, 0)),
            pl.BlockSpec((tm, ng), lambda i: (i, 0)),
        ],
        out_shape=[jax.ShapeDtypeStruct((m, n), F32), jax.ShapeDtypeStruct((m, ng), F32)],
        compiler_params=_params("arbitrary"),
        name="gla_proj",
    )(x, w_main, w_glow, w_gate, b_gate)


def _gla_core_kernel(q_ref, k_ref, v_ref, r_ref, lg_ref, gn_ref, o_ref, st_ref, s_scr,
                     *, n_meta, chunk, n_chunks):
    dk = q_ref.shape[1]
    scale = dk ** -0.5
    s_scr[...] = jnp.zeros_like(s_scr)

    def do_chunk(off, length):
        rows = pl.ds(off, length)
        ii = lax.broadcasted_iota(jnp.int32, (length, length), 0)
        jj = lax.broadcasted_iota(jnp.int32, (length, length), 1)
        causal = ii >= jj
        lg = lg_ref[rows, :]
        b = jnp.dot(causal.astype(F32), lg, precision=lax.Precision.HIGHEST,
                    preferred_element_type=F32)
        q = q_ref[rows, :] * scale
        k = k_ref[rows, :]
        vb = v_ref[rows, :].astype(BF16)
        st = s_scr[...]
        o = lax.dot_general((q * jnp.exp(b)).astype(BF16), st.astype(BF16), _NT,
                            preferred_element_type=F32)
        b_mid = b[length // 2:length // 2 + 1, :]
        qs = (q * jnp.exp(b - b_mid)).astype(BF16)
        ks = (k * jnp.exp(b_mid - b)).astype(BF16)
        scores = lax.dot_general(qs, ks, _NT, preferred_element_type=F32)
        scores = jnp.where(causal, scores, 0.0)
        o = o + jnp.dot(scores.astype(BF16), vb, preferred_element_type=F32)
        b_last = b[length - 1:length, :]
        kd = (k * jnp.exp(b_last - b)).astype(BF16)
        s_scr[...] = st * jnp.exp(b_last) + lax.dot_general(vb, kd, _TN, preferred_element_type=F32)
        r = r_ref[rows, :]
        gated = _rms_norm(o, gn_ref[...]) * (r / (1.0 + jnp.exp(-r)))
        o_ref[rows, :] = gated.astype(o_ref.dtype)

    do_chunk(0, n_meta)
    for c in range(n_chunks):
        do_chunk(n_meta + c * chunk, chunk)
    st_ref[0, 0] = jnp.transpose(s_scr[...])


def _gla_core(h, lg, g_norm, batch, seq, n_meta):
    m = h.shape[0]
    hk = lg.shape[1]
    dk = hk // GLA_HEADS
    dv = g_norm.shape[1]
    hv = GLA_HEADS * dv
    assert h.shape[1] == 2 * hk + 2 * hv and m == batch * seq
    chunk = GLA_CHUNK
    n_chunks = (seq - n_meta) // chunk
    assert n_meta + n_chunks * chunk == seq
    kq = hk // dk
    kern = functools.partial(_gla_core_kernel, n_meta=n_meta, chunk=chunk, n_chunks=n_chunks)
    return pl.pallas_call(
        kern,
        grid=(batch, GLA_HEADS),
        in_specs=[
            pl.BlockSpec((seq, dk), lambda b, hh: (b, hh)),
            pl.BlockSpec((seq, dk), lambda b, hh: (b, kq + hh)),
            pl.BlockSpec((seq, dv), lambda b, hh: (b, 2 * hk // dv + hh)),
            pl.BlockSpec((seq, dv), lambda b, hh: (b, (2 * hk + hv) // dv + hh)),
            pl.BlockSpec((seq, dk), lambda b, hh: (b, hh)),
            pl.BlockSpec((1, dv), lambda b, hh: (0, 0)),
        ],
        out_specs=[
            pl.BlockSpec((seq, dv), lambda b, hh: (b, hh)),
            pl.BlockSpec((1, 1, dk, dv), lambda b, hh: (b, hh, 0, 0)),
        ],
        out_shape=[jax.ShapeDtypeStruct((m, hv), BF16),
                   jax.ShapeDtypeStruct((batch, GLA_HEADS, dk, dv), F32)],
        scratch_shapes=[pltpu.VMEM((dv, dk), F32)],
        compiler_params=_params("arbitrary", "arbitrary"),
        name="gla_core",
    )(h, h, h, h, lg, g_norm)


def _gla_step_kernel(h_ref, lg_ref, gn_ref, s_ref, o_ref, sn_ref):
    dk, dv = s_ref.shape[2], s_ref.shape[3]
    hk, hv = GLA_HEADS * dk, GLA_HEADS * dv
    scale = dk ** -0.5
    for hh in range(GLA_HEADS):
        q = h_ref[0, :, hh * dk:(hh + 1) * dk] * scale
        k = h_ref[0, :, hk + hh * dk:hk + (hh + 1) * dk]
        v = h_ref[0, :, 2 * hk + hh * dv:2 * hk + (hh + 1) * dv]
        r = h_ref[0, :, 2 * hk + hv + hh * dv:2 * hk + hv + (hh + 1) * dv]
        a = jnp.exp(lg_ref[0, :, hh * dk:(hh + 1) * dk])
        reps = dv // dk
        a_c = jnp.tile(_lane_col(a, dk), (1, reps))
        k_c = jnp.tile(_lane_col(k, dk), (1, reps))
        q_c = jnp.tile(_lane_col(q, dk), (1, reps))
        s_new = s_ref[0, hh] * a_c + k_c * v
        sn_ref[0, hh] = s_new
        o = jnp.sum(q_c * s_new, axis=0, keepdims=True)
        gated = _rms_norm(o, gn_ref[...]) * (r / (1.0 + jnp.exp(-r)))
        o_ref[0, :, hh * dv:(hh + 1) * dv] = gated.astype(o_ref.dtype)


def _gla_step(h, lg, g_norm, state):
    bs, n = h.shape
    _, _, dk, dv = state.shape
    hv = GLA_HEADS * dv
    o, s_new = pl.pallas_call(
        _gla_step_kernel,
        grid=(bs,),
        in_specs=[
            pl.BlockSpec((1, 1, n), lambda b: (b, 0, 0)),
            pl.BlockSpec((1, 1, lg.shape[1]), lambda b: (b, 0, 0)),
            pl.BlockSpec((1, dv), lambda b: (0, 0)),
            pl.BlockSpec((1, GLA_HEADS, dk, dv), lambda b: (b, 0, 0, 0)),
        ],
        out_specs=[
            pl.BlockSpec((1, 1, hv), lambda b: (b, 0, 0)),
            pl.BlockSpec((1, GLA_HEADS, dk, dv), lambda b: (b, 0, 0, 0)),
        ],
        out_shape=[jax.ShapeDtypeStruct((bs, 1, hv), BF16), jax.ShapeDtypeStruct(state.shape, F32)],
        compiler_params=_params("arbitrary"),
        name="gla_step",
    )(h.reshape(bs, 1, n), lg.reshape(bs, 1, -1), g_norm, state)
    return o.reshape(bs, hv), s_new


def _proj_ln_kernel(a_ref, w_ref, x_ref, g_ref, b_ref, o_ref):
    f = jnp.dot(a_ref[...], w_ref[...], preferred_element_type=F32)
    o_ref[...] = _layer_norm(DEEPNORM_ALPHA * x_ref[...] + f, g_ref[...], b_ref[...])


def _proj_ln(a, w, x, g, b):
    m, kk = a.shape
    d = w.shape[1]
    tm = _row_tile(m)
    return pl.pallas_call(
        _proj_ln_kernel,
        grid=(m // tm,),
        in_specs=[
            pl.BlockSpec((tm, kk), lambda i: (i, 0)),
            pl.BlockSpec((kk, d), lambda i: (0, 0)),
            pl.BlockSpec((tm, d), lambda i: (i, 0)),
            pl.BlockSpec((1, d), lambda i: (0, 0)),
            pl.BlockSpec((1, d), lambda i: (0, 0)),
        ],
        out_specs=pl.BlockSpec((tm, d), lambda i: (i, 0)),
        out_shape=jax.ShapeDtypeStruct((m, d), F32),
        compiler_params=_params("arbitrary"),
        name="proj_ln",
    )(a, w, x, g, b)


def _mlp_ln_kernel(x_ref, w1_ref, w2_ref, g_ref, b_ref, o_ref, xb_ref, acc_ref):
    f = pl.program_id(1)

    @pl.when(f == 0)
    def _():
        xb_ref[...] = x_ref[...].astype(BF16)
        acc_ref[...] = jnp.zeros_like(acc_ref)

    hid = jnp.maximum(jnp.dot(xb_ref[...], w1_ref[...], preferred_element_type=F32), 0.0)
    acc_ref[...] += jnp.dot((hid * hid).astype(BF16), w2_ref[...], preferred_element_type=F32)

    @pl.when(f == pl.num_programs(1) - 1)
    def _():
        o_ref[...] = _layer_norm(DEEPNORM_ALPHA * x_ref[...] + acc_ref[...], g_ref[...], b_ref[...])


def _mlp_ln(x, w1, w2, g, b):
    m, d = x.shape
    ff = w1.shape[1]
    tm, tf = _row_tile(m), _col_tile(ff)
    return pl.pallas_call(
        _mlp_ln_kernel,
        grid=(m // tm, ff // tf),
        in_specs=[
            pl.BlockSpec((tm, d), lambda i, f: (i, 0)),
            pl.BlockSpec((d, tf), lambda i, f: (0, f)),
            pl.BlockSpec((tf, d), lambda i, f: (f, 0)),
            pl.BlockSpec((1, d), lambda i, f: (0, 0)),
            pl.BlockSpec((1, d), lambda i, f: (0, 0)),
        ],
        out_specs=pl.BlockSpec((tm, d), lambda i, f: (i, 0)),
        out_shape=jax.ShapeDtypeStruct((m, d), F32),
        scratch_shapes=[pltpu.VMEM((tm, d), BF16), pltpu.VMEM((tm, d), F32)],
        compiler_params=_params("arbitrary", "arbitrary"),
        name="mlp_ln",
    )(x, w1, w2, g, b)


def _diff_proj_kernel(x_ref, w_ref, cos_ref, sin_up_ref, sin_dn_ref,
                      q_ref, k_ref, kb_ref, v_ref, vb_ref, *, tn, q_scale, rot_half):
    xb = x_ref[...].astype(BF16)
    nq = q_ref.shape[1]

    def project(seg, j):
        lo = seg * nq + j * tn
        return jnp.dot(xb, w_ref[:, lo:lo + tn], preferred_element_type=F32)

    def rope(t):
        groups = []
        for g in range(tn // LANES):
            tg = t[:, g * LANES:(g + 1) * LANES]
            groups.append(tg * cos_ref[...] + pltpu.roll(tg, LANES - rot_half, 1) * sin_up_ref[...]
                          + pltpu.roll(tg, rot_half, 1) * sin_dn_ref[...])
        return jnp.concatenate(groups, axis=1)

    for j in range(nq // tn):
        cols = slice(j * tn, (j + 1) * tn)
        q_ref[:, cols] = (rope(project(0, j)) * q_scale).astype(q_ref.dtype)
        kr = rope(project(1, j))
        k_ref[:, cols] = kr
        kb_ref[:, cols] = kr.astype(kb_ref.dtype)
        y = project(2, j)
        v_ref[:, cols] = y
        vb_ref[:, cols] = y.astype(vb_ref.dtype)


def _rope_tables(pos, dh):
    rot = dh // 4
    inv = ROPE_THETA ** (-jnp.arange(0, rot, 2, dtype=F32) / rot)
    ang = pos.astype(F32)[:, None] * inv[None, :]
    n = pos.shape[0]

    def lanes(first, second, fill):
        sub = jnp.concatenate([first, second, jnp.full((n, dh - rot), fill, F32)], axis=1)
        return jnp.tile(sub, (1, LANES // dh))

    zeros = jnp.zeros_like(ang)
    return (lanes(jnp.cos(ang), jnp.cos(ang), 1.0),
            lanes(-jnp.sin(ang), zeros, 0.0),
            lanes(zeros, jnp.sin(ang), 0.0))


def _diff_project(x, w, pos):
    m, d = x.shape
    n = w.shape[1]
    nq = n // 3
    dh = nq // (2 * DIFF_HEADS)
    assert m % pos.shape[0] == 0
    tm = _row_tile(pos.shape[0])
    pos_tiles = pos.shape[0] // tm
    cos_t, sin_up, sin_dn = _rope_tables(pos, dh)
    kern = functools.partial(_diff_proj_kernel, tn=_col_tile(nq), q_scale=dh ** -0.5, rot_half=dh // 8)
    tab_spec = pl.BlockSpec((tm, LANES), lambda i: (i % pos_tiles, 0))
    out_spec = pl.BlockSpec((tm, nq), lambda i: (i, 0))
    return pl.pallas_call(
        kern,
        grid=(m // tm,),
        in_specs=[
            pl.BlockSpec((tm, d), lambda i: (i, 0)),
            pl.BlockSpec((d, n), lambda i: (0, 0)),
            tab_spec, tab_spec, tab_spec,
        ],
        out_specs=[out_spec] * 5,
        out_shape=[jax.ShapeDtypeStruct((m, nq), BF16),
                   jax.ShapeDtypeStruct((m, nq), F32), jax.ShapeDtypeStruct((m, nq), BF16),
                   jax.ShapeDtypeStruct((m, nq), F32), jax.ShapeDtypeStruct((m, nq), BF16)],
        compiler_params=_params("arbitrary"),
        name="diff_proj",
    )(x, w, cos_t, sin_up, sin_dn)


def _lambda_value(lp):
    e1 = jnp.exp(jnp.sum(lp[0:1] * lp[1:2], axis=1, keepdims=True))
    e2 = jnp.exp(jnp.sum(lp[2:3] * lp[3:4], axis=1, keepdims=True))
    return e1 - e2 + LAM_INIT


def _diff_attn_kernel(lam_ref, q_ref, k_ref, v_ref, g_ref, o_ref, kp_scr, vp_scr, s_scr, p_scr,
                      *, n_meta, tile, n_tiles):
    lam = _lambda_value(lam_ref[...])
    seq, dq = k_ref.shape
    dv = v_ref.shape[1]
    pad = tile - n_meta
    rows = 2 * tile
    kp_scr[0:pad, :] = jnp.zeros((pad, dq), BF16)
    kp_scr[pad:pad + seq, :] = k_ref[...]
    vp_scr[0:pad, :] = jnp.zeros((pad, 2 * dv), BF16)
    vp_scr[pad:pad + seq, 0:dv] = v_ref[...]
    vp_scr[pad:pad + seq, dv:2 * dv] = jnp.ones((seq, dv), BF16)
    first_half = lax.broadcasted_iota(jnp.int32, (1, dq), 1) < dq // 2

    def stack(q):
        zero = jnp.zeros_like(q)
        return jnp.concatenate([jnp.where(first_half, q, zero), jnp.where(first_half, zero, q)], axis=0)

    def block_coords(length):
        ri = lax.broadcasted_iota(jnp.int32, (2 * length, tile), 0)
        ri = jnp.where(ri >= length, ri - length, ri)
        ci = lax.broadcasted_iota(jnp.int32, (2 * length, tile), 1)
        return ci, ri

    def finish(acc, length):
        ratio = acc[:, 0:dv] / acc[:, dv:2 * dv]
        o = ratio[:length] - lam * ratio[length:]
        return (_rms_norm(o, g_ref[...]) * (1.0 - LAM_INIT)).astype(o_ref.dtype)

    ci, ri = block_coords(n_meta)
    s = lax.dot_general(stack(q_ref[0:n_meta, :]), kp_scr[0:tile, :], _NT, preferred_element_type=F32)
    s = jnp.where(jnp.logical_and(ci >= pad, ci - pad <= ri), s, MASK_VALUE)
    p = jnp.exp(s - jnp.max(s, axis=1, keepdims=True))
    o_ref[0:n_meta, :] = finish(jnp.dot(p.astype(BF16), vp_scr[0:tile, :], preferred_element_type=F32), n_meta)

    ci, ri = block_coords(tile)
    causal = ci <= ri
    not_pad = ci >= pad
    row_chunk = 64
    for it in range(n_tiles):
        buf = it % 2
        qoff = n_meta + it * tile
        qs = stack(q_ref[qoff:qoff + tile, :])
        nb = it + 2
        for j in range(nb):
            s = lax.dot_general(qs, kp_scr[j * tile:(j + 1) * tile, :], _NT, preferred_element_type=F32)
            if j == 0:
                s = jnp.where(not_pad, s, MASK_VALUE)
            if j == nb - 1:
                s = jnp.where(causal, s, MASK_VALUE)
            s_scr[buf, :, j * tile:(j + 1) * tile] = s
        n_cols = nb * tile // LANES
        for rc in range(rows // row_chunk):
            rsl = slice(rc * row_chunk, (rc + 1) * row_chunk)
            mx = s_scr[buf, rsl, 0:LANES]
            for c in range(1, n_cols):
                mx = jnp.maximum(mx, s_scr[buf, rsl, c * LANES:(c + 1) * LANES])
            m_b = jnp.broadcast_to(jnp.max(mx, axis=1, keepdims=True), (row_chunk, LANES))
            for c in range(n_cols):
                csl = slice(c * LANES, (c + 1) * LANES)
                p_scr[buf, rsl, csl] = jnp.exp(s_scr[buf, rsl, csl] - m_b).astype(BF16)
        acc = jnp.dot(p_scr[buf, :, 0:nb * tile], vp_scr[0:nb * tile, :], preferred_element_type=F32)
        o_ref[qoff:qoff + tile, :] = finish(acc, tile)


def _diff_attention(q, k, v, lam_params, g_sub, batch, seq, n_meta):
    m, n = q.shape
    dv = g_sub.shape[1]
    tile = ATTN_TILE
    n_tiles = (seq - n_meta) // tile
    assert n_meta + n_tiles * tile == seq and n == DIFF_HEADS * dv
    assert n_meta <= tile and (tile - n_meta) % BF16_ROWS == 0
    padded = (n_tiles + 1) * tile
    kern = functools.partial(_diff_attn_kernel, n_meta=n_meta, tile=tile, n_tiles=n_tiles)
    head_spec = pl.BlockSpec((seq, dv), lambda b, hh: (b, hh))
    return pl.pallas_call(
        kern,
        grid=(batch, DIFF_HEADS),
        in_specs=[
            pl.BlockSpec(lam_params.shape, lambda b, hh: (0, 0)),
            head_spec, head_spec, head_spec,
            pl.BlockSpec((1, dv), lambda b, hh: (0, 0)),
        ],
        out_specs=head_spec,
        out_shape=jax.ShapeDtypeStruct((m, n), BF16),
        scratch_shapes=[pltpu.VMEM((padded, dv), BF16), pltpu.VMEM((padded, 2 * dv), BF16),
                        pltpu.VMEM((2, 2 * tile, padded), F32), pltpu.VMEM((2, 2 * tile, padded), BF16)],
        compiler_params=_params("arbitrary", "arbitrary"),
        name="diff_attn",
    )(lam_params, q, k, v, g_sub)


def _decode_attn_kernel(pt_ref, lam_ref, q_ref, kc_ref, vc_ref, g_ref, *refs, pages_per_step):
    del pt_ref
    k_refs = refs[:pages_per_step]
    v_refs = refs[pages_per_step:2 * pages_per_step]
    o_ref, m_scr, l_scr, acc_scr = refs[2 * pages_per_step:]
    step = pl.program_id(1)
    _, n_head, dv = q_ref.shape
    n_rows = 2 * n_head
    qt = q_ref[0]
    first_half = lax.broadcasted_iota(jnp.int32, (n_head, dv), 1) < dv // 2
    q_rows = jnp.concatenate([jnp.where(first_half, qt, 0.0), jnp.where(first_half, 0.0, qt)],
                             axis=0).astype(BF16)

    def own_head(n, n_valid):
        col = lax.broadcasted_iota(jnp.int32, (n_rows, n), 1)
        row = lax.broadcasted_iota(jnp.int32, (n_rows, n), 0)
        return jnp.logical_and(col % n_head == row % n_head, col < n_valid)

    def update(ks, vs, own):
        ss = [jnp.where(own, lax.dot_general(q_rows, k2.astype(BF16), _NT, preferred_element_type=F32) * LOG2_E,
                        MASK_VALUE) for k2 in ks]
        m_old = m_scr[...]
        m_new = jnp.maximum(m_old, jnp.max(functools.reduce(jnp.maximum, ss), axis=1, keepdims=True))
        alpha = jnp.exp2(m_old - m_new)
        ps = [jnp.exp2(s - m_new) for s in ss]
        m_scr[...] = m_new
        l_scr[...] = alpha * l_scr[...] + jnp.sum(functools.reduce(jnp.add, ps), axis=1, keepdims=True)
        pv = [jnp.dot(p.astype(BF16), v2.astype(BF16), preferred_element_type=F32) for p, v2 in zip(ps, vs)]
        acc_scr[...] = alpha * acc_scr[...] + functools.reduce(jnp.add, pv)

    @pl.when(step == 0)
    def _():
        m_scr[...] = jnp.full(m_scr.shape, MASK_VALUE, F32)
        l_scr[...] = jnp.zeros_like(l_scr)
        acc_scr[...] = jnp.zeros_like(acc_scr)

    n_tok = k_refs[0].shape[1]
    own_page = own_head(n_tok * n_head, n_tok * n_head)
    update([r[0].reshape(n_tok * n_head, dv) for r in k_refs],
           [r[0].reshape(n_tok * n_head, dv) for r in v_refs], own_page)

    @pl.when(step == pl.num_programs(1) - 1)
    def _():
        fill = jnp.zeros((LANES - n_head, dv), F32)
        update([jnp.concatenate([kc_ref[0], fill], axis=0)], [jnp.concatenate([vc_ref[0], fill], axis=0)],
               own_head(LANES, n_head))
        lam = _lambda_value(lam_ref[...])
        ratio = acc_scr[...] / l_scr[...]
        o = ratio[:n_head] - lam * ratio[n_head:]
        o_ref[0] = _rms_norm(o, g_ref[...]) * (1.0 - LAM_INIT)


def _decode_attention(q, k_cur, v_cur, cache_k, cache_v, page_table, lam_params, g_sub):
    bs, n_head, dv = q.shape
    n_pages = page_table.shape[1]
    pps = DECODE_PAGES_PER_STEP
    while n_pages % pps:
        pps //= 2
    page = cache_k.shape[1]
    row_spec = pl.BlockSpec((1, n_head, dv), lambda b, s, pt: (b, 0, 0))

    def page_spec(i):
        return pl.BlockSpec((1, page, n_head, dv), lambda b, s, pt: (pt[b, s * pps + i], 0, 0, 0))

    grid_spec = pltpu.PrefetchScalarGridSpec(
        num_scalar_prefetch=1,
        grid=(bs, n_pages // pps),
        in_specs=[
            pl.BlockSpec(lam_params.shape, lambda b, s, pt: (0, 0)),
            row_spec, row_spec, row_spec,
            pl.BlockSpec((1, dv), lambda b, s, pt: (0, 0)),
        ] + [page_spec(i) for i in range(pps)] + [page_spec(i) for i in range(pps)],
        out_specs=row_spec,
        scratch_shapes=[pltpu.VMEM((2 * n_head, 1), F32), pltpu.VMEM((2 * n_head, 1), F32),
                        pltpu.VMEM((2 * n_head, dv), F32)],
    )
    return pl.pallas_call(
        functools.partial(_decode_attn_kernel, pages_per_step=pps),
        grid_spec=grid_spec,
        out_shape=jax.ShapeDtypeStruct((bs, n_head, dv), F32),
        compiler_params=_params("arbitrary", "arbitrary"),
        name="decode_attn",
    )(page_table, lam_params, q, k_cur, v_cur, g_sub, *([cache_k] * pps), *([cache_v] * pps))


def kernel(x_prompt, x_sample, state_gla, cache_k, cache_v, page_table, meta_tokens,
           gla_w_in, gla_w_gate, gla_b_gate, gla_norm, gla_w_out,
           diff_w_in, diff_lambda, diff_norm, diff_w_out,
           mlp_w1, mlp_w2, ln_mix_g, ln_mix_b, ln_mlp_g, ln_mlp_b):
    batch, seq_new, d = x_prompt.shape
    n_meta = meta_tokens.shape[0]
    seq = n_meta + seq_new
    bs = x_sample.shape[0]
    assert x_sample.shape[1] == 1

    meta = jnp.broadcast_to(meta_tokens[None].astype(x_prompt.dtype), (batch, n_meta, d))
    x_p = jnp.concatenate([meta, x_prompt], axis=1).reshape(batch * seq, d)
    x_s = x_sample.reshape(bs, d)

    def row(vec):
        return vec.reshape(1, -1).astype(F32)

    hk = gla_w_gate.shape[2]
    rank = gla_w_gate.shape[1]
    n_main = gla_w_in.shape[2] - rank
    w_main = gla_w_in[0, :, :n_main].astype(BF16)
    w_glow = jnp.pad(gla_w_in[0, :, n_main:], ((0, 0), (0, LANES - rank))).astype(BF16)
    w_gate = jnp.pad(gla_w_gate[0], ((0, LANES - rank), (0, 0))).astype(BF16)
    b_gate = row(gla_b_gate[0])
    g_norm = row(gla_norm[0])
    w_out0 = gla_w_out[0].astype(BF16)

    h_p, lg_p = _gla_project(x_p, w_main, w_glow, w_gate, b_gate)
    h_s, lg_s = _gla_project(x_s, w_main, w_glow, w_gate, b_gate)
    o_p, st_p = _gla_core(h_p, lg_p, g_norm, batch, seq, n_meta)
    o_s, st_s = _gla_step(h_s, lg_s, g_norm, state_gla[0].astype(F32))

    def finish_layer(layer, o, x, w_out):
        x = _proj_ln(o, w_out, x, row(ln_mix_g[layer]), row(ln_mix_b[layer]))
        return _mlp_ln(x, mlp_w1[layer].astype(BF16), mlp_w2[layer].astype(BF16),
                       row(ln_mlp_g[layer]), row(ln_mlp_b[layer]))

    x_p = finish_layer(0, o_p, x_p, w_out0)
    x_s = finish_layer(0, o_s, x_s, w_out0)

    w_in1 = diff_w_in[0].astype(BF16)
    w_out1 = diff_w_out[0].astype(BF16)
    g_sub = row(diff_norm[0])
    lam_params = diff_lambda[0].astype(F32)
    past = page_table.shape[1] * PAGE_SIZE
    dv = g_sub.shape[1]

    q_p, k_p, kb_p, v_p, vb_p = _diff_project(x_p, w_in1, jnp.arange(seq))
    q_s, k_s, _, v_s, _ = _diff_project(x_s, w_in1, jnp.full((bs,), past))
    a_p = _diff_attention(q_p, kb_p, vb_p, lam_params, g_sub, batch, seq, n_meta)
    heads = (bs, DIFF_HEADS, dv)
    a_s = _decode_attention(q_s.astype(F32).reshape(heads), k_s.reshape(heads), v_s.reshape(heads),
                            cache_k[0], cache_v[0], page_table, lam_params, g_sub)
    a_s = a_s.reshape(bs, DIFF_HEADS * dv).astype(BF16)

    x_p = finish_layer(1, a_p, x_p, w_out1)
    x_s = finish_layer(1, a_s, x_s, w_out1)

    y_prompt = x_p.reshape(batch, seq, d)[:, n_meta:]
    y_sample = x_s.reshape(bs, 1, d)
    return (y_prompt, y_sample, st_p[None], st_s[None],
            k_p.reshape(1, batch, seq, DIFF_HEADS, dv), v_p.reshape(1, batch, seq, DIFF_HEADS, dv),
            k_s.reshape(1, bs, 1, DIFF_HEADS, dv), v_s.reshape(1, bs, 1, DIFF_HEADS, dv))
```

```python
import functools
import math

import jax
import jax.numpy as jnp
from jax import lax
from jax.experimental import pallas as pl
from jax.experimental.pallas import tpu as pltpu

F32 = jnp.float32
BF16 = jnp.bfloat16

DEPTH = 2
GLA_HEADS = 4
GLA_TAU = 16.0
DIFF_HEADS = 8
ROPE_THETA = 500000.0
PAGE_SIZE = 128
LN_EPS = 1e-5
DEEPNORM_ALPHA = (2 * DEPTH) ** 0.25
LAM_INIT = 0.8 - 0.6 * math.exp(-0.3 * 1)

LANES = 128
SUBLANES = 8
BF16_ROWS = 16
VMEM_LIMIT = 48 * 1024 * 1024

GLA_CHUNK = 128
ATTN_TILE = 256
MLP_FF_TILE = 512
HOST_TILES_PER_SEQ = 2
MASK_VALUE = -1e30
LOG2_E = 1.4426950408889634

_NT = (((1,), (1,)), ((), ()))
_TN = (((0,), (0,)), ((), ()))


def _row_tile(m):
    for cand in (688, 512, 256, 128, 64, 32, 16, 8):
        if m % cand == 0:
            return cand
    raise ValueError(f"row count {m} is not a multiple of 8")


def _col_tile(n):
    for cand in (512, 256, 128):
        if n % cand == 0:
            return cand
    raise ValueError(f"column count {n} is not a multiple of 128")


def _params(*sem):
    return pltpu.CompilerParams(dimension_semantics=sem, vmem_limit_bytes=VMEM_LIMIT)


def _layer_norm(y, g, b):
    mu = jnp.mean(y, axis=-1, keepdims=True)
    yc = y - mu
    var = jnp.mean(yc * yc, axis=-1, keepdims=True)
    return yc * lax.rsqrt(var + LN_EPS) * g + b


def _rms_norm(y, g):
    return y * lax.rsqrt(jnp.mean(y * y, axis=-1, keepdims=True) + LN_EPS) * g


def _lane_col(row, n):
    return jnp.transpose(jnp.broadcast_to(row, (n, n)))


def _gla_proj_kernel(x_ref, w_ref, wg_ref, wgate_ref, bgate_ref, h_ref, lg_ref, *, tn):
    xb = x_ref[...].astype(BF16)
    g_low = jnp.dot(xb, wg_ref[...], preferred_element_type=F32)
    z = jnp.dot(g_low.astype(BF16), wgate_ref[...], preferred_element_type=F32) + bgate_ref[...]
    log_sig = jnp.minimum(z, 0.0) - jnp.log(1.0 + jnp.exp(-jnp.abs(z)))
    lg_ref[...] = log_sig * (1.0 / GLA_TAU)
    for j in range(w_ref.shape[1] // tn):
        cols = slice(j * tn, (j + 1) * tn)
        h_ref[:, cols] = jnp.dot(xb, w_ref[:, cols], preferred_element_type=F32)


def _gla_project(x, w_main, w_glow, w_gate, b_gate):
    m, d = x.shape
    n = w_main.shape[1]
    ng = w_gate.shape[1]
    tm = _row_tile(m)
    return pl.pallas_call(
        functools.partial(_gla_proj_kernel, tn=_col_tile(n)),
        grid=(m // tm,),
        in_specs=[
            pl.BlockSpec((tm, d), lambda i: (i, 0)),
            pl.BlockSpec((d, n), lambda i: (0, 0)),
            pl.BlockSpec((d, LANES), lambda i: (0, 0)),
            pl.BlockSpec((LANES, ng), lambda i: (0, 0)),
            pl.BlockSpec((1, ng), lambda i: (0, 0)),
        ],
        out_specs=[
            pl.BlockSpec((tm, n), lambda i: (i, 0)),
            pl.BlockSpec((tm, ng), lambda i: (i, 0)),
        ],
        out_shape=[jax.ShapeDtypeStruct((m, n), F32), jax.ShapeDtypeStruct((m, ng), F32)],
        compiler_params=_params("arbitrary"),
        name="gla_proj",
    )(x, w_main, w_glow, w_gate, b_gate)


def _gla_core_kernel(q_ref, k_ref, v_ref, r_ref, lg_ref, gn_ref, o_ref, st_ref, s_scr,
                     *, n_meta, chunk, n_chunks):
    dk = q_ref.shape[1]
    scale = dk ** -0.5
    s_scr[...] = jnp.zeros_like(s_scr)

    def do_chunk(off, length):
        rows = pl.ds(off, length)
        ii = lax.broadcasted_iota(jnp.int32, (length, length), 0)
        jj = lax.broadcasted_iota(jnp.int32, (length, length), 1)
        causal = ii >= jj
        lg = lg_ref[rows, :]
        hi = lg.astype(BF16)
        rest = lg - hi.astype(F32)
        mid = rest.astype(BF16)
        lo = (rest - mid.astype(F32)).astype(BF16)
        parts = jnp.dot(causal.astype(BF16), jnp.concatenate([hi, mid, lo], axis=1),
                        preferred_element_type=F32)
        b = parts[:, 0:dk] + parts[:, dk:2 * dk] + parts[:, 2 * dk:3 * dk]
        q = q_ref[rows, :] * scale
        k = k_ref[rows, :]
        vb = v_ref[rows, :].astype(BF16)
        st = s_scr[...]
        o = lax.dot_general((q * jnp.exp(b)).astype(BF16), st.astype(BF16), _NT,
                            preferred_element_type=F32)
        b_mid = b[length // 2:length // 2 + 1, :]
        qs = (q * jnp.exp(b - b_mid)).astype(BF16)
        ks = (k * jnp.exp(b_mid - b)).astype(BF16)
        scores = lax.dot_general(qs, ks, _NT, preferred_element_type=F32)
        scores = jnp.where(causal, scores, 0.0)
        o = o + jnp.dot(scores.astype(BF16), vb, preferred_element_type=F32)
        b_last = b[length - 1:length, :]
        kd = (k * jnp.exp(b_last - b)).astype(BF16)
        s_scr[...] = st * jnp.exp(b_last) + lax.dot_general(vb, kd, _TN, preferred_element_type=F32)
        r = r_ref[rows, :]
        gated = _rms_norm(o, gn_ref[...]) * (r / (1.0 + jnp.exp(-r)))
        o_ref[rows, :] = gated.astype(o_ref.dtype)

    do_chunk(0, n_meta)
    for c in range(n_chunks):
        do_chunk(n_meta + c * chunk, chunk)
    st_ref[0, 0] = jnp.transpose(s_scr[...])


def _gla_core(h, lg, g_norm, batch, seq, n_meta):
    m = h.shape[0]
    hk = lg.shape[1]
    dk = hk // GLA_HEADS
    dv = g_norm.shape[1]
    hv = GLA_HEADS * dv
    assert h.shape[1] == 2 * hk + 2 * hv and m == batch * seq
    chunk = GLA_CHUNK
    n_chunks = (seq - n_meta) // chunk
    assert n_meta + n_chunks * chunk == seq
    kq = hk // dk
    kern = functools.partial(_gla_core_kernel, n_meta=n_meta, chunk=chunk, n_chunks=n_chunks)
    return pl.pallas_call(
        kern,
        grid=(batch, GLA_HEADS),
        in_specs=[
            pl.BlockSpec((seq, dk), lambda b, hh: (b, hh)),
            pl.BlockSpec((seq, dk), lambda b, hh: (b, kq + hh)),
            pl.BlockSpec((seq, dv), lambda b, hh: (b, 2 * hk // dv + hh)),
            pl.BlockSpec((seq, dv), lambda b, hh: (b, (2 * hk + hv) // dv + hh)),
            pl.BlockSpec((seq, dk), lambda b, hh: (b, hh)),
            pl.BlockSpec((1, dv), lambda b, hh: (0, 0)),
        ],
        out_specs=[
            pl.BlockSpec((seq, dv), lambda b, hh: (b, hh)),
            pl.BlockSpec((1, 1, dk, dv), lambda b, hh: (b, hh, 0, 0)),
        ],
        out_shape=[jax.ShapeDtypeStruct((m, hv), BF16),
                   jax.ShapeDtypeStruct((batch, GLA_HEADS, dk, dv), F32)],
        scratch_shapes=[pltpu.VMEM((dv, dk), F32)],
        compiler_params=_params("arbitrary", "arbitrary"),
        name="gla_core",
    )(h, h, h, h, lg, g_norm)


def _gla_step_kernel(h_ref, lg_ref, gn_ref, s_ref, o_ref, sn_ref):
    dk, dv = s_ref.shape[2], s_ref.shape[3]
    hk, hv = GLA_HEADS * dk, GLA_HEADS * dv
    scale = dk ** -0.5
    for hh in range(GLA_HEADS):
        q = h_ref[0, :, hh * dk:(hh + 1) * dk] * scale
        k = h_ref[0, :, hk + hh * dk:hk + (hh + 1) * dk]
        v = h_ref[0, :, 2 * hk + hh * dv:2 * hk + (hh + 1) * dv]
        r = h_ref[0, :, 2 * hk + hv + hh * dv:2 * hk + hv + (hh + 1) * dv]
        a = jnp.exp(lg_ref[0, :, hh * dk:(hh + 1) * dk])
        reps = dv // dk
        a_c = jnp.tile(_lane_col(a, dk), (1, reps))
        k_c = jnp.tile(_lane_col(k, dk), (1, reps))
        q_c = jnp.tile(_lane_col(q, dk), (1, reps))
        s_new = s_ref[0, hh] * a_c + k_c * v
        sn_ref[0, hh] = s_new
        o = jnp.sum(q_c * s_new, axis=0, keepdims=True)
        gated = _rms_norm(o, gn_ref[...]) * (r / (1.0 + jnp.exp(-r)))
        o_ref[0, :, hh * dv:(hh + 1) * dv] = gated.astype(o_ref.dtype)


def _gla_step(h, lg, g_norm, state):
    bs, n = h.shape
    _, _, dk, dv = state.shape
    hv = GLA_HEADS * dv
    o, s_new = pl.pallas_call(
        _gla_step_kernel,
        grid=(bs,),
        in_specs=[
            pl.BlockSpec((1, 1, n), lambda b: (b, 0, 0)),
            pl.BlockSpec((1, 1, lg.shape[1]), lambda b: (b, 0, 0)),
            pl.BlockSpec((1, dv), lambda b: (0, 0)),
            pl.BlockSpec((1, GLA_HEADS, dk, dv), lambda b: (b, 0, 0, 0)),
        ],
        out_specs=[
            pl.BlockSpec((1, 1, hv), lambda b: (b, 0, 0)),
            pl.BlockSpec((1, GLA_HEADS, dk, dv), lambda b: (b, 0, 0, 0)),
        ],
        out_shape=[jax.ShapeDtypeStruct((bs, 1, hv), BF16), jax.ShapeDtypeStruct(state.shape, F32)],
        compiler_params=_params("arbitrary"),
        name="gla_step",
    )(h.reshape(bs, 1, n), lg.reshape(bs, 1, -1), g_norm, state)
    return o.reshape(bs, hv), s_new


def _lambda_value(lp):
    e1 = jnp.exp(jnp.sum(lp[0:1] * lp[1:2], axis=1, keepdims=True))
    e2 = jnp.exp(jnp.sum(lp[2:3] * lp[3:4], axis=1, keepdims=True))
    return e1 - e2 + LAM_INIT


class _Decode:
    def __init__(self, q_ref, m_scr, l_scr, acc_scr):
        self.m_scr, self.l_scr, self.acc_scr = m_scr, l_scr, acc_scr
        _, self.n_head, self.dv = q_ref.shape
        qt = q_ref[0]
        first_half = lax.broadcasted_iota(jnp.int32, qt.shape, 1) < self.dv // 2
        self.q_rows = jnp.concatenate([jnp.where(first_half, qt, 0.0), jnp.where(first_half, 0.0, qt)],
                                      axis=0).astype(BF16)

    def init(self):
        self.m_scr[...] = jnp.full(self.m_scr.shape, MASK_VALUE, F32)
        self.l_scr[...] = jnp.zeros_like(self.l_scr)
        self.acc_scr[...] = jnp.zeros_like(self.acc_scr)

    def own_head(self, n, n_valid):
        shape = (2 * self.n_head, n)
        col = lax.broadcasted_iota(jnp.int32, shape, 1)
        row = lax.broadcasted_iota(jnp.int32, shape, 0)
        return jnp.logical_and(col % self.n_head == row % self.n_head, col < n_valid)

    def scores(self, ks, own):
        return [jnp.where(own, lax.dot_general(self.q_rows, k2.astype(BF16), _NT,
                                               preferred_element_type=F32) * LOG2_E, MASK_VALUE) for k2 in ks]

    def accumulate(self, ss, vs):
        m_old = self.m_scr[...]
        m_new = jnp.maximum(m_old, jnp.max(functools.reduce(jnp.maximum, ss), axis=1, keepdims=True))
        alpha = jnp.exp2(m_old - m_new)
        ps = [jnp.exp2(s - m_new) for s in ss]
        self.m_scr[...] = m_new
        self.l_scr[...] = alpha * self.l_scr[...] + jnp.sum(functools.reduce(jnp.add, ps), axis=1, keepdims=True)
        pv = [jnp.dot(p.astype(BF16), v2.astype(BF16), preferred_element_type=F32) for p, v2 in zip(ps, vs)]
        self.acc_scr[...] = alpha * self.acc_scr[...] + functools.reduce(jnp.add, pv)

    def page_scores(self, k_refs):
        n = k_refs[0].shape[1] * self.n_head
        return self.scores([r[0].reshape(n, self.dv) for r in k_refs], self.own_head(n, n))

    def accumulate_pages(self, ss, v_refs):
        n = v_refs[0].shape[1] * self.n_head
        self.accumulate(ss, [r[0].reshape(n, self.dv) for r in v_refs])

    def finish(self, kc_ref, vc_ref, lam_ref, g_ref, o_ref):
        fill = jnp.zeros((LANES - self.n_head, self.dv), F32)
        ss = self.scores([jnp.concatenate([kc_ref[0], fill], axis=0)], self.own_head(LANES, self.n_head))
        self.accumulate(ss, [jnp.concatenate([vc_ref[0], fill], axis=0)])
        ratio = self.acc_scr[...] / self.l_scr[...]
        o = ratio[:self.n_head] - _lambda_value(lam_ref[...]) * ratio[self.n_head:]
        o_ref[0] = _rms_norm(o, g_ref[...]) * (1.0 - LAM_INIT)


def _finish_kernel(*refs, n_pages_step, tiles_per_seq):
    hosting = n_pages_step > 0
    if hosting:
        refs = refs[1:]
    a_ref, wo_ref, x_ref, g1_ref, b1_ref, w1_ref, w2_ref, g2_ref, b2_ref = refs[:9]
    refs = refs[9:]
    if hosting:
        lam_ref, q_ref, kc_ref, vc_ref, gs_ref = refs[:5]
        k_refs = refs[5:5 + n_pages_step]
        v_refs = refs[5 + n_pages_step:5 + 2 * n_pages_step]
        o_ref, od_ref, x1_scr, xb_scr, acc_scr, m_scr, l_scr, dacc_scr = refs[5 + 2 * n_pages_step:]
        dec = _Decode(q_ref, m_scr, l_scr, dacc_scr)
    else:
        o_ref, x1_scr, xb_scr, acc_scr = refs
    f = pl.program_id(1)
    last_f = pl.num_programs(1) - 1
    tile_in_seq = pl.program_id(0) % tiles_per_seq

    @pl.when(f == 0)
    def _():
        proj = jnp.dot(a_ref[...], wo_ref[...], preferred_element_type=F32)
        x1 = _layer_norm(DEEPNORM_ALPHA * x_ref[...] + proj, g1_ref[...], b1_ref[...])
        x1_scr[...] = x1
        xb_scr[...] = x1.astype(BF16)
        acc_scr[...] = jnp.zeros_like(acc_scr)

    if hosting:
        pl.when(jnp.logical_and(f == 0, tile_in_seq == 0))(dec.init)

    if hosting:
        page_scores = dec.page_scores(k_refs)
    hid = jnp.maximum(jnp.dot(xb_scr[...], w1_ref[...], preferred_element_type=F32), 0.0)
    if hosting:
        dec.accumulate_pages(page_scores, v_refs)
    acc_scr[...] += jnp.dot((hid * hid).astype(BF16), w2_ref[...], preferred_element_type=F32)

    @pl.when(f == last_f)
    def _():
        o_ref[...] = _layer_norm(DEEPNORM_ALPHA * x1_scr[...] + acc_scr[...], g2_ref[...], b2_ref[...])

    if hosting:
        @pl.when(jnp.logical_and(f == last_f, tile_in_seq == tiles_per_seq - 1))
        def _():
            dec.finish(kc_ref, vc_ref, lam_ref, gs_ref, od_ref)


def _finish_layer(a, w_out, x, ln1, w1, w2, ln2, decode=None):
    m, kk = a.shape
    d = w_out.shape[1]
    ff = w1.shape[1]
    tf = min(MLP_FF_TILE, ff)
    n_f = ff // tf
    tps = HOST_TILES_PER_SEQ
    if decode is None:
        tm, pps = _row_tile(m), 0
        n_tiles = m // tm
    else:
        q, k_cur, v_cur, cache_k, cache_v, page_table, lam_params, g_sub = decode
        n_seqs, n_head, dv = q.shape
        n_tiles = n_seqs * tps
        tm = pl.cdiv(pl.cdiv(m, n_tiles), BF16_ROWS) * BF16_ROWS
        n_pages, page = page_table.shape[1], cache_k.shape[1]
        assert n_pages % (tps * n_f) == 0 and pl.cdiv(m, tm) == n_tiles
        pps = n_pages // (tps * n_f)
    row_spec = pl.BlockSpec((tm, d), lambda i, f, *_: (i, 0))
    vec_spec = pl.BlockSpec((1, d), lambda i, f, *_: (0, 0))
    in_specs = [
        pl.BlockSpec((tm, kk), lambda i, f, *_: (i, 0)),
        pl.BlockSpec((kk, d), lambda i, f, *_: (0, 0)),
        row_spec, vec_spec, vec_spec,
        pl.BlockSpec((d, tf), lambda i, f, *_: (0, f)),
        pl.BlockSpec((tf, d), lambda i, f, *_: (f, 0)),
        vec_spec, vec_spec,
    ]
    args = [a, w_out, x, *ln1, w1, w2, *ln2]
    out_specs = [row_spec]
    out_shape = [jax.ShapeDtypeStruct((m, d), F32)]
    scratch = [pltpu.VMEM((tm, d), F32), pltpu.VMEM((tm, d), BF16), pltpu.VMEM((tm, d), F32)]
    prefetch = []
    if decode is not None:
        head_spec = pl.BlockSpec((1, n_head, dv), lambda i, f, pt: (i // tps, 0, 0))

        def page_spec(s):
            return pl.BlockSpec((1, page, n_head, dv),
                                lambda i, f, pt: (pt[i // tps, ((i % tps) * n_f + f) * pps + s], 0, 0, 0))

        in_specs += [pl.BlockSpec(lam_params.shape, lambda i, f, pt: (0, 0)), head_spec, head_spec, head_spec,
                     pl.BlockSpec((1, dv), lambda i, f, pt: (0, 0))]
        in_specs += [page_spec(s) for s in range(pps)] * 2
        args += [lam_params, q, k_cur, v_cur, g_sub] + [cache_k] * pps + [cache_v] * pps
        out_specs.append(head_spec)
        out_shape.append(jax.ShapeDtypeStruct((n_seqs, n_head, dv), F32))
        scratch += [pltpu.VMEM((2 * n_head, 1), F32), pltpu.VMEM((2 * n_head, 1), F32),
                    pltpu.VMEM((2 * n_head, dv), F32)]
        prefetch = [page_table]
    outs = pl.pallas_call(
        functools.partial(_finish_kernel, n_pages_step=pps, tiles_per_seq=tps),
        grid_spec=pltpu.PrefetchScalarGridSpec(
            num_scalar_prefetch=len(prefetch), grid=(n_tiles, n_f),
            in_specs=in_specs, out_specs=out_specs, scratch_shapes=scratch),
        out_shape=out_shape,
        compiler_params=_params("arbitrary", "arbitrary"),
        name="finish_layer" if decode is None else "finish_layer_host",
    )(*prefetch, *args)
    return outs[0] if decode is None else tuple(outs)


def _diff_proj_kernel(x_ref, w_ref, cos_ref, sin_up_ref, sin_dn_ref,
                      q_ref, k_ref, kb_ref, v_ref, vb_ref, *, tn, q_scale, rot_half):
    xb = x_ref[...].astype(BF16)
    nq = q_ref.shape[1]

    def project(seg, j):
        lo = seg * nq + j * tn
        return jnp.dot(xb, w_ref[:, lo:lo + tn], preferred_element_type=F32)

    def rope(t):
        groups = []
        for g in range(tn // LANES):
            tg = t[:, g * LANES:(g + 1) * LANES]
            groups.append(tg * cos_ref[...] + pltpu.roll(tg, LANES - rot_half, 1) * sin_up_ref[...]
                          + pltpu.roll(tg, rot_half, 1) * sin_dn_ref[...])
        return jnp.concatenate(groups, axis=1)

    for j in range(nq // tn):
        cols = slice(j * tn, (j + 1) * tn)
        q_ref[:, cols] = (rope(project(0, j)) * q_scale).astype(q_ref.dtype)
        kr = rope(project(1, j))
        k_ref[:, cols] = kr
        kb_ref[:, cols] = kr.astype(kb_ref.dtype)
        y = project(2, j)
        v_ref[:, cols] = y
        vb_ref[:, cols] = y.astype(vb_ref.dtype)


def _rope_tables(pos, dh):
    rot = dh // 4
    inv = ROPE_THETA ** (-jnp.arange(0, rot, 2, dtype=F32) / rot)
    ang = pos.astype(F32)[:, None] * inv[None, :]
    n = pos.shape[0]

    def lanes(first, second, fill):
        sub = jnp.concatenate([first, second, jnp.full((n, dh - rot), fill, F32)], axis=1)
        return jnp.tile(sub, (1, LANES // dh))

    zeros = jnp.zeros_like(ang)
    return (lanes(jnp.cos(ang), jnp.cos(ang), 1.0),
            lanes(-jnp.sin(ang), zeros, 0.0),
            lanes(zeros, jnp.sin(ang), 0.0))


def _diff_project(x, w, pos):
    m, d = x.shape
    n = w.shape[1]
    nq = n // 3
    dh = nq // (2 * DIFF_HEADS)
    assert m % pos.shape[0] == 0
    tm = _row_tile(pos.shape[0])
    pos_tiles = pos.shape[0] // tm
    cos_t, sin_up, sin_dn = _rope_tables(pos, dh)
    kern = functools.partial(_diff_proj_kernel, tn=_col_tile(nq), q_scale=dh ** -0.5, rot_half=dh // 8)
    tab_spec = pl.BlockSpec((tm, LANES), lambda i: (i % pos_tiles, 0))
    out_spec = pl.BlockSpec((tm, nq), lambda i: (i, 0))
    return pl.pallas_call(
        kern,
        grid=(m // tm,),
        in_specs=[
            pl.BlockSpec((tm, d), lambda i: (i, 0)),
            pl.BlockSpec((d, n), lambda i: (0, 0)),
            tab_spec, tab_spec, tab_spec,
        ],
        out_specs=[out_spec] * 5,
        out_shape=[jax.ShapeDtypeStruct((m, nq), BF16),
                   jax.ShapeDtypeStruct((m, nq), F32), jax.ShapeDtypeStruct((m, nq), BF16),
                   jax.ShapeDtypeStruct((m, nq), F32), jax.ShapeDtypeStruct((m, nq), BF16)],
        compiler_params=_params("arbitrary"),
        name="diff_proj",
    )(x, w, cos_t, sin_up, sin_dn)


def _diff_attn_kernel(lam_ref, q_ref, k_ref, v_ref, g_ref, o_ref, kp_scr, vp_scr, s_scr, p_scr,
                      *, n_meta, tile, n_tiles):
    lam = _lambda_value(lam_ref[...])
    seq, dq = k_ref.shape
    dv = v_ref.shape[1]
    first = LANES
    pad = first - n_meta
    rows = 2 * tile
    kp_scr[0:pad, :] = jnp.zeros((pad, dq), BF16)
    kp_scr[pad:pad + seq, :] = k_ref[...]
    vp_scr[0:pad, :] = jnp.zeros((pad, 2 * dv), BF16)
    vp_scr[pad:pad + seq, 0:dv] = v_ref[...]
    vp_scr[pad:pad + seq, dv:2 * dv] = jnp.ones((seq, dv), BF16)
    first_half = lax.broadcasted_iota(jnp.int32, (1, dq), 1) < dq // 2

    def stack(q):
        zero = jnp.zeros_like(q)
        return jnp.concatenate([jnp.where(first_half, q, zero), jnp.where(first_half, zero, q)], axis=0)

    def block_coords(length, width):
        ri = lax.broadcasted_iota(jnp.int32, (2 * length, width), 0)
        ri = jnp.where(ri >= length, ri - length, ri)
        ci = lax.broadcasted_iota(jnp.int32, (2 * length, width), 1)
        return ci, ri

    def finish(acc, length):
        ratio = acc[:, 0:dv] / acc[:, dv:2 * dv]
        o = ratio[:length] - lam * ratio[length:]
        return (_rms_norm(o, g_ref[...]) * (1.0 - LAM_INIT)).astype(o_ref.dtype)

    ci, ri = block_coords(n_meta, first)
    s = lax.dot_general(stack(q_ref[0:n_meta, :]), kp_scr[0:first, :], _NT, preferred_element_type=F32)
    s = jnp.where(jnp.logical_and(ci >= pad, ci - pad <= ri), s, MASK_VALUE)
    p = jnp.exp(s - jnp.max(s, axis=1, keepdims=True))
    o_ref[0:n_meta, :] = finish(jnp.dot(p.astype(BF16), vp_scr[0:first, :], preferred_element_type=F32), n_meta)

    ci, ri = block_coords(tile, tile)
    causal = ci <= ri
    not_pad = block_coords(tile, first)[0] >= pad
    row_chunk = 64

    def key_block(j):
        return (0, first) if j == 0 else (first + (j - 1) * tile, first + j * tile)

    def score_tile(it):
        qoff = n_meta + it * tile
        qs = stack(q_ref[qoff:qoff + tile, :])
        for j in range(it + 2):
            lo, hi = key_block(j)
            s = lax.dot_general(qs, kp_scr[lo:hi, :], _NT, preferred_element_type=F32)
            if j == 0:
                s = jnp.where(not_pad, s, MASK_VALUE)
            if j == it + 1:
                s = jnp.where(causal, s, MASK_VALUE)
            s_scr[it % 2, :, lo:hi] = s

    def softmax_tile(it):
        buf = it % 2
        qoff = n_meta + it * tile
        width = key_block(it + 1)[1]
        n_cols = width // LANES
        for rc in range(rows // row_chunk):
            rsl = slice(rc * row_chunk, (rc + 1) * row_chunk)
            mx = s_scr[buf, rsl, 0:LANES]
            for c in range(1, n_cols):
                mx = jnp.maximum(mx, s_scr[buf, rsl, c * LANES:(c + 1) * LANES])
            m_b = jnp.broadcast_to(jnp.max(mx, axis=1, keepdims=True), (row_chunk, LANES))
            for c in range(n_cols):
                csl = slice(c * LANES, (c + 1) * LANES)
                p_scr[buf, rsl, csl] = jnp.exp(s_scr[buf, rsl, csl] - m_b).astype(BF16)
        acc = jnp.dot(p_scr[buf, :, 0:width], vp_scr[0:width, :], preferred_element_type=F32)
        o_ref[qoff:qoff + tile, :] = finish(acc, tile)

    score_tile(0)
    for it in range(n_tiles):
        if it + 1 < n_tiles:
            score_tile(it + 1)
        softmax_tile(it)


def _diff_attention(q, k, v, lam_params, g_sub, batch, seq, n_meta):
    m, n = q.shape
    dv = g_sub.shape[1]
    tile = ATTN_TILE
    n_tiles = (seq - n_meta) // tile
    assert n_meta + n_tiles * tile == seq and n == DIFF_HEADS * dv
    assert n_meta <= LANES and (LANES - n_meta) % BF16_ROWS == 0
    padded = LANES + n_tiles * tile
    kern = functools.partial(_diff_attn_kernel, n_meta=n_meta, tile=tile, n_tiles=n_tiles)
    head_spec = pl.BlockSpec((seq, dv), lambda b, hh: (b, hh))
    return pl.pallas_call(
        kern,
        grid=(batch, DIFF_HEADS),
        in_specs=[
            pl.BlockSpec(lam_params.shape, lambda b, hh: (0, 0)),
            head_spec, head_spec, head_spec,
            pl.BlockSpec((1, dv), lambda b, hh: (0, 0)),
        ],
        out_specs=head_spec,
        out_shape=jax.ShapeDtypeStruct((m, n), BF16),
        scratch_shapes=[pltpu.VMEM((padded, dv), BF16), pltpu.VMEM((padded, 2 * dv), BF16),
                        pltpu.VMEM((2, 2 * tile, padded), F32), pltpu.VMEM((2, 2 * tile, padded), BF16)],
        compiler_params=_params("arbitrary", "arbitrary"),
        name="diff_attn",
    )(lam_params, q, k, v, g_sub)


def kernel(x_prompt, x_sample, state_gla, cache_k, cache_v, page_table, meta_tokens,
           gla_w_in, gla_w_gate, gla_b_gate, gla_norm, gla_w_out,
           diff_w_in, diff_lambda, diff_norm, diff_w_out,
           mlp_w1, mlp_w2, ln_mix_g, ln_mix_b, ln_mlp_g, ln_mlp_b):
    batch, seq_new, d = x_prompt.shape
    n_meta = meta_tokens.shape[0]
    seq = n_meta + seq_new
    bs = x_sample.shape[0]
    assert x_sample.shape[1] == 1 and bs % 2 == 0

    meta = jnp.broadcast_to(meta_tokens[None].astype(x_prompt.dtype), (batch, n_meta, d))
    x_p = jnp.concatenate([meta, x_prompt], axis=1).reshape(batch * seq, d)
    x_s = x_sample.reshape(bs, d)

    def row(vec):
        return vec.reshape(1, -1).astype(F32)

    def finish_layer(layer, o, x, w_out, decode=None):
        return _finish_layer(o, w_out, x, (row(ln_mix_g[layer]), row(ln_mix_b[layer])),
                             mlp_w1[layer].astype(BF16), mlp_w2[layer].astype(BF16),
                             (row(ln_mlp_g[layer]), row(ln_mlp_b[layer])), decode)

    rank = gla_w_gate.shape[1]
    n_main = gla_w_in.shape[2] - rank
    w_main = gla_w_in[0, :, :n_main].astype(BF16)
    w_glow = jnp.pad(gla_w_in[0, :, n_main:], ((0, 0), (0, LANES - rank))).astype(BF16)
    w_gate = jnp.pad(gla_w_gate[0], ((0, LANES - rank), (0, 0))).astype(BF16)
    b_gate = row(gla_b_gate[0])
    g_norm = row(gla_norm[0])
    w_out0 = gla_w_out[0].astype(BF16)
    w_in1 = diff_w_in[0].astype(BF16)
    w_out1 = diff_w_out[0].astype(BF16)
    g_sub = row(diff_norm[0])
    lam_params = diff_lambda[0].astype(F32)
    dv = g_sub.shape[1]
    past = page_table.shape[1] * PAGE_SIZE

    h_s, lg_s = _gla_project(x_s, w_main, w_glow, w_gate, b_gate)
    o_s, st_s = _gla_step(h_s, lg_s, g_norm, state_gla[0].astype(F32))
    x_s = finish_layer(0, o_s, x_s, w_out0)
    q_s, k_s, _, v_s, _ = _diff_project(x_s, w_in1, jnp.full((bs,), past))
    heads = (bs, DIFF_HEADS, dv)
    q_s3, k_s3, v_s3 = q_s.astype(F32).reshape(heads), k_s.reshape(heads), v_s.reshape(heads)

    def hosted(lo):
        hi = lo + bs // 2
        return (q_s3[lo:hi], k_s3[lo:hi], v_s3[lo:hi], cache_k[0], cache_v[0], page_table[lo:hi],
                lam_params, g_sub)

    h_p, lg_p = _gla_project(x_p, w_main, w_glow, w_gate, b_gate)
    o_p, st_p = _gla_core(h_p, lg_p, g_norm, batch, seq, n_meta)
    x_p, a_s_lo = finish_layer(0, o_p, x_p, w_out0, hosted(0))

    q_p, k_p, kb_p, v_p, vb_p = _diff_project(x_p, w_in1, jnp.arange(seq))
    a_p = _diff_attention(q_p, kb_p, vb_p, lam_params, g_sub, batch, seq, n_meta)
    x_p, a_s_hi = finish_layer(1, a_p, x_p, w_out1, hosted(bs // 2))

    a_s = jnp.concatenate([a_s_lo, a_s_hi], axis=0).reshape(bs, DIFF_HEADS * dv).astype(BF16)
    x_s = finish_layer(1, a_s, x_s, w_out1)

    y_prompt = x_p.reshape(batch, seq, d)[:, n_meta:]
    y_sample = x_s.reshape(bs, 1, d)
    return (y_prompt, y_sample, st_p[None], st_s[None],
            k_p.reshape(1, batch, seq, DIFF_HEADS, dv), v_p.reshape(1, batch, seq, DIFF_HEADS, dv),
            k_s.reshape(1, bs, 1, DIFF_HEADS, dv), v_s.reshape(1, bs, 1, DIFF_HEADS, dv))
```

```python
import functools
import math

import jax
import jax.numpy as jnp
from jax import lax
from jax.experimental import pallas as pl
from jax.experimental.pallas import tpu as pltpu

F32 = jnp.float32
BF16 = jnp.bfloat16

DEPTH = 2
GLA_HEADS = 4
GLA_TAU = 16.0
DIFF_HEADS = 8
ROPE_THETA = 500000.0
PAGE_SIZE = 128
LN_EPS = 1e-5
DEEPNORM_ALPHA = (2 * DEPTH) ** 0.25
LAM_INIT = 0.8 - 0.6 * math.exp(-0.3 * 1)

LANES = 128
SUBLANES = 8
BF16_ROWS = 16
VMEM_LIMIT = 48 * 1024 * 1024

GLA_CHUNK = 128
GLA_HEADS_PER_STEP = 2
ATTN_TILE = 256
MLP_FF_TILE = 512
HOST_TILES_PER_SEQ = 1
VMEM_LIMIT_HOST = 60000 * 1024
MASK_VALUE = -1e30
LOG2_E = 1.4426950408889634

_NT = (((1,), (1,)), ((), ()))
_TN = (((0,), (0,)), ((), ()))


def _row_tile(m):
    for cand in (688, 512, 256, 128, 64, 32, 16, 8):
        if m % cand == 0:
            return cand
    raise ValueError(f"row count {m} is not a multiple of 8")


def _col_tile(n):
    for cand in (512, 256, 128):
        if n % cand == 0:
            return cand
    raise ValueError(f"column count {n} is not a multiple of 128")


def _params(*sem, limit=VMEM_LIMIT):
    return pltpu.CompilerParams(dimension_semantics=sem, vmem_limit_bytes=limit)


def _layer_norm(y, g, b):
    mu = jnp.mean(y, axis=-1, keepdims=True)
    yc = y - mu
    var = jnp.mean(yc * yc, axis=-1, keepdims=True)
    return yc * lax.rsqrt(var + LN_EPS) * g + b


def _rms_norm(y, g):
    return y * lax.rsqrt(jnp.mean(y * y, axis=-1, keepdims=True) + LN_EPS) * g


def _lane_col(row, n):
    return jnp.transpose(jnp.broadcast_to(row, (n, n)))


def _gla_proj_kernel(x_ref, w_ref, wg_ref, wgate_ref, bgate_ref, h_ref, lg_ref, *, tn):
    xb = x_ref[...].astype(BF16)
    g_low = jnp.dot(xb, wg_ref[...], preferred_element_type=F32)
    z = jnp.dot(g_low.astype(BF16), wgate_ref[...], preferred_element_type=F32) + bgate_ref[...]
    log_sig = jnp.minimum(z, 0.0) - jnp.log(1.0 + jnp.exp(-jnp.abs(z)))
    lg_ref[...] = log_sig * (1.0 / GLA_TAU)
    for j in range(w_ref.shape[1] // tn):
        cols = slice(j * tn, (j + 1) * tn)
        h_ref[:, cols] = jnp.dot(xb, w_ref[:, cols], preferred_element_type=F32)


def _gla_project(x, w_main, w_glow, w_gate, b_gate):
    m, d = x.shape
    n = w_main.shape[1]
    ng = w_gate.shape[1]
    tm = _row_tile(m)
    return pl.pallas_call(
        functools.partial(_gla_proj_kernel, tn=_col_tile(n)),
        grid=(m // tm,),
        in_specs=[
            pl.BlockSpec((tm, d), lambda i: (i, 0)),
            pl.BlockSpec((d, n), lambda i: (0, 0)),
            pl.BlockSpec((d, LANES), lambda i: (0, 0)),
            pl.BlockSpec((LANES, ng), lambda i: (0, 0)),
            pl.BlockSpec((1, ng), lambda i: (0, 0)),
        ],
        out_specs=[
            pl.BlockSpec((tm, n), lambda i: (i, 0)),
            pl.BlockSpec((tm, ng), lambda i: (i, 0)),
        ],
        out_shape=[jax.ShapeDtypeStruct((m, n), F32), jax.ShapeDtypeStruct((m, ng), F32)],
        compiler_params=_params("arbitrary"),
        name="gla_proj",
    )(x, w_main, w_glow, w_gate, b_gate)


def _gla_core_kernel(q_ref, k_ref, v_ref, r_ref, lg_ref, gn_ref, o_ref, st_ref, s_scr,
                     *, n_meta, chunk, n_chunks, dk, dv):
    wk = q_ref.shape[1]
    heads = range(wk // dk)
    scale = dk ** -0.5
    s_scr[...] = jnp.zeros_like(s_scr)

    def do_chunk(off, length):
        rows = pl.ds(off, length)
        ii = lax.broadcasted_iota(jnp.int32, (length, length), 0)
        jj = lax.broadcasted_iota(jnp.int32, (length, length), 1)
        causal = ii >= jj
        lg = lg_ref[rows, :]
        hi = lg.astype(BF16)
        rest = lg - hi.astype(F32)
        mid = rest.astype(BF16)
        lo = (rest - mid.astype(F32)).astype(BF16)
        parts = jnp.dot(causal.astype(BF16), jnp.concatenate([hi, mid, lo], axis=1),
                        preferred_element_type=F32)
        b = parts[:, 0:wk] + parts[:, wk:2 * wk] + parts[:, 2 * wk:3 * wk]
        q = q_ref[rows, :] * scale
        k = k_ref[rows, :]
        b_mid = b[length // 2:length // 2 + 1, :]
        b_last = b[length - 1:length, :]
        q_in = (q * jnp.exp(b)).astype(BF16)
        qs = (q * jnp.exp(b - b_mid)).astype(BF16)
        ks = (k * jnp.exp(b_mid - b)).astype(BF16)
        kd = (k * jnp.exp(b_last - b)).astype(BF16)
        decay = jnp.exp(b_last)
        vb = v_ref[rows, :].astype(BF16)

        def kcols(hh):
            return slice(hh * dk, (hh + 1) * dk)

        def vcols(hh):
            return slice(hh * dv, (hh + 1) * dv)

        states = [s_scr[hh] for hh in heads]
        o_inter = [lax.dot_general(q_in[:, kcols(hh)], states[hh].astype(BF16), _NT,
                                   preferred_element_type=F32) for hh in heads]
        scores = [lax.dot_general(qs[:, kcols(hh)], ks[:, kcols(hh)], _NT, preferred_element_type=F32)
                  for hh in heads]
        scores = [jnp.where(causal, s, 0.0).astype(BF16) for s in scores]
        o_intra = [jnp.dot(scores[hh], vb[:, vcols(hh)], preferred_element_type=F32) for hh in heads]
        grown = [lax.dot_general(vb[:, vcols(hh)], kd[:, kcols(hh)], _TN, preferred_element_type=F32)
                 for hh in heads]
        for hh in heads:
            s_scr[hh] = states[hh] * decay[:, kcols(hh)] + grown[hh]
            r = r_ref[rows, vcols(hh)]
            gated = _rms_norm(o_inter[hh] + o_intra[hh], gn_ref[...]) * (r / (1.0 + jnp.exp(-r)))
            o_ref[rows, vcols(hh)] = gated.astype(o_ref.dtype)

    do_chunk(0, n_meta)
    for c in range(n_chunks):
        do_chunk(n_meta + c * chunk, chunk)
    for hh in heads:
        st_ref[0, hh] = jnp.transpose(s_scr[hh])


def _gla_core(h, lg, g_norm, batch, seq, n_meta):
    m = h.shape[0]
    hk = lg.shape[1]
    dk = hk // GLA_HEADS
    dv = g_norm.shape[1]
    hv = GLA_HEADS * dv
    assert h.shape[1] == 2 * hk + 2 * hv and m == batch * seq
    chunk = GLA_CHUNK
    n_chunks = (seq - n_meta) // chunk
    assert n_meta + n_chunks * chunk == seq
    hps = GLA_HEADS_PER_STEP
    groups = GLA_HEADS // hps
    wk, wv = hps * dk, hps * dv
    kern = functools.partial(_gla_core_kernel, n_meta=n_meta, chunk=chunk, n_chunks=n_chunks, dk=dk, dv=dv)
    return pl.pallas_call(
        kern,
        grid=(batch, groups),
        in_specs=[
            pl.BlockSpec((seq, wk), lambda b, g: (b, g)),
            pl.BlockSpec((seq, wk), lambda b, g: (b, groups + g)),
            pl.BlockSpec((seq, wv), lambda b, g: (b, 2 * hk // wv + g)),
            pl.BlockSpec((seq, wv), lambda b, g: (b, (2 * hk + hv) // wv + g)),
            pl.BlockSpec((seq, wk), lambda b, g: (b, g)),
            pl.BlockSpec((1, dv), lambda b, g: (0, 0)),
        ],
        out_specs=[
            pl.BlockSpec((seq, wv), lambda b, g: (b, g)),
            pl.BlockSpec((1, hps, dk, dv), lambda b, g: (b, g, 0, 0)),
        ],
        out_shape=[jax.ShapeDtypeStruct((m, hv), BF16),
                   jax.ShapeDtypeStruct((batch, GLA_HEADS, dk, dv), F32)],
        scratch_shapes=[pltpu.VMEM((hps, dv, dk), F32)],
        compiler_params=_params("arbitrary", "arbitrary"),
        name="gla_core",
    )(h, h, h, h, lg, g_norm)


def _gla_step_kernel(h_ref, lg_ref, gn_ref, s_ref, o_ref, sn_ref):
    dk, dv = s_ref.shape[2], s_ref.shape[3]
    hk, hv = GLA_HEADS * dk, GLA_HEADS * dv
    scale = dk ** -0.5
    for hh in range(GLA_HEADS):
        q = h_ref[0, :, hh * dk:(hh + 1) * dk] * scale
        k = h_ref[0, :, hk + hh * dk:hk + (hh + 1) * dk]
        v = h_ref[0, :, 2 * hk + hh * dv:2 * hk + (hh + 1) * dv]
        r = h_ref[0, :, 2 * hk + hv + hh * dv:2 * hk + hv + (hh + 1) * dv]
        a = jnp.exp(lg_ref[0, :, hh * dk:(hh + 1) * dk])
        reps = dv // dk
        a_c = jnp.tile(_lane_col(a, dk), (1, reps))
        k_c = jnp.tile(_lane_col(k, dk), (1, reps))
        q_c = jnp.tile(_lane_col(q, dk), (1, reps))
        s_new = s_ref[0, hh] * a_c + k_c * v
        sn_ref[0, hh] = s_new
        o = jnp.sum(q_c * s_new, axis=0, keepdims=True)
        gated = _rms_norm(o, gn_ref[...]) * (r / (1.0 + jnp.exp(-r)))
        o_ref[0, :, hh * dv:(hh + 1) * dv] = gated.astype(o_ref.dtype)


def _gla_step(h, lg, g_norm, state):
    bs, n = h.shape
    _, _, dk, dv = state.shape
    hv = GLA_HEADS * dv
    o, s_new = pl.pallas_call(
        _gla_step_kernel,
        grid=(bs,),
        in_specs=[
            pl.BlockSpec((1, 1, n), lambda b: (b, 0, 0)),
            pl.BlockSpec((1, 1, lg.shape[1]), lambda b: (b, 0, 0)),
            pl.BlockSpec((1, dv), lambda b: (0, 0)),
            pl.BlockSpec((1, GLA_HEADS, dk, dv), lambda b: (b, 0, 0, 0)),
        ],
        out_specs=[
            pl.BlockSpec((1, 1, hv), lambda b: (b, 0, 0)),
            pl.BlockSpec((1, GLA_HEADS, dk, dv), lambda b: (b, 0, 0, 0)),
        ],
        out_shape=[jax.ShapeDtypeStruct((bs, 1, hv), BF16), jax.ShapeDtypeStruct(state.shape, F32)],
        compiler_params=_params("arbitrary"),
        name="gla_step",
    )(h.reshape(bs, 1, n), lg.reshape(bs, 1, -1), g_norm, state)
    return o.reshape(bs, hv), s_new


def _lambda_value(lp):
    e1 = jnp.exp(jnp.sum(lp[0:1] * lp[1:2], axis=1, keepdims=True))
    e2 = jnp.exp(jnp.sum(lp[2:3] * lp[3:4], axis=1, keepdims=True))
    return e1 - e2 + LAM_INIT


class _Decode:
    def __init__(self, q_ref, m_scr, l_scr, acc_scr):
        self.m_scr, self.l_scr, self.acc_scr = m_scr, l_scr, acc_scr
        _, self.n_head, self.dv = q_ref.shape
        qt = q_ref[0]
        first_half = lax.broadcasted_iota(jnp.int32, qt.shape, 1) < self.dv // 2
        self.q_rows = jnp.concatenate([jnp.where(first_half, qt, 0.0), jnp.where(first_half, 0.0, qt)],
                                      axis=0).astype(BF16)

    def init(self):
        self.m_scr[...] = jnp.full(self.m_scr.shape, MASK_VALUE, F32)
        self.l_scr[...] = jnp.zeros_like(self.l_scr)
        self.acc_scr[...] = jnp.zeros_like(self.acc_scr)

    def own_head(self, n, n_valid):
        shape = (2 * self.n_head, n)
        col = lax.broadcasted_iota(jnp.int32, shape, 1)
        row = lax.broadcasted_iota(jnp.int32, shape, 0)
        return jnp.logical_and(col % self.n_head == row % self.n_head, col < n_valid)

    def scores(self, ks, own):
        return [jnp.where(own, lax.dot_general(self.q_rows, k2.astype(BF16), _NT,
                                               preferred_element_type=F32) * LOG2_E, MASK_VALUE) for k2 in ks]

    def accumulate(self, ss, vs):
        m_old = self.m_scr[...]
        m_new = jnp.maximum(m_old, jnp.max(functools.reduce(jnp.maximum, ss), axis=1, keepdims=True))
        alpha = jnp.exp2(m_old - m_new)
        ps = [jnp.exp2(s - m_new) for s in ss]
        self.m_scr[...] = m_new
        self.l_scr[...] = alpha * self.l_scr[...] + jnp.sum(functools.reduce(jnp.add, ps), axis=1, keepdims=True)
        pv = [jnp.dot(p.astype(BF16), v2.astype(BF16), preferred_element_type=F32) for p, v2 in zip(ps, vs)]
        self.acc_scr[...] = alpha * self.acc_scr[...] + functools.reduce(jnp.add, pv)

    def page_scores(self, k_refs):
        n = k_refs[0].shape[1] * self.n_head
        return self.scores([r[0].reshape(n, self.dv) for r in k_refs], self.own_head(n, n))

    def accumulate_pages(self, ss, v_refs):
        n = v_refs[0].shape[1] * self.n_head
        self.accumulate(ss, [r[0].reshape(n, self.dv) for r in v_refs])

    def finish(self, kc_ref, vc_ref, lam_ref, g_ref, o_ref):
        fill = jnp.zeros((LANES - self.n_head, self.dv), F32)
        ss = self.scores([jnp.concatenate([kc_ref[0], fill], axis=0)], self.own_head(LANES, self.n_head))
        self.accumulate(ss, [jnp.concatenate([vc_ref[0], fill], axis=0)])
        ratio = self.acc_scr[...] / self.l_scr[...]
        o = ratio[:self.n_head] - _lambda_value(lam_ref[...]) * ratio[self.n_head:]
        o_ref[0] = _rms_norm(o, g_ref[...]) * (1.0 - LAM_INIT)


def _finish_kernel(*refs, n_pages_step, tiles_per_seq):
    hosting = n_pages_step > 0
    if hosting:
        refs = refs[1:]
    a_ref, wo_ref, x_ref, g1_ref, b1_ref, w1_ref, w2_ref, g2_ref, b2_ref = refs[:9]
    refs = refs[9:]
    if hosting:
        lam_ref, q_ref, kc_ref, vc_ref, gs_ref = refs[:5]
        k_refs = refs[5:5 + n_pages_step]
        v_refs = refs[5 + n_pages_step:5 + 2 * n_pages_step]
        o_ref, od_ref, xb_scr, acc_scr, m_scr, l_scr, dacc_scr = refs[5 + 2 * n_pages_step:]
        dec = _Decode(q_ref, m_scr, l_scr, dacc_scr)
    else:
        o_ref, xb_scr, acc_scr = refs
    f = pl.program_id(1)
    last_f = pl.num_programs(1) - 1
    tile_in_seq = pl.program_id(0) % tiles_per_seq

    @pl.when(f == 0)
    def _():
        proj = jnp.dot(a_ref[...], wo_ref[...], preferred_element_type=F32)
        x1 = _layer_norm(DEEPNORM_ALPHA * x_ref[...] + proj, g1_ref[...], b1_ref[...])
        o_ref[...] = x1
        xb_scr[...] = x1.astype(BF16)
        acc_scr[...] = jnp.zeros_like(acc_scr)

    if hosting:
        pl.when(jnp.logical_and(f == 0, tile_in_seq == 0))(dec.init)

    if hosting:
        page_scores = dec.page_scores(k_refs)
    hid = jnp.maximum(jnp.dot(xb_scr[...], w1_ref[...], preferred_element_type=F32), 0.0)
    if hosting:
        dec.accumulate_pages(page_scores, v_refs)
    acc_scr[...] += jnp.dot((hid * hid).astype(BF16), w2_ref[...], preferred_element_type=F32)

    @pl.when(f == last_f)
    def _():
        o_ref[...] = _layer_norm(DEEPNORM_ALPHA * o_ref[...] + acc_scr[...], g2_ref[...], b2_ref[...])

    if hosting:
        @pl.when(jnp.logical_and(f == last_f, tile_in_seq == tiles_per_seq - 1))
        def _():
            dec.finish(kc_ref, vc_ref, lam_ref, gs_ref, od_ref)


def _finish_layer(a, w_out, x, ln1, w1, w2, ln2, decode=None):
    m, kk = a.shape
    d = w_out.shape[1]
    ff = w1.shape[1]
    tf = min(MLP_FF_TILE, ff)
    n_f = ff // tf
    tps = HOST_TILES_PER_SEQ
    if decode is None:
        tm, pps = _row_tile(m), 0
        n_tiles = m // tm
    else:
        q, k_cur, v_cur, cache_k, cache_v, page_table, lam_params, g_sub = decode
        n_seqs, n_head, dv = q.shape
        n_tiles = n_seqs * tps
        tm = pl.cdiv(pl.cdiv(m, n_tiles), BF16_ROWS) * BF16_ROWS
        n_pages, page = page_table.shape[1], cache_k.shape[1]
        assert n_pages % (tps * n_f) == 0 and pl.cdiv(m, tm) == n_tiles
        pps = n_pages // (tps * n_f)
    row_spec = pl.BlockSpec((tm, d), lambda i, f, *_: (i, 0))
    vec_spec = pl.BlockSpec((1, d), lambda i, f, *_: (0, 0))
    in_specs = [
        pl.BlockSpec((tm, kk), lambda i, f, *_: (i, 0)),
        pl.BlockSpec((kk, d), lambda i, f, *_: (0, 0)),
        row_spec, vec_spec, vec_spec,
        pl.BlockSpec((d, tf), lambda i, f, *_: (0, f)),
        pl.BlockSpec((tf, d), lambda i, f, *_: (f, 0)),
        vec_spec, vec_spec,
    ]
    args = [a, w_out, x, *ln1, w1, w2, *ln2]
    out_specs = [row_spec]
    out_shape = [jax.ShapeDtypeStruct((m, d), F32)]
    scratch = [pltpu.VMEM((tm, d), BF16), pltpu.VMEM((tm, d), F32)]
    prefetch = []
    if decode is not None:
        head_spec = pl.BlockSpec((1, n_head, dv), lambda i, f, pt: (i // tps, 0, 0))

        def page_spec(s):
            return pl.BlockSpec((1, page, n_head, dv),
                                lambda i, f, pt: (pt[i // tps, ((i % tps) * n_f + f) * pps + s], 0, 0, 0))

        in_specs += [pl.BlockSpec(lam_params.shape, lambda i, f, pt: (0, 0)), head_spec, head_spec, head_spec,
                     pl.BlockSpec((1, dv), lambda i, f, pt: (0, 0))]
        in_specs += [page_spec(s) for s in range(pps)] * 2
        args += [lam_params, q, k_cur, v_cur, g_sub] + [cache_k] * pps + [cache_v] * pps
        out_specs.append(head_spec)
        out_shape.append(jax.ShapeDtypeStruct((n_seqs, n_head, dv), F32))
        scratch += [pltpu.VMEM((2 * n_head, 1), F32), pltpu.VMEM((2 * n_head, 1), F32),
                    pltpu.VMEM((2 * n_head, dv), F32)]
        prefetch = [page_table]
    outs = pl.pallas_call(
        functools.partial(_finish_kernel, n_pages_step=pps, tiles_per_seq=tps),
        grid_spec=pltpu.PrefetchScalarGridSpec(
            num_scalar_prefetch=len(prefetch), grid=(n_tiles, n_f),
            in_specs=in_specs, out_specs=out_specs, scratch_shapes=scratch),
        out_shape=out_shape,
        compiler_params=_params("arbitrary", "arbitrary",
                                limit=VMEM_LIMIT if decode is None else VMEM_LIMIT_HOST),
        name="finish_layer" if decode is None else "finish_layer_host",
    )(*prefetch, *args)
    return outs[0] if decode is None else tuple(outs)


def _diff_proj_kernel(x_ref, w_ref, cos_ref, sin_up_ref, sin_dn_ref,
                      q_ref, k_ref, kb_ref, v_ref, vb_ref, *, tn, q_scale, rot_half):
    xb = x_ref[...].astype(BF16)
    nq = q_ref.shape[1]

    def project(seg, j):
        lo = seg * nq + j * tn
        return jnp.dot(xb, w_ref[:, lo:lo + tn], preferred_element_type=F32)

    def rope(t):
        groups = []
        for g in range(tn // LANES):
            tg = t[:, g * LANES:(g + 1) * LANES]
            groups.append(tg * cos_ref[...] + pltpu.roll(tg, LANES - rot_half, 1) * sin_up_ref[...]
                          + pltpu.roll(tg, rot_half, 1) * sin_dn_ref[...])
        return jnp.concatenate(groups, axis=1)

    for j in range(nq // tn):
        cols = slice(j * tn, (j + 1) * tn)
        q_ref[:, cols] = (rope(project(0, j)) * q_scale).astype(q_ref.dtype)
        kr = rope(project(1, j))
        k_ref[:, cols] = kr
        kb_ref[:, cols] = kr.astype(kb_ref.dtype)
        y = project(2, j)
        v_ref[:, cols] = y
        vb_ref[:, cols] = y.astype(vb_ref.dtype)


def _rope_tables(pos, dh):
    rot = dh // 4
    inv = ROPE_THETA ** (-jnp.arange(0, rot, 2, dtype=F32) / rot)
    ang = pos.astype(F32)[:, None] * inv[None, :]
    n = pos.shape[0]

    def lanes(first, second, fill):
        sub = jnp.concatenate([first, second, jnp.full((n, dh - rot), fill, F32)], axis=1)
        return jnp.tile(sub, (1, LANES // dh))

    zeros = jnp.zeros_like(ang)
    return (lanes(jnp.cos(ang), jnp.cos(ang), 1.0),
            lanes(-jnp.sin(ang), zeros, 0.0),
            lanes(zeros, jnp.sin(ang), 0.0))


def _diff_project(x, w, pos):
    m, d = x.shape
    n = w.shape[1]
    nq = n // 3
    dh = nq // (2 * DIFF_HEADS)
    assert m % pos.shape[0] == 0
    tm = _row_tile(pos.shape[0])
    pos_tiles = pos.shape[0] // tm
    cos_t, sin_up, sin_dn = _rope_tables(pos, dh)
    kern = functools.partial(_diff_proj_kernel, tn=_col_tile(nq), q_scale=dh ** -0.5, rot_half=dh // 8)
    tab_spec = pl.BlockSpec((tm, LANES), lambda i: (i % pos_tiles, 0))
    out_spec = pl.BlockSpec((tm, nq), lambda i: (i, 0))
    return pl.pallas_call(
        kern,
        grid=(m // tm,),
        in_specs=[
            pl.BlockSpec((tm, d), lambda i: (i, 0)),
            pl.BlockSpec((d, n), lambda i: (0, 0)),
            tab_spec, tab_spec, tab_spec,
        ],
        out_specs=[out_spec] * 5,
        out_shape=[jax.ShapeDtypeStruct((m, nq), BF16),
                   jax.ShapeDtypeStruct((m, nq), F32), jax.ShapeDtypeStruct((m, nq), BF16),
                   jax.ShapeDtypeStruct((m, nq), F32), jax.ShapeDtypeStruct((m, nq), BF16)],
        compiler_params=_params("arbitrary"),
        name="diff_proj",
    )(x, w, cos_t, sin_up, sin_dn)


def _diff_attn_kernel(lam_ref, q_ref, k_ref, v_ref, g_ref, o_ref, kp_scr, vp_scr, s_scr, p_scr,
                      *, n_meta, tile, n_tiles):
    lam = _lambda_value(lam_ref[...])
    seq, dq = k_ref.shape
    dv = v_ref.shape[1]
    first = LANES
    pad = first - n_meta
    rows = 2 * tile
    kp_scr[0:pad, :] = jnp.zeros((pad, dq), BF16)
    kp_scr[pad:pad + seq, :] = k_ref[...]
    vp_scr[0:pad, :] = jnp.zeros((pad, 2 * dv), BF16)
    vp_scr[pad:pad + seq, 0:dv] = v_ref[...]
    vp_scr[pad:pad + seq, dv:2 * dv] = jnp.ones((seq, dv), BF16)
    first_half = lax.broadcasted_iota(jnp.int32, (1, dq), 1) < dq // 2

    def stack(q):
        zero = jnp.zeros_like(q)
        return jnp.concatenate([jnp.where(first_half, q, zero), jnp.where(first_half, zero, q)], axis=0)

    def block_coords(length, width):
        ri = lax.broadcasted_iota(jnp.int32, (2 * length, width), 0)
        ri = jnp.where(ri >= length, ri - length, ri)
        ci = lax.broadcasted_iota(jnp.int32, (2 * length, width), 1)
        return ci, ri

    def finish(acc, length):
        ratio = acc[:, 0:dv] / acc[:, dv:2 * dv]
        o = ratio[:length] - lam * ratio[length:]
        return (_rms_norm(o, g_ref[...]) * (1.0 - LAM_INIT)).astype(o_ref.dtype)

    ci, ri = block_coords(n_meta, first)
    s = lax.dot_general(stack(q_ref[0:n_meta, :]), kp_scr[0:first, :], _NT, preferred_element_type=F32)
    s = jnp.where(jnp.logical_and(ci >= pad, ci - pad <= ri), s, MASK_VALUE)
    p = jnp.exp(s - jnp.max(s, axis=1, keepdims=True))
    o_ref[0:n_meta, :] = finish(jnp.dot(p.astype(BF16), vp_scr[0:first, :], preferred_element_type=F32), n_meta)

    ci, ri = block_coords(tile, tile)
    causal = ci <= ri
    not_pad = block_coords(tile, first)[0] >= pad
    row_chunk = 64

    def key_block(j):
        return (0, first) if j == 0 else (first + (j - 1) * tile, first + j * tile)

    def score_tile(it):
        qoff = n_meta + it * tile
        qs = stack(q_ref[qoff:qoff + tile, :])
        for j in range(it + 2):
            lo, hi = key_block(j)
            s = lax.dot_general(qs, kp_scr[lo:hi, :], _NT, preferred_element_type=F32)
            if j == 0:
                s = jnp.where(not_pad, s, MASK_VALUE)
            if j == it + 1:
                s = jnp.where(causal, s, MASK_VALUE)
            s_scr[it % 2, :, lo:hi] = s

    def softmax_tile(it):
        buf = it % 2
        qoff = n_meta + it * tile
        width = key_block(it + 1)[1]
        n_cols = width // LANES
        for rc in range(rows // row_chunk):
            rsl = slice(rc * row_chunk, (rc + 1) * row_chunk)
            mx = s_scr[buf, rsl, 0:LANES]
            for c in range(1, n_cols):
                mx = jnp.maximum(mx, s_scr[buf, rsl, c * LANES:(c + 1) * LANES])
            m_b = jnp.broadcast_to(jnp.max(mx, axis=1, keepdims=True), (row_chunk, LANES))
            for c in range(n_cols):
                csl = slice(c * LANES, (c + 1) * LANES)
                p_scr[buf, rsl, csl] = jnp.exp(s_scr[buf, rsl, csl] - m_b).astype(BF16)
        acc = jnp.dot(p_scr[buf, :, 0:width], vp_scr[0:width, :], preferred_element_type=F32)
        o_ref[qoff:qoff + tile, :] = finish(acc, tile)

    score_tile(0)
    for it in range(n_tiles):
        if it + 1 < n_tiles:
            score_tile(it + 1)
        softmax_tile(it)


def _diff_attention(q, k, v, lam_params, g_sub, batch, seq, n_meta):
    m, n = q.shape
    dv = g_sub.shape[1]
    tile = ATTN_TILE
    n_tiles = (seq - n_meta) // tile
    assert n_meta + n_tiles * tile == seq and n == DIFF_HEADS * dv
    assert n_meta <= LANES and (LANES - n_meta) % BF16_ROWS == 0
    padded = LANES + n_tiles * tile
    kern = functools.partial(_diff_attn_kernel, n_meta=n_meta, tile=tile, n_tiles=n_tiles)
    head_spec = pl.BlockSpec((seq, dv), lambda b, hh: (b, hh))
    return pl.pallas_call(
        kern,
        grid=(batch, DIFF_HEADS),
        in_specs=[
            pl.BlockSpec(lam_params.shape, lambda b, hh: (0, 0)),
            head_spec, head_spec, head_spec,
            pl.BlockSpec((1, dv), lambda b, hh: (0, 0)),
        ],
        out_specs=head_spec,
        out_shape=jax.ShapeDtypeStruct((m, n), BF16),
        scratch_shapes=[pltpu.VMEM((padded, dv), BF16), pltpu.VMEM((padded, 2 * dv), BF16),
                        pltpu.VMEM((2, 2 * tile, padded), F32), pltpu.VMEM((2, 2 * tile, padded), BF16)],
        compiler_params=_params("arbitrary", "arbitrary"),
        name="diff_attn",
    )(lam_params, q, k, v, g_sub)


def kernel(x_prompt, x_sample, state_gla, cache_k, cache_v, page_table, meta_tokens,
           gla_w_in, gla_w_gate, gla_b_gate, gla_norm, gla_w_out,
           diff_w_in, diff_lambda, diff_norm, diff_w_out,
           mlp_w1, mlp_w2, ln_mix_g, ln_mix_b, ln_mlp_g, ln_mlp_b):
    batch, seq_new, d = x_prompt.shape
    n_meta = meta_tokens.shape[0]
    seq = n_meta + seq_new
    bs = x_sample.shape[0]
    assert x_sample.shape[1] == 1 and bs % 2 == 0

    meta = jnp.broadcast_to(meta_tokens[None].astype(x_prompt.dtype), (batch, n_meta, d))
    x_p = jnp.concatenate([meta, x_prompt], axis=1).reshape(batch * seq, d)
    x_s = x_sample.reshape(bs, d)

    def row(vec):
        return vec.reshape(1, -1).astype(F32)

    def finish_layer(layer, o, x, w_out, decode=None):
        return _finish_layer(o, w_out, x, (row(ln_mix_g[layer]), row(ln_mix_b[layer])),
                             mlp_w1[layer].astype(BF16), mlp_w2[layer].astype(BF16),
                             (row(ln_mlp_g[layer]), row(ln_mlp_b[layer])), decode)

    rank = gla_w_gate.shape[1]
    n_main = gla_w_in.shape[2] - rank
    w_main = gla_w_in[0, :, :n_main].astype(BF16)
    w_glow = jnp.pad(gla_w_in[0, :, n_main:], ((0, 0), (0, LANES - rank))).astype(BF16)
    w_gate = jnp.pad(gla_w_gate[0], ((0, LANES - rank), (0, 0))).astype(BF16)
    b_gate = row(gla_b_gate[0])
    g_norm = row(gla_norm[0])
    w_out0 = gla_w_out[0].astype(BF16)
    w_in1 = diff_w_in[0].astype(BF16)
    w_out1 = diff_w_out[0].astype(BF16)
    g_sub = row(diff_norm[0])
    lam_params = diff_lambda[0].astype(F32)
    dv = g_sub.shape[1]
    past = page_table.shape[1] * PAGE_SIZE

    h_s, lg_s = _gla_project(x_s, w_main, w_glow, w_gate, b_gate)
    o_s, st_s = _gla_step(h_s, lg_s, g_norm, state_gla[0].astype(F32))
    x_s = finish_layer(0, o_s, x_s, w_out0)
    q_s, k_s, _, v_s, _ = _diff_project(x_s, w_in1, jnp.full((bs,), past))
    heads = (bs, DIFF_HEADS, dv)
    q_s3, k_s3, v_s3 = q_s.astype(F32).reshape(heads), k_s.reshape(heads), v_s.reshape(heads)

    def hosted(lo):
        hi = lo + bs // 2
        return (q_s3[lo:hi], k_s3[lo:hi], v_s3[lo:hi], cache_k[0], cache_v[0], page_table[lo:hi],
                lam_params, g_sub)

    h_p, lg_p = _gla_project(x_p, w_main, w_glow, w_gate, b_gate)
    o_p, st_p = _gla_core(h_p, lg_p, g_norm, batch, seq, n_meta)
    x_p, a_s_lo = finish_layer(0, o_p, x_p, w_out0, hosted(0))

    q_p, k_p, kb_p, v_p, vb_p = _diff_project(x_p, w_in1, jnp.arange(seq))
    a_p = _diff_attention(q_p, kb_p, vb_p, lam_params, g_sub, batch, seq, n_meta)
    x_p, a_s_hi = finish_layer(1, a_p, x_p, w_out1, hosted(bs // 2))

    a_s = jnp.concatenate([a_s_lo, a_s_hi], axis=0).reshape(bs, DIFF_HEADS * dv).astype(BF16)
    x_s = finish_layer(1, a_s, x_s, w_out1)

    y_prompt = x_p.reshape(batch, seq, d)[:, n_meta:]
    y_sample = x_s.reshape(bs, 1, d)
    return (y_prompt, y_sample, st_p[None], st_s[None],
            k_p.reshape(1, batch, seq, DIFF_HEADS, dv), v_p.reshape(1, batch, seq, DIFF_HEADS, dv),
            k_s.reshape(1, bs, 1, DIFF_HEADS, dv), v_s.reshape(1, bs, 1, DIFF_HEADS, dv))
```

```python
import functools
import math

import jax
import jax.numpy as jnp
from jax import lax
from jax.experimental import pallas as pl
from jax.experimental.pallas import tpu as pltpu

F32 = jnp.float32
BF16 = jnp.bfloat16

DEPTH = 2
GLA_HEADS = 4
GLA_TAU = 16.0
DIFF_HEADS = 8
ROPE_THETA = 500000.0
PAGE_SIZE = 128
LN_EPS = 1e-5
DEEPNORM_ALPHA = (2 * DEPTH) ** 0.25
LAM_INIT = 0.8 - 0.6 * math.exp(-0.3 * 1)

LANES = 128
SUBLANES = 8
BF16_ROWS = 16
VMEM_LIMIT = 48 * 1024 * 1024

GLA_CHUNK = 128
GLA_HEADS_PER_STEP = 2
GLA_STEP_SEQS = 8
GLA_SAFE_LOG_SPAN = 80.0
ATTN_TILE = 256
MLP_FF_TILE = 512
HOST_TILES_PER_SEQ = 1
VMEM_LIMIT_HOST = 60000 * 1024
MASK_VALUE = -1e30
LOG2_E = 1.4426950408889634

_NT = (((1,), (1,)), ((), ()))
_TN = (((0,), (0,)), ((), ()))


def _row_tile(m):
    for cand in (688, 512, 256, 128, 64, 32, 16, 8):
        if m % cand == 0:
            return cand
    raise ValueError(f"row count {m} is not a multiple of 8")


def _col_tile(n):
    for cand in (512, 256, 128):
        if n % cand == 0:
            return cand
    raise ValueError(f"column count {n} is not a multiple of 128")


def _params(*sem, limit=VMEM_LIMIT):
    return pltpu.CompilerParams(dimension_semantics=sem, vmem_limit_bytes=limit)


def _layer_norm(y, g, b):
    mu = jnp.mean(y, axis=-1, keepdims=True)
    yc = y - mu
    var = jnp.mean(yc * yc, axis=-1, keepdims=True)
    return yc * lax.rsqrt(var + LN_EPS) * g + b


def _rms_norm(y, g):
    return y * lax.rsqrt(jnp.mean(y * y, axis=-1, keepdims=True) + LN_EPS) * g


def _lane_col(row, n):
    return jnp.transpose(jnp.broadcast_to(row, (n, n)))


def _gla_proj_kernel(x_ref, w_ref, wg_ref, wgate_ref, bgate_ref, h_ref, lg_ref, *, tn):
    xb = x_ref[...].astype(BF16)
    g_low = jnp.dot(xb, wg_ref[...], preferred_element_type=F32)
    z = jnp.dot(g_low.astype(BF16), wgate_ref[...], preferred_element_type=F32) + bgate_ref[...]
    log_sig = jnp.minimum(z, 0.0) - jnp.log(1.0 + jnp.exp(-jnp.abs(z)))
    lg_ref[...] = log_sig * (1.0 / GLA_TAU)
    for j in range(w_ref.shape[1] // tn):
        cols = slice(j * tn, (j + 1) * tn)
        h_ref[:, cols] = jnp.dot(xb, w_ref[:, cols], preferred_element_type=F32)


def _gla_project(x, w_main, w_glow, w_gate, b_gate):
    m, d = x.shape
    n = w_main.shape[1]
    ng = w_gate.shape[1]
    tm = _row_tile(m)
    return pl.pallas_call(
        functools.partial(_gla_proj_kernel, tn=_col_tile(n)),
        grid=(m // tm,),
        in_specs=[
            pl.BlockSpec((tm, d), lambda i: (i, 0)),
            pl.BlockSpec((d, n), lambda i: (0, 0)),
            pl.BlockSpec((d, LANES), lambda i: (0, 0)),
            pl.BlockSpec((LANES, ng), lambda i: (0, 0)),
            pl.BlockSpec((1, ng), lambda i: (0, 0)),
        ],
        out_specs=[
            pl.BlockSpec((tm, n), lambda i: (i, 0)),
            pl.BlockSpec((tm, ng), lambda i: (i, 0)),
        ],
        out_shape=[jax.ShapeDtypeStruct((m, n), F32), jax.ShapeDtypeStruct((m, ng), F32)],
        compiler_params=_params("arbitrary"),
        name="gla_proj",
    )(x, w_main, w_glow, w_gate, b_gate)


def _gla_core_kernel(q_ref, k_ref, v_ref, r_ref, lg_ref, gn_ref, o_ref, st_ref, s_scr,
                     *, n_meta, chunk, n_chunks, dk, dv):
    wk = q_ref.shape[1]
    heads = range(wk // dk)
    scale = dk ** -0.5
    s_scr[...] = jnp.zeros_like(s_scr)

    def do_chunk(off, length):
        rows = pl.ds(off, length)
        ii = lax.broadcasted_iota(jnp.int32, (length, length), 0)
        jj = lax.broadcasted_iota(jnp.int32, (length, length), 1)
        causal = ii >= jj
        lg = lg_ref[rows, :]
        hi = lg.astype(BF16)
        rest = lg - hi.astype(F32)
        mid = rest.astype(BF16)
        lo = (rest - mid.astype(F32)).astype(BF16)
        parts = jnp.dot(causal.astype(BF16), jnp.concatenate([hi, mid, lo], axis=1),
                        preferred_element_type=F32)
        b = parts[:, 0:wk] + parts[:, wk:2 * wk] + parts[:, 2 * wk:3 * wk]
        q = q_ref[rows, :] * scale
        k = k_ref[rows, :]
        b_mid = b[length // 2:length // 2 + 1, :]
        b_last = b[length - 1:length, :]
        q_in = (q * jnp.exp(b)).astype(BF16)
        qs = (q * jnp.exp(b - b_mid)).astype(BF16)
        ks = (k * jnp.exp(b_mid - b)).astype(BF16)
        kd = (k * jnp.exp(b_last - b)).astype(BF16)
        decay = jnp.exp(b_last)
        vb = v_ref[rows, :].astype(BF16)

        def kcols(hh):
            return slice(hh * dk, (hh + 1) * dk)

        def vcols(hh):
            return slice(hh * dv, (hh + 1) * dv)

        states = [s_scr[hh] for hh in heads]
        o_inter = [lax.dot_general(q_in[:, kcols(hh)], states[hh].astype(BF16), _NT,
                                   preferred_element_type=F32) for hh in heads]
        scores = [lax.dot_general(qs[:, kcols(hh)], ks[:, kcols(hh)], _NT, preferred_element_type=F32)
                  for hh in heads]
        scores = [jnp.where(causal, s, 0.0).astype(BF16) for s in scores]
        o_intra = [jnp.dot(scores[hh], vb[:, vcols(hh)], preferred_element_type=F32) for hh in heads]
        grown = [lax.dot_general(vb[:, vcols(hh)], kd[:, kcols(hh)], _TN, preferred_element_type=F32)
                 for hh in heads]
        for hh in heads:
            s_scr[hh] = states[hh] * decay[:, kcols(hh)] + grown[hh]
            r = r_ref[rows, vcols(hh)]
            gated = _rms_norm(o_inter[hh] + o_intra[hh], gn_ref[...]) * (r / (1.0 + jnp.exp(-r)))
            o_ref[rows, vcols(hh)] = gated.astype(o_ref.dtype)

    def token_by_token():
        group = BF16_ROWS
        seq = q_ref.shape[0]

        def as_rows(row):
            return jnp.concatenate([row, jnp.zeros((SUBLANES - 1, row.shape[1]), row.dtype)], axis=0)

        def body(g, carry):
            base = pl.multiple_of(g * group, group)
            outs = [[] for _ in heads]
            for t in range(group):
                row = pl.ds(base + t, 1)
                a = jnp.exp(lg_ref[row, :])
                q = q_ref[row, :] * scale
                k = k_ref[row, :]
                v = v_ref[row, :]
                r = r_ref[row, :]
                for hh in heads:
                    kc, vc = slice(hh * dk, (hh + 1) * dk), slice(hh * dv, (hh + 1) * dv)
                    grown = lax.dot_general(as_rows(v[:, vc]).astype(BF16), as_rows(k[:, kc]).astype(BF16),
                                            _TN, preferred_element_type=F32)
                    s_new = s_scr[hh] * a[:, kc] + grown
                    s_scr[hh] = s_new
                    o = lax.dot_general(as_rows(q[:, kc]).astype(BF16), s_new.astype(BF16), _NT,
                                        preferred_element_type=F32)[0:1]
                    rr = r[:, vc]
                    outs[hh].append(_rms_norm(o, gn_ref[...]) * (rr / (1.0 + jnp.exp(-rr))))
            for hh in heads:
                o_ref[pl.ds(base, group), hh * dv:(hh + 1) * dv] = jnp.concatenate(
                    outs[hh], axis=0).astype(o_ref.dtype)
            return carry

        lax.fori_loop(0, seq // group, body, 0)

    half_span = max(chunk, n_meta) // 2
    chunked_is_safe = jnp.min(lg_ref[...]) * half_span > -GLA_SAFE_LOG_SPAN

    @pl.when(chunked_is_safe)
    def _():
        do_chunk(0, n_meta)
        for c in range(n_chunks):
            do_chunk(n_meta + c * chunk, chunk)

    pl.when(jnp.logical_not(chunked_is_safe))(token_by_token)

    for hh in heads:
        st_ref[0, hh] = jnp.transpose(s_scr[hh])


def _gla_core(h, lg, g_norm, batch, seq, n_meta):
    m = h.shape[0]
    hk = lg.shape[1]
    dk = hk // GLA_HEADS
    dv = g_norm.shape[1]
    hv = GLA_HEADS * dv
    assert h.shape[1] == 2 * hk + 2 * hv and m == batch * seq
    chunk = GLA_CHUNK
    n_chunks = (seq - n_meta) // chunk
    assert n_meta + n_chunks * chunk == seq
    hps = GLA_HEADS_PER_STEP
    groups = GLA_HEADS // hps
    wk, wv = hps * dk, hps * dv
    kern = functools.partial(_gla_core_kernel, n_meta=n_meta, chunk=chunk, n_chunks=n_chunks, dk=dk, dv=dv)
    return pl.pallas_call(
        kern,
        grid=(batch, groups),
        in_specs=[
            pl.BlockSpec((seq, wk), lambda b, g: (b, g)),
            pl.BlockSpec((seq, wk), lambda b, g: (b, groups + g)),
            pl.BlockSpec((seq, wv), lambda b, g: (b, 2 * hk // wv + g)),
            pl.BlockSpec((seq, wv), lambda b, g: (b, (2 * hk + hv) // wv + g)),
            pl.BlockSpec((seq, wk), lambda b, g: (b, g)),
            pl.BlockSpec((1, dv), lambda b, g: (0, 0)),
        ],
        out_specs=[
            pl.BlockSpec((seq, wv), lambda b, g: (b, g)),
            pl.BlockSpec((1, hps, dk, dv), lambda b, g: (b, g, 0, 0)),
        ],
        out_shape=[jax.ShapeDtypeStruct((m, hv), BF16),
                   jax.ShapeDtypeStruct((batch, GLA_HEADS, dk, dv), F32)],
        scratch_shapes=[pltpu.VMEM((hps, dv, dk), F32)],
        compiler_params=_params("arbitrary", "arbitrary"),
        name="gla_core",
    )(h, h, h, h, lg, g_norm)


def _gla_step_kernel(h_ref, lg_ref, gn_ref, s_ref, o_ref, sn_ref):
    dk, dv = s_ref.shape[2], s_ref.shape[3]
    hk, hv = GLA_HEADS * dk, GLA_HEADS * dv
    scale = dk ** -0.5
    reps = dv // dk
    for sq in range(s_ref.shape[0]):
        for hh in range(GLA_HEADS):
            q = h_ref[sq, :, hh * dk:(hh + 1) * dk] * scale
            k = h_ref[sq, :, hk + hh * dk:hk + (hh + 1) * dk]
            v = h_ref[sq, :, 2 * hk + hh * dv:2 * hk + (hh + 1) * dv]
            r = h_ref[sq, :, 2 * hk + hv + hh * dv:2 * hk + hv + (hh + 1) * dv]
            a = jnp.exp(lg_ref[sq, :, hh * dk:(hh + 1) * dk])
            a_c = jnp.tile(_lane_col(a, dk), (1, reps))
            k_c = jnp.tile(_lane_col(k, dk), (1, reps))
            q_c = jnp.tile(_lane_col(q, dk), (1, reps))
            s_new = s_ref[sq, hh] * a_c + k_c * v
            sn_ref[sq, hh] = s_new
            o = jnp.sum(q_c * s_new, axis=0, keepdims=True)
            gated = _rms_norm(o, gn_ref[...]) * (r / (1.0 + jnp.exp(-r)))
            o_ref[sq, :, hh * dv:(hh + 1) * dv] = gated.astype(o_ref.dtype)


def _gla_step(h, lg, g_norm, state):
    bs, n = h.shape
    _, _, dk, dv = state.shape
    hv = GLA_HEADS * dv
    nb = math.gcd(bs, GLA_STEP_SEQS)
    o, s_new = pl.pallas_call(
        _gla_step_kernel,
        grid=(bs // nb,),
        in_specs=[
            pl.BlockSpec((nb, 1, n), lambda b: (b, 0, 0)),
            pl.BlockSpec((nb, 1, lg.shape[1]), lambda b: (b, 0, 0)),
            pl.BlockSpec((1, dv), lambda b: (0, 0)),
            pl.BlockSpec((nb, GLA_HEADS, dk, dv), lambda b: (b, 0, 0, 0)),
        ],
        out_specs=[
            pl.BlockSpec((nb, 1, hv), lambda b: (b, 0, 0)),
            pl.BlockSpec((nb, GLA_HEADS, dk, dv), lambda b: (b, 0, 0, 0)),
        ],
        out_shape=[jax.ShapeDtypeStruct((bs, 1, hv), BF16), jax.ShapeDtypeStruct(state.shape, F32)],
        compiler_params=_params("arbitrary"),
        name="gla_step",
    )(h.reshape(bs, 1, n), lg.reshape(bs, 1, -1), g_norm, state)
    return o.reshape(bs, hv), s_new


def _lambda_value(lp):
    e1 = jnp.exp(jnp.sum(lp[0:1] * lp[1:2], axis=1, keepdims=True))
    e2 = jnp.exp(jnp.sum(lp[2:3] * lp[3:4], axis=1, keepdims=True))
    return e1 - e2 + LAM_INIT


class _Decode:
    def __init__(self, q_ref, m_scr, l_scr, acc_scr):
        self.m_scr, self.l_scr, self.acc_scr = m_scr, l_scr, acc_scr
        _, self.n_head, self.dv = q_ref.shape
        qt = q_ref[0]
        first_half = lax.broadcasted_iota(jnp.int32, qt.shape, 1) < self.dv // 2
        self.q_rows = jnp.concatenate([jnp.where(first_half, qt, 0.0), jnp.where(first_half, 0.0, qt)],
                                      axis=0).astype(BF16)

    def init(self):
        self.m_scr[...] = jnp.full(self.m_scr.shape, MASK_VALUE, F32)
        self.l_scr[...] = jnp.zeros_like(self.l_scr)
        self.acc_scr[...] = jnp.zeros_like(self.acc_scr)

    def own_head(self, n, n_valid):
        shape = (2 * self.n_head, n)
        col = lax.broadcasted_iota(jnp.int32, shape, 1)
        row = lax.broadcasted_iota(jnp.int32, shape, 0)
        return jnp.logical_and(col % self.n_head == row % self.n_head, col < n_valid)

    def scores(self, ks, own):
        return [jnp.where(own, lax.dot_general(self.q_rows, k2.astype(BF16), _NT, preferred_element_type=F32),
                          MASK_VALUE) for k2 in ks]

    def accumulate(self, ss, vs):
        m_old = self.m_scr[...]
        m_new = jnp.maximum(m_old, jnp.max(functools.reduce(jnp.maximum, ss), axis=1, keepdims=True))
        alpha = jnp.exp2(m_old - m_new)
        ps = [jnp.exp2(s - m_new) for s in ss]
        self.m_scr[...] = m_new
        self.l_scr[...] = alpha * self.l_scr[...] + jnp.sum(functools.reduce(jnp.add, ps), axis=1, keepdims=True)
        pv = [jnp.dot(p.astype(BF16), v2.astype(BF16), preferred_element_type=F32) for p, v2 in zip(ps, vs)]
        self.acc_scr[...] = alpha * self.acc_scr[...] + functools.reduce(jnp.add, pv)

    def page_scores(self, k_refs):
        n = k_refs[0].shape[1] * self.n_head
        return self.scores([r[0].reshape(n, self.dv) for r in k_refs], self.own_head(n, n))

    def accumulate_pages(self, ss, v_refs):
        n = v_refs[0].shape[1] * self.n_head
        self.accumulate(ss, [r[0].reshape(n, self.dv) for r in v_refs])

    def finish(self, kc_ref, vc_ref, lam_ref, g_ref, o_ref):
        fill = jnp.zeros((LANES - self.n_head, self.dv), F32)
        ss = self.scores([jnp.concatenate([kc_ref[0], fill], axis=0)], self.own_head(LANES, self.n_head))
        self.accumulate(ss, [jnp.concatenate([vc_ref[0], fill], axis=0)])
        ratio = self.acc_scr[...] / self.l_scr[...]
        o = ratio[:self.n_head] - _lambda_value(lam_ref[...]) * ratio[self.n_head:]
        o_ref[0] = _rms_norm(o, g_ref[...]) * (1.0 - LAM_INIT)


def _finish_kernel(*refs, n_pages_step, tiles_per_seq):
    hosting = n_pages_step > 0
    if hosting:
        refs = refs[1:]
    a_ref, wo_ref, x_ref, g1_ref, b1_ref, w1_ref, w2_ref, g2_ref, b2_ref = refs[:9]
    refs = refs[9:]
    if hosting:
        lam_ref, q_ref, kc_ref, vc_ref, gs_ref = refs[:5]
        k_refs = refs[5:5 + n_pages_step]
        v_refs = refs[5 + n_pages_step:5 + 2 * n_pages_step]
        o_ref, od_ref, xb_scr, acc_scr, m_scr, l_scr, dacc_scr = refs[5 + 2 * n_pages_step:]
        dec = _Decode(q_ref, m_scr, l_scr, dacc_scr)
    else:
        o_ref, xb_scr, acc_scr = refs
    f = pl.program_id(1)
    last_f = pl.num_programs(1) - 1
    tile_in_seq = pl.program_id(0) % tiles_per_seq

    @pl.when(f == 0)
    def _():
        proj = jnp.dot(a_ref[...], wo_ref[...], preferred_element_type=F32)
        x1 = _layer_norm(DEEPNORM_ALPHA * x_ref[...] + proj, g1_ref[...], b1_ref[...])
        o_ref[...] = x1
        xb_scr[...] = x1.astype(BF16)
        acc_scr[...] = jnp.zeros_like(acc_scr)

    if hosting:
        pl.when(jnp.logical_and(f == 0, tile_in_seq == 0))(dec.init)

    if hosting:
        page_scores = dec.page_scores(k_refs)
    hid = jnp.maximum(jnp.dot(xb_scr[...], w1_ref[...], preferred_element_type=F32), 0.0)
    if hosting:
        dec.accumulate_pages(page_scores, v_refs)
    acc_scr[...] += jnp.dot((hid * hid).astype(BF16), w2_ref[...], preferred_element_type=F32)

    @pl.when(f == last_f)
    def _():
        o_ref[...] = _layer_norm(DEEPNORM_ALPHA * o_ref[...] + acc_scr[...], g2_ref[...], b2_ref[...])

    if hosting:
        @pl.when(jnp.logical_and(f == last_f, tile_in_seq == tiles_per_seq - 1))
        def _():
            dec.finish(kc_ref, vc_ref, lam_ref, gs_ref, od_ref)


def _finish_layer(a, w_out, x, ln1, w1, w2, ln2, decode=None):
    m, kk = a.shape
    d = w_out.shape[1]
    ff = w1.shape[1]
    tf = min(MLP_FF_TILE, ff)
    n_f = ff // tf
    tps = HOST_TILES_PER_SEQ
    if decode is None:
        tm, pps = _row_tile(m), 0
        n_tiles = m // tm
    else:
        q, k_cur, v_cur, cache_k, cache_v, page_table, lam_params, g_sub = decode
        n_seqs, n_head, dv = q.shape
        n_tiles = n_seqs * tps
        tm = pl.cdiv(pl.cdiv(m, n_tiles), BF16_ROWS) * BF16_ROWS
        n_pages, page = page_table.shape[1], cache_k.shape[1]
        assert n_pages % (tps * n_f) == 0 and pl.cdiv(m, tm) == n_tiles
        pps = n_pages // (tps * n_f)
    row_spec = pl.BlockSpec((tm, d), lambda i, f, *_: (i, 0))
    vec_spec = pl.BlockSpec((1, d), lambda i, f, *_: (0, 0))
    in_specs = [
        pl.BlockSpec((tm, kk), lambda i, f, *_: (i, 0)),
        pl.BlockSpec((kk, d), lambda i, f, *_: (0, 0)),
        row_spec, vec_spec, vec_spec,
        pl.BlockSpec((d, tf), lambda i, f, *_: (0, f)),
        pl.BlockSpec((tf, d), lambda i, f, *_: (f, 0)),
        vec_spec, vec_spec,
    ]
    args = [a, w_out, x, *ln1, w1, w2, *ln2]
    out_specs = [row_spec]
    out_shape = [jax.ShapeDtypeStruct((m, d), F32)]
    scratch = [pltpu.VMEM((tm, d), BF16), pltpu.VMEM((tm, d), F32)]
    prefetch = []
    if decode is not None:
        head_spec = pl.BlockSpec((1, n_head, dv), lambda i, f, pt: (i // tps, 0, 0))

        def page_spec(s):
            return pl.BlockSpec((1, page, n_head, dv),
                                lambda i, f, pt: (pt[i // tps, ((i % tps) * n_f + f) * pps + s], 0, 0, 0))

        in_specs += [pl.BlockSpec(lam_params.shape, lambda i, f, pt: (0, 0)), head_spec, head_spec, head_spec,
                     pl.BlockSpec((1, dv), lambda i, f, pt: (0, 0))]
        in_specs += [page_spec(s) for s in range(pps)] * 2
        args += [lam_params, q, k_cur, v_cur, g_sub] + [cache_k] * pps + [cache_v] * pps
        out_specs.append(head_spec)
        out_shape.append(jax.ShapeDtypeStruct((n_seqs, n_head, dv), F32))
        scratch += [pltpu.VMEM((2 * n_head, 1), F32), pltpu.VMEM((2 * n_head, 1), F32),
                    pltpu.VMEM((2 * n_head, dv), F32)]
        prefetch = [page_table]
    outs = pl.pallas_call(
        functools.partial(_finish_kernel, n_pages_step=pps, tiles_per_seq=tps),
        grid_spec=pltpu.PrefetchScalarGridSpec(
            num_scalar_prefetch=len(prefetch), grid=(n_tiles, n_f),
            in_specs=in_specs, out_specs=out_specs, scratch_shapes=scratch),
        out_shape=out_shape,
        compiler_params=_params("arbitrary", "arbitrary",
                                limit=VMEM_LIMIT if decode is None else VMEM_LIMIT_HOST),
        name="finish_layer" if decode is None else "finish_layer_host",
    )(*prefetch, *args)
    return outs[0] if decode is None else tuple(outs)


def _diff_proj_kernel(x_ref, w_ref, cos_ref, sin_up_ref, sin_dn_ref,
                      q_ref, k_ref, kb_ref, v_ref, vb_ref, *, tn, q_scale, rot_half):
    xb = x_ref[...].astype(BF16)
    nq = q_ref.shape[1]

    def project(seg, j):
        lo = seg * nq + j * tn
        return jnp.dot(xb, w_ref[:, lo:lo + tn], preferred_element_type=F32)

    def rope(t):
        groups = []
        for g in range(tn // LANES):
            tg = t[:, g * LANES:(g + 1) * LANES]
            groups.append(tg * cos_ref[...] + pltpu.roll(tg, LANES - rot_half, 1) * sin_up_ref[...]
                          + pltpu.roll(tg, rot_half, 1) * sin_dn_ref[...])
        return jnp.concatenate(groups, axis=1)

    for j in range(nq // tn):
        cols = slice(j * tn, (j + 1) * tn)
        q_ref[:, cols] = (rope(project(0, j)) * q_scale).astype(q_ref.dtype)
        kr = rope(project(1, j))
        k_ref[:, cols] = kr
        kb_ref[:, cols] = kr.astype(kb_ref.dtype)
        y = project(2, j)
        v_ref[:, cols] = y
        vb_ref[:, cols] = y.astype(vb_ref.dtype)


def _rope_tables(pos, dh):
    rot = dh // 4
    inv = ROPE_THETA ** (-jnp.arange(0, rot, 2, dtype=F32) / rot)
    ang = pos.astype(F32)[:, None] * inv[None, :]
    n = pos.shape[0]

    def lanes(first, second, fill):
        sub = jnp.concatenate([first, second, jnp.full((n, dh - rot), fill, F32)], axis=1)
        return jnp.tile(sub, (1, LANES // dh))

    zeros = jnp.zeros_like(ang)
    return (lanes(jnp.cos(ang), jnp.cos(ang), 1.0),
            lanes(-jnp.sin(ang), zeros, 0.0),
            lanes(zeros, jnp.sin(ang), 0.0))


def _diff_project(x, w, pos):
    m, d = x.shape
    n = w.shape[1]
    nq = n // 3
    dh = nq // (2 * DIFF_HEADS)
    assert m % pos.shape[0] == 0
    tm = _row_tile(pos.shape[0])
    pos_tiles = pos.shape[0] // tm
    cos_t, sin_up, sin_dn = _rope_tables(pos, dh)
    kern = functools.partial(_diff_proj_kernel, tn=_col_tile(nq), q_scale=dh ** -0.5 * LOG2_E, rot_half=dh // 8)
    tab_spec = pl.BlockSpec((tm, LANES), lambda i: (i % pos_tiles, 0))
    out_spec = pl.BlockSpec((tm, nq), lambda i: (i, 0))
    return pl.pallas_call(
        kern,
        grid=(m // tm,),
        in_specs=[
            pl.BlockSpec((tm, d), lambda i: (i, 0)),
            pl.BlockSpec((d, n), lambda i: (0, 0)),
            tab_spec, tab_spec, tab_spec,
        ],
        out_specs=[out_spec] * 5,
        out_shape=[jax.ShapeDtypeStruct((m, nq), BF16),
                   jax.ShapeDtypeStruct((m, nq), F32), jax.ShapeDtypeStruct((m, nq), BF16),
                   jax.ShapeDtypeStruct((m, nq), F32), jax.ShapeDtypeStruct((m, nq), BF16)],
        compiler_params=_params("arbitrary"),
        name="diff_proj",
    )(x, w, cos_t, sin_up, sin_dn)


def _diff_attn_kernel(lam_ref, q_ref, k_ref, v_ref, g_ref, o_ref, kp_scr, vp_scr, s_scr, a_scr,
                      *, n_meta, tile, n_tiles):
    lam = _lambda_value(lam_ref[...])
    seq, dq = k_ref.shape
    first = LANES
    pad = first - n_meta
    kp_scr[0:pad, :] = jnp.zeros((pad, dq), BF16)
    kp_scr[pad:pad + seq, :] = k_ref[...]
    vp_scr[0:pad, :] = jnp.zeros((pad, v_ref.shape[1]), BF16)
    vp_scr[pad:pad + seq, :] = v_ref[...]
    first_half = lax.broadcasted_iota(jnp.int32, (1, dq), 1) < dq // 2

    def stack(q):
        zero = jnp.zeros_like(q)
        return jnp.concatenate([jnp.where(first_half, q, zero), jnp.where(first_half, zero, q)], axis=0)

    def block_coords(length, width):
        ri = lax.broadcasted_iota(jnp.int32, (2 * length, width), 0)
        ri = jnp.where(ri >= length, ri - length, ri)
        ci = lax.broadcasted_iota(jnp.int32, (2 * length, width), 1)
        return ci, ri

    def finish(weights, width):
        o = jnp.dot(weights, vp_scr[0:width, :], preferred_element_type=F32)
        return (_rms_norm(o, g_ref[...]) * (1.0 - LAM_INIT)).astype(o_ref.dtype)

    ci, ri = block_coords(n_meta, first)
    s = lax.dot_general(stack(q_ref[0:n_meta, :]), kp_scr[0:first, :], _NT, preferred_element_type=F32)
    s = jnp.where(jnp.logical_and(ci >= pad, ci - pad <= ri), s, MASK_VALUE)
    p = jnp.exp2(s - jnp.max(s, axis=1, keepdims=True))
    p = p / jnp.sum(p, axis=1, keepdims=True)
    o_ref[0:n_meta, :] = finish((p[:n_meta] - lam * p[n_meta:]).astype(BF16), first)

    ci, ri = block_coords(tile, tile)
    causal = ci <= ri
    not_pad = block_coords(tile, first)[0] >= pad
    row_chunk = 64

    def key_block(j):
        return (0, first) if j == 0 else (first + (j - 1) * tile, first + j * tile)

    def score_tile(it):
        qoff = n_meta + it * tile
        qs = stack(q_ref[qoff:qoff + tile, :])
        for j in range(it + 2):
            lo, hi = key_block(j)
            s = lax.dot_general(qs, kp_scr[lo:hi, :], _NT, preferred_element_type=F32)
            if j == 0:
                s = jnp.where(not_pad, s, MASK_VALUE)
            if j == it + 1:
                s = jnp.where(causal, s, MASK_VALUE)
            s_scr[it % 2, :, lo:hi] = s

    def softmax_tile(it):
        buf = it % 2
        qoff = n_meta + it * tile
        width = key_block(it + 1)[1]
        col_chunks = [slice(c * LANES, (c + 1) * LANES) for c in range(width // LANES)]

        def normaliser(rsl):
            mx = functools.reduce(jnp.maximum, [s_scr[buf, rsl, csl] for csl in col_chunks])
            m_b = jnp.broadcast_to(jnp.max(mx, axis=1, keepdims=True), (row_chunk, LANES))
            total = jnp.zeros((row_chunk, LANES), F32)
            for csl in col_chunks:
                p = jnp.exp2(s_scr[buf, rsl, csl] - m_b)
                s_scr[buf, rsl, csl] = p
                total = total + p
            return jnp.broadcast_to(1.0 / jnp.sum(total, axis=1, keepdims=True), (row_chunk, LANES))

        for rc in range(tile // row_chunk):
            rows1 = slice(rc * row_chunk, (rc + 1) * row_chunk)
            rows2 = slice(tile + rc * row_chunk, tile + (rc + 1) * row_chunk)
            c1 = normaliser(rows1)
            c2 = lam * normaliser(rows2)
            for csl in col_chunks:
                a_scr[buf, rows1, csl] = (s_scr[buf, rows1, csl] * c1 - s_scr[buf, rows2, csl] * c2).astype(BF16)
        o_ref[qoff:qoff + tile, :] = finish(a_scr[buf, :, 0:width], width)

    score_tile(0)
    for it in range(n_tiles):
        if it + 1 < n_tiles:
            score_tile(it + 1)
        softmax_tile(it)


def _diff_attention(q, k, v, lam_params, g_sub, batch, seq, n_meta):
    m, n = q.shape
    dv = g_sub.shape[1]
    tile = ATTN_TILE
    n_tiles = (seq - n_meta) // tile
    assert n_meta + n_tiles * tile == seq and n == DIFF_HEADS * dv
    assert n_meta <= LANES and (LANES - n_meta) % BF16_ROWS == 0
    padded = LANES + n_tiles * tile
    kern = functools.partial(_diff_attn_kernel, n_meta=n_meta, tile=tile, n_tiles=n_tiles)
    head_spec = pl.BlockSpec((seq, dv), lambda b, hh: (b, hh))
    return pl.pallas_call(
        kern,
        grid=(batch, DIFF_HEADS),
        in_specs=[
            pl.BlockSpec(lam_params.shape, lambda b, hh: (0, 0)),
            head_spec, head_spec, head_spec,
            pl.BlockSpec((1, dv), lambda b, hh: (0, 0)),
        ],
        out_specs=head_spec,
        out_shape=jax.ShapeDtypeStruct((m, n), BF16),
        scratch_shapes=[pltpu.VMEM((padded, dv), BF16), pltpu.VMEM((padded, dv), BF16),
                        pltpu.VMEM((2, 2 * tile, padded), F32), pltpu.VMEM((2, tile, padded), BF16)],
        compiler_params=_params("arbitrary", "arbitrary"),
        name="diff_attn",
    )(lam_params, q, k, v, g_sub)


def kernel(x_prompt, x_sample, state_gla, cache_k, cache_v, page_table, meta_tokens,
           gla_w_in, gla_w_gate, gla_b_gate, gla_norm, gla_w_out,
           diff_w_in, diff_lambda, diff_norm, diff_w_out,
           mlp_w1, mlp_w2, ln_mix_g, ln_mix_b, ln_mlp_g, ln_mlp_b):
    batch, seq_new, d = x_prompt.shape
    n_meta = meta_tokens.shape[0]
    seq = n_meta + seq_new
    bs = x_sample.shape[0]
    assert x_sample.shape[1] == 1 and bs % 2 == 0

    meta = jnp.broadcast_to(meta_tokens[None].astype(x_prompt.dtype), (batch, n_meta, d))
    x_p = jnp.concatenate([meta, x_prompt], axis=1).reshape(batch * seq, d)
    x_s = x_sample.reshape(bs, d)

    def row(vec):
        return vec.reshape(1, -1).astype(F32)

    def finish_layer(layer, o, x, w_out, decode=None):
        return _finish_layer(o, w_out, x, (row(ln_mix_g[layer]), row(ln_mix_b[layer])),
                             mlp_w1[layer].astype(BF16), mlp_w2[layer].astype(BF16),
                             (row(ln_mlp_g[layer]), row(ln_mlp_b[layer])), decode)

    rank = gla_w_gate.shape[1]
    n_main = gla_w_in.shape[2] - rank
    w_main = gla_w_in[0, :, :n_main].astype(BF16)
    w_glow = jnp.pad(gla_w_in[0, :, n_main:], ((0, 0), (0, LANES - rank))).astype(BF16)
    w_gate = jnp.pad(gla_w_gate[0], ((0, LANES - rank), (0, 0))).astype(BF16)
    b_gate = row(gla_b_gate[0])
    g_norm = row(gla_norm[0])
    w_out0 = gla_w_out[0].astype(BF16)
    w_in1 = diff_w_in[0].astype(BF16)
    w_out1 = diff_w_out[0].astype(BF16)
    g_sub = row(diff_norm[0])
    lam_params = diff_lambda[0].astype(F32)
    dv = g_sub.shape[1]
    past = page_table.shape[1] * PAGE_SIZE

    h_s, lg_s = _gla_project(x_s, w_main, w_glow, w_gate, b_gate)
    o_s, st_s = _gla_step(h_s, lg_s, g_norm, state_gla[0].astype(F32))
    x_s = finish_layer(0, o_s, x_s, w_out0)
    q_s, k_s, _, v_s, _ = _diff_project(x_s, w_in1, jnp.full((bs,), past))
    heads = (bs, DIFF_HEADS, dv)
    q_s3, k_s3, v_s3 = q_s.astype(F32).reshape(heads), k_s.reshape(heads), v_s.reshape(heads)

    def hosted(lo):
        hi = lo + bs // 2
        return (q_s3[lo:hi], k_s3[lo:hi], v_s3[lo:hi], cache_k[0], cache_v[0], page_table[lo:hi],
                lam_params, g_sub)

    h_p, lg_p = _gla_project(x_p, w_main, w_glow, w_gate, b_gate)
    o_p, st_p = _gla_core(h_p, lg_p, g_norm, batch, seq, n_meta)
    x_p, a_s_lo = finish_layer(0, o_p, x_p, w_out0, hosted(0))

    q_p, k_p, kb_p, v_p, vb_p = _diff_project(x_p, w_in1, jnp.arange(seq))
    a_p = _diff_attention(q_p, kb_p, vb_p, lam_params, g_sub, batch, seq, n_meta)
    x_p, a_s_hi = finish_layer(1, a_p, x_p, w_out1, hosted(bs // 2))

    a_s = jnp.concatenate([a_s_lo, a_s_hi], axis=0).reshape(bs, DIFF_HEADS * dv).astype(BF16)
    x_s = finish_layer(1, a_s, x_s, w_out1)

    y_prompt = x_p.reshape(batch, seq, d)[:, n_meta:]
    y_sample = x_s.reshape(bs, 1, d)
    return (y_prompt, y_sample, st_p[None], st_s[None],
            k_p.reshape(1, batch, seq, DIFF_HEADS, dv), v_p.reshape(1, batch, seq, DIFF_HEADS, dv),
            k_s.reshape(1, bs, 1, DIFF_HEADS, dv), v_s.reshape(1, bs, 1, DIFF_HEADS, dv))
```

```python
import functools
import math

import jax
import jax.numpy as jnp
from jax import lax
from jax.experimental import pallas as pl
from jax.experimental.pallas import tpu as pltpu

F32 = jnp.float32
BF16 = jnp.bfloat16

DEPTH = 2
GLA_HEADS = 4
GLA_TAU = 16.0
DIFF_HEADS = 8
ROPE_THETA = 500000.0
PAGE_SIZE = 128
LN_EPS = 1e-5
DEEPNORM_ALPHA = (2 * DEPTH) ** 0.25
LAM_INIT = 0.8 - 0.6 * math.exp(-0.3 * 1)

LANES = 128
SUBLANES = 8
BF16_ROWS = 16
VMEM_LIMIT = 48 * 1024 * 1024

GLA_CHUNK = 128
GLA_HEADS_PER_STEP = 2
GLA_STEP_SEQS = 8
GLA_SAFE_LOG_SPAN = 80.0
ATTN_TILE = 256
MLP_FF_TILE = 512
HOST_TILES_PER_SEQ = 1
VMEM_LIMIT_HOST = 60000 * 1024
MASK_VALUE = -1e30
LOG2_E = 1.4426950408889634

_NT = (((1,), (1,)), ((), ()))
_TN = (((0,), (0,)), ((), ()))


def _row_tile(m):
    for cand in (688, 512, 256, 128, 64, 32, 16, 8):
        if m % cand == 0:
            return cand
    raise ValueError(f"row count {m} is not a multiple of 8")


def _col_tile(n):
    for cand in (512, 256, 128):
        if n % cand == 0:
            return cand
    raise ValueError(f"column count {n} is not a multiple of 128")


def _params(*sem, limit=VMEM_LIMIT):
    return pltpu.CompilerParams(dimension_semantics=sem, vmem_limit_bytes=limit)


def _layer_norm(y, g, b):
    mu = jnp.mean(y, axis=-1, keepdims=True)
    yc = y - mu
    var = jnp.mean(yc * yc, axis=-1, keepdims=True)
    return yc * lax.rsqrt(var + LN_EPS) * g + b


def _rms_norm(y, g):
    return y * lax.rsqrt(jnp.mean(y * y, axis=-1, keepdims=True) + LN_EPS) * g


def _lane_col(row, n):
    return jnp.transpose(jnp.broadcast_to(row, (n, n)))


def _gla_proj_kernel(x_ref, w_ref, wg_ref, wgate_ref, bgate_ref, qk_ref, vr_ref, lg_ref, *, tn):
    xb = x_ref[...].astype(BF16)
    g_low = jnp.dot(xb, wg_ref[...], preferred_element_type=F32)
    z = jnp.dot(g_low.astype(BF16), wgate_ref[...], preferred_element_type=F32) + bgate_ref[...]
    log_sig = jnp.minimum(z, 0.0) - jnp.log(1.0 + jnp.exp(-jnp.abs(z)))
    lg_ref[...] = log_sig * (1.0 / GLA_TAU)
    n_qk = qk_ref.shape[1]
    for j in range((n_qk + vr_ref.shape[1]) // tn):
        y = jnp.dot(xb, w_ref[:, j * tn:(j + 1) * tn], preferred_element_type=F32)
        if j * tn < n_qk:
            qk_ref[:, j * tn:(j + 1) * tn] = y
        else:
            vr_ref[:, j * tn - n_qk:(j + 1) * tn - n_qk] = y.astype(vr_ref.dtype)


def _gla_project(x, w_in, w_glow, w_gate, b_gate, dv):
    m, d = x.shape
    ng = w_gate.shape[1]
    n_qk, n_vr = 2 * ng, 2 * GLA_HEADS * dv
    assert w_in.shape[1] >= n_qk + n_vr
    tm = _row_tile(m)
    return pl.pallas_call(
        functools.partial(_gla_proj_kernel, tn=_col_tile(n_qk)),
        grid=(m // tm,),
        in_specs=[
            pl.BlockSpec((tm, d), lambda i: (i, 0)),
            pl.BlockSpec((d, n_qk + n_vr), lambda i: (0, 0)),
            pl.BlockSpec((d, LANES), lambda i: (0, 0)),
            pl.BlockSpec((LANES, ng), lambda i: (0, 0)),
            pl.BlockSpec((1, ng), lambda i: (0, 0)),
        ],
        out_specs=[
            pl.BlockSpec((tm, n_qk), lambda i: (i, 0)),
            pl.BlockSpec((tm, n_vr), lambda i: (i, 0)),
            pl.BlockSpec((tm, ng), lambda i: (i, 0)),
        ],
        out_shape=[jax.ShapeDtypeStruct((m, n_qk), F32), jax.ShapeDtypeStruct((m, n_vr), BF16),
                   jax.ShapeDtypeStruct((m, ng), F32)],
        compiler_params=_params("arbitrary"),
        name="gla_proj",
    )(x, w_in, w_glow, w_gate, b_gate)


def _gla_core_kernel(q_ref, k_ref, v_ref, r_ref, lg_ref, gn_ref, o_ref, st_ref, s_scr,
                     *, n_meta, chunk, n_chunks, dk, dv):
    wk = q_ref.shape[1]
    heads = range(wk // dk)
    scale = dk ** -0.5
    s_scr[...] = jnp.zeros_like(s_scr)

    def do_chunk(off, length):
        rows = pl.ds(off, length)
        ii = lax.broadcasted_iota(jnp.int32, (length, length), 0)
        jj = lax.broadcasted_iota(jnp.int32, (length, length), 1)
        causal = ii >= jj
        lg = lg_ref[rows, :]
        hi = lg.astype(BF16)
        rest = lg - hi.astype(F32)
        mid = rest.astype(BF16)
        lo = (rest - mid.astype(F32)).astype(BF16)
        parts = jnp.dot(causal.astype(BF16), jnp.concatenate([hi, mid, lo], axis=1),
                        preferred_element_type=F32)
        b = parts[:, 0:wk] + parts[:, wk:2 * wk] + parts[:, 2 * wk:3 * wk]
        q = q_ref[rows, :] * scale
        k = k_ref[rows, :]
        b_mid = b[length // 2:length // 2 + 1, :]
        b_last = b[length - 1:length, :]
        q_in = (q * jnp.exp(b)).astype(BF16)
        qs = (q * jnp.exp(b - b_mid)).astype(BF16)
        ks = (k * jnp.exp(b_mid - b)).astype(BF16)
        kd = (k * jnp.exp(b_last - b)).astype(BF16)
        decay = jnp.exp(b_last)
        vb = v_ref[rows, :]

        def kcols(hh):
            return slice(hh * dk, (hh + 1) * dk)

        def vcols(hh):
            return slice(hh * dv, (hh + 1) * dv)

        states = [s_scr[hh] for hh in heads]
        o_inter = [lax.dot_general(q_in[:, kcols(hh)], states[hh].astype(BF16), _NT,
                                   preferred_element_type=F32) for hh in heads]
        scores = [lax.dot_general(qs[:, kcols(hh)], ks[:, kcols(hh)], _NT, preferred_element_type=F32)
                  for hh in heads]
        scores = [jnp.where(causal, s, 0.0).astype(BF16) for s in scores]
        o_intra = [jnp.dot(scores[hh], vb[:, vcols(hh)], preferred_element_type=F32) for hh in heads]
        grown = [lax.dot_general(vb[:, vcols(hh)], kd[:, kcols(hh)], _TN, preferred_element_type=F32)
                 for hh in heads]
        for hh in heads:
            s_scr[hh] = states[hh] * decay[:, kcols(hh)] + grown[hh]
            r = r_ref[rows, vcols(hh)].astype(F32)
            gated = _rms_norm(o_inter[hh] + o_intra[hh], gn_ref[...]) * (r / (1.0 + jnp.exp(-r)))
            o_ref[rows, vcols(hh)] = gated.astype(o_ref.dtype)

    def token_by_token():
        group = BF16_ROWS
        seq = q_ref.shape[0]

        def as_rows(row):
            return jnp.concatenate([row, jnp.zeros((SUBLANES - 1, row.shape[1]), row.dtype)], axis=0)

        def body(g, carry):
            rows = pl.ds(pl.multiple_of(g * group, group), group)
            a_g = jnp.exp(lg_ref[rows, :])
            q_g = q_ref[rows, :] * scale
            k_g = k_ref[rows, :]
            v_g = v_ref[rows, :].astype(F32)
            r_g = r_ref[rows, :].astype(F32)
            outs = [[] for _ in heads]
            for t in range(group):
                a, q, k, v, r = (x[t:t + 1] for x in (a_g, q_g, k_g, v_g, r_g))
                for hh in heads:
                    kc, vc = slice(hh * dk, (hh + 1) * dk), slice(hh * dv, (hh + 1) * dv)
                    grown = lax.dot_general(as_rows(v[:, vc]).astype(BF16), as_rows(k[:, kc]).astype(BF16),
                                            _TN, preferred_element_type=F32)
                    s_new = s_scr[hh] * a[:, kc] + grown
                    s_scr[hh] = s_new
                    o = lax.dot_general(as_rows(q[:, kc]).astype(BF16), s_new.astype(BF16), _NT,
                                        preferred_element_type=F32)[0:1]
                    rr = r[:, vc]
                    outs[hh].append(_rms_norm(o, gn_ref[...]) * (rr / (1.0 + jnp.exp(-rr))))
            for hh in heads:
                o_ref[rows, hh * dv:(hh + 1) * dv] = jnp.concatenate(
                    outs[hh], axis=0).astype(o_ref.dtype)
            return carry

        lax.fori_loop(0, seq // group, body, 0)

    half_span = max(chunk, n_meta) // 2
    chunked_is_safe = jnp.min(lg_ref[...]) * half_span > -GLA_SAFE_LOG_SPAN

    @pl.when(chunked_is_safe)
    def _():
        do_chunk(0, n_meta)
        for c in range(n_chunks):
            do_chunk(n_meta + c * chunk, chunk)

    pl.when(jnp.logical_not(chunked_is_safe))(token_by_token)

    for hh in heads:
        st_ref[0, hh] = jnp.transpose(s_scr[hh])


def _gla_core(qk, vr, lg, g_norm, batch, seq, n_meta):
    m = qk.shape[0]
    hk = lg.shape[1]
    dk = hk // GLA_HEADS
    dv = g_norm.shape[1]
    hv = GLA_HEADS * dv
    assert qk.shape[1] == 2 * hk and vr.shape[1] == 2 * hv and m == batch * seq
    chunk = GLA_CHUNK
    n_chunks = (seq - n_meta) // chunk
    assert n_meta + n_chunks * chunk == seq
    hps = GLA_HEADS_PER_STEP
    groups = GLA_HEADS // hps
    wk, wv = hps * dk, hps * dv
    kern = functools.partial(_gla_core_kernel, n_meta=n_meta, chunk=chunk, n_chunks=n_chunks, dk=dk, dv=dv)
    return pl.pallas_call(
        kern,
        grid=(batch, groups),
        in_specs=[
            pl.BlockSpec((seq, wk), lambda b, g: (b, g)),
            pl.BlockSpec((seq, wk), lambda b, g: (b, groups + g)),
            pl.BlockSpec((seq, wv), lambda b, g: (b, g)),
            pl.BlockSpec((seq, wv), lambda b, g: (b, groups + g)),
            pl.BlockSpec((seq, wk), lambda b, g: (b, g)),
            pl.BlockSpec((1, dv), lambda b, g: (0, 0)),
        ],
        out_specs=[
            pl.BlockSpec((seq, wv), lambda b, g: (b, g)),
            pl.BlockSpec((1, hps, dk, dv), lambda b, g: (b, g, 0, 0)),
        ],
        out_shape=[jax.ShapeDtypeStruct((m, hv), BF16),
                   jax.ShapeDtypeStruct((batch, GLA_HEADS, dk, dv), F32)],
        scratch_shapes=[pltpu.VMEM((hps, dv, dk), F32)],
        compiler_params=_params("arbitrary", "arbitrary"),
        name="gla_core",
    )(qk, qk, vr, vr, lg, g_norm)


def _gla_step_kernel(qk_ref, vr_ref, lg_ref, gn_ref, s_ref, o_ref, sn_ref):
    dk, dv = s_ref.shape[2], s_ref.shape[3]
    hk, hv = GLA_HEADS * dk, GLA_HEADS * dv
    scale = dk ** -0.5
    reps = dv // dk
    for sq in range(s_ref.shape[0]):
        for hh in range(GLA_HEADS):
            q = qk_ref[sq, :, hh * dk:(hh + 1) * dk] * scale
            k = qk_ref[sq, :, hk + hh * dk:hk + (hh + 1) * dk]
            v = vr_ref[sq, :, hh * dv:(hh + 1) * dv]
            r = vr_ref[sq, :, hv + hh * dv:hv + (hh + 1) * dv]
            a = jnp.exp(lg_ref[sq, :, hh * dk:(hh + 1) * dk])
            a_c = jnp.tile(_lane_col(a, dk), (1, reps))
            k_c = jnp.tile(_lane_col(k, dk), (1, reps))
            q_c = jnp.tile(_lane_col(q, dk), (1, reps))
            s_new = s_ref[sq, hh] * a_c + k_c * v
            sn_ref[sq, hh] = s_new
            o = jnp.sum(q_c * s_new, axis=0, keepdims=True)
            gated = _rms_norm(o, gn_ref[...]) * (r / (1.0 + jnp.exp(-r)))
            o_ref[sq, :, hh * dv:(hh + 1) * dv] = gated.astype(o_ref.dtype)


def _gla_step(qk, vr, lg, g_norm, state):
    bs = qk.shape[0]
    _, _, dk, dv = state.shape
    hv = GLA_HEADS * dv
    nb = math.gcd(bs, GLA_STEP_SEQS)
    o, s_new = pl.pallas_call(
        _gla_step_kernel,
        grid=(bs // nb,),
        in_specs=[
            pl.BlockSpec((nb, 1, qk.shape[1]), lambda b: (b, 0, 0)),
            pl.BlockSpec((nb, 1, vr.shape[1]), lambda b: (b, 0, 0)),
            pl.BlockSpec((nb, 1, lg.shape[1]), lambda b: (b, 0, 0)),
            pl.BlockSpec((1, dv), lambda b: (0, 0)),
            pl.BlockSpec((nb, GLA_HEADS, dk, dv), lambda b: (b, 0, 0, 0)),
        ],
        out_specs=[
            pl.BlockSpec((nb, 1, hv), lambda b: (b, 0, 0)),
            pl.BlockSpec((nb, GLA_HEADS, dk, dv), lambda b: (b, 0, 0, 0)),
        ],
        out_shape=[jax.ShapeDtypeStruct((bs, 1, hv), BF16), jax.ShapeDtypeStruct(state.shape, F32)],
        compiler_params=_params("arbitrary"),
        name="gla_step",
    )(qk.reshape(bs, 1, -1), vr.reshape(bs, 1, -1), lg.reshape(bs, 1, -1), g_norm, state)
    return o.reshape(bs, hv), s_new


def _lambda_value(lp):
    e1 = jnp.exp(jnp.sum(lp[0:1] * lp[1:2], axis=1, keepdims=True))
    e2 = jnp.exp(jnp.sum(lp[2:3] * lp[3:4], axis=1, keepdims=True))
    return e1 - e2 + LAM_INIT


class _Decode:
    def __init__(self, q_ref, m_scr, l_scr, acc_scr):
        self.m_scr, self.l_scr, self.acc_scr = m_scr, l_scr, acc_scr
        _, self.n_head, self.dv = q_ref.shape
        qt = q_ref[0]
        first_half = lax.broadcasted_iota(jnp.int32, qt.shape, 1) < self.dv // 2
        self.q_rows = jnp.concatenate([jnp.where(first_half, qt, 0.0), jnp.where(first_half, 0.0, qt)],
                                      axis=0).astype(BF16)

    def init(self):
        self.m_scr[...] = jnp.full(self.m_scr.shape, MASK_VALUE, F32)
        self.l_scr[...] = jnp.zeros_like(self.l_scr)
        self.acc_scr[...] = jnp.zeros_like(self.acc_scr)

    def own_head(self, n, n_valid):
        shape = (2 * self.n_head, n)
        col = lax.broadcasted_iota(jnp.int32, shape, 1)
        row = lax.broadcasted_iota(jnp.int32, shape, 0)
        return jnp.logical_and(col % self.n_head == row % self.n_head, col < n_valid)

    def scores(self, ks, own):
        return [jnp.where(own, lax.dot_general(self.q_rows, k2.astype(BF16), _NT, preferred_element_type=F32),
                          MASK_VALUE) for k2 in ks]

    def accumulate(self, ss, vs):
        m_old = self.m_scr[...]
        m_new = jnp.maximum(m_old, jnp.max(functools.reduce(jnp.maximum, ss), axis=1, keepdims=True))
        alpha = jnp.exp2(m_old - m_new)
        ps = [jnp.exp2(s - m_new) for s in ss]
        self.m_scr[...] = m_new
        self.l_scr[...] = alpha * self.l_scr[...] + jnp.sum(functools.reduce(jnp.add, ps), axis=1, keepdims=True)
        pv = [jnp.dot(p.astype(BF16), v2.astype(BF16), preferred_element_type=F32) for p, v2 in zip(ps, vs)]
        self.acc_scr[...] = alpha * self.acc_scr[...] + functools.reduce(jnp.add, pv)

    def page_scores(self, k_refs):
        n = k_refs[0].shape[1] * self.n_head
        return self.scores([r[0].reshape(n, self.dv) for r in k_refs], self.own_head(n, n))

    def accumulate_pages(self, ss, v_refs):
        n = v_refs[0].shape[1] * self.n_head
        self.accumulate(ss, [r[0].reshape(n, self.dv) for r in v_refs])

    def finish(self, kc_ref, vc_ref, lam_ref, g_ref, o_ref):
        fill = jnp.zeros((LANES - self.n_head, self.dv), F32)
        ss = self.scores([jnp.concatenate([kc_ref[0], fill], axis=0)], self.own_head(LANES, self.n_head))
        self.accumulate(ss, [jnp.concatenate([vc_ref[0], fill], axis=0)])
        ratio = self.acc_scr[...] / self.l_scr[...]
        o = ratio[:self.n_head] - _lambda_value(lam_ref[...]) * ratio[self.n_head:]
        o_ref[0] = _rms_norm(o, g_ref[...]) * (1.0 - LAM_INIT)


def _finish_kernel(*refs, n_pages_step, tiles_per_seq):
    hosting = n_pages_step > 0
    if hosting:
        refs = refs[1:]
    a_ref, wo_ref, x_ref, g1_ref, b1_ref, w1_ref, w2_ref, g2_ref, b2_ref = refs[:9]
    refs = refs[9:]
    if hosting:
        lam_ref, q_ref, kc_ref, vc_ref, gs_ref = refs[:5]
        k_refs = refs[5:5 + n_pages_step]
        v_refs = refs[5 + n_pages_step:5 + 2 * n_pages_step]
        o_ref, od_ref, xb_scr, acc_scr, m_scr, l_scr, dacc_scr = refs[5 + 2 * n_pages_step:]
        dec = _Decode(q_ref, m_scr, l_scr, dacc_scr)
    else:
        o_ref, xb_scr, acc_scr = refs
    f = pl.program_id(1)
    last_f = pl.num_programs(1) - 1
    tile_in_seq = pl.program_id(0) % tiles_per_seq

    @pl.when(f == 0)
    def _():
        proj = jnp.dot(a_ref[...], wo_ref[...], preferred_element_type=F32)
        x1 = _layer_norm(DEEPNORM_ALPHA * x_ref[...] + proj, g1_ref[...], b1_ref[...])
        o_ref[...] = x1
        xb_scr[...] = x1.astype(BF16)
        acc_scr[...] = jnp.zeros_like(acc_scr)

    if hosting:
        pl.when(jnp.logical_and(f == 0, tile_in_seq == 0))(dec.init)

    if hosting:
        page_scores = dec.page_scores(k_refs)
    hid = jnp.maximum(jnp.dot(xb_scr[...], w1_ref[...], preferred_element_type=F32), 0.0)
    if hosting:
        dec.accumulate_pages(page_scores, v_refs)
    acc_scr[...] += jnp.dot((hid * hid).astype(BF16), w2_ref[...], preferred_element_type=F32)

    @pl.when(f == last_f)
    def _():
        o_ref[...] = _layer_norm(DEEPNORM_ALPHA * o_ref[...] + acc_scr[...], g2_ref[...], b2_ref[...])

    if hosting:
        @pl.when(jnp.logical_and(f == last_f, tile_in_seq == tiles_per_seq - 1))
        def _():
            dec.finish(kc_ref, vc_ref, lam_ref, gs_ref, od_ref)


def _finish_layer(a, w_out, x, ln1, w1, w2, layer, ln2, decode=None):
    m, kk = a.shape
    d = w_out.shape[1]
    ff = w1.shape[2]
    tf = min(MLP_FF_TILE, ff)
    n_f = ff // tf
    tps = HOST_TILES_PER_SEQ
    if decode is None:
        tm, pps = _row_tile(m), 0
        n_tiles = m // tm
    else:
        q, k_cur, v_cur, cache_k, cache_v, page_table, lam_params, g_sub = decode
        n_seqs, n_head, dv = q.shape
        n_tiles = n_seqs * tps
        tm = pl.cdiv(pl.cdiv(m, n_tiles), BF16_ROWS) * BF16_ROWS
        n_pages, page = page_table.shape[1], cache_k.shape[1]
        assert n_pages % (tps * n_f) == 0 and pl.cdiv(m, tm) == n_tiles
        pps = n_pages // (tps * n_f)
    row_spec = pl.BlockSpec((tm, d), lambda i, f, *_: (i, 0))
    vec_spec = pl.BlockSpec((1, d), lambda i, f, *_: (0, 0))
    in_specs = [
        pl.BlockSpec((tm, kk), lambda i, f, *_: (i, 0)),
        pl.BlockSpec((kk, d), lambda i, f, *_: (0, 0)),
        row_spec, vec_spec, vec_spec,
        pl.BlockSpec((None, d, tf), lambda i, f, *_: (layer, 0, f)),
        pl.BlockSpec((None, tf, d), lambda i, f, *_: (layer, f, 0)),
        vec_spec, vec_spec,
    ]
    args = [a, w_out, x, *ln1, w1, w2, *ln2]
    out_specs = [row_spec]
    out_shape = [jax.ShapeDtypeStruct((m, d), F32)]
    scratch = [pltpu.VMEM((tm, d), BF16), pltpu.VMEM((tm, d), F32)]
    prefetch = []
    if decode is not None:
        head_spec = pl.BlockSpec((1, n_head, dv), lambda i, f, pt: (i // tps, 0, 0))

        def page_spec(s):
            return pl.BlockSpec((1, page, n_head, dv),
                                lambda i, f, pt: (pt[i // tps, ((i % tps) * n_f + f) * pps + s], 0, 0, 0))

        in_specs += [pl.BlockSpec(lam_params.shape, lambda i, f, pt: (0, 0)), head_spec, head_spec, head_spec,
                     pl.BlockSpec((1, dv), lambda i, f, pt: (0, 0))]
        in_specs += [page_spec(s) for s in range(pps)] * 2
        args += [lam_params, q, k_cur, v_cur, g_sub] + [cache_k] * pps + [cache_v] * pps
        out_specs.append(head_spec)
        out_shape.append(jax.ShapeDtypeStruct((n_seqs, n_head, dv), F32))
        scratch += [pltpu.VMEM((2 * n_head, 1), F32), pltpu.VMEM((2 * n_head, 1), F32),
                    pltpu.VMEM((2 * n_head, dv), F32)]
        prefetch = [page_table]
    outs = pl.pallas_call(
        functools.partial(_finish_kernel, n_pages_step=pps, tiles_per_seq=tps),
        grid_spec=pltpu.PrefetchScalarGridSpec(
            num_scalar_prefetch=len(prefetch), grid=(n_tiles, n_f),
            in_specs=in_specs, out_specs=out_specs, scratch_shapes=scratch),
        out_shape=out_shape,
        compiler_params=_params("arbitrary", "arbitrary",
                                limit=VMEM_LIMIT if decode is None else VMEM_LIMIT_HOST),
        name="finish_layer" if decode is None else "finish_layer_host",
    )(*prefetch, *args)
    return outs[0] if decode is None else tuple(outs)


def _diff_proj_kernel(x_ref, w_ref, cos_ref, sin_up_ref, sin_dn_ref,
                      q_ref, k_ref, kb_ref, v_ref, vb_ref, *, tn, q_scale, rot_half):
    xb = x_ref[...].astype(BF16)
    nq = q_ref.shape[1]

    def project(seg, j):
        lo = seg * nq + j * tn
        return jnp.dot(xb, w_ref[:, lo:lo + tn], preferred_element_type=F32)

    def rope(t):
        groups = []
        for g in range(tn // LANES):
            tg = t[:, g * LANES:(g + 1) * LANES]
            groups.append(tg * cos_ref[...] + pltpu.roll(tg, LANES - rot_half, 1) * sin_up_ref[...]
                          + pltpu.roll(tg, rot_half, 1) * sin_dn_ref[...])
        return jnp.concatenate(groups, axis=1)

    for j in range(nq // tn):
        cols = slice(j * tn, (j + 1) * tn)
        q_ref[:, cols] = (rope(project(0, j)) * q_scale).astype(q_ref.dtype)
        kr = rope(project(1, j))
        k_ref[:, cols] = kr
        kb_ref[:, cols] = kr.astype(kb_ref.dtype)
        y = project(2, j)
        v_ref[:, cols] = y
        vb_ref[:, cols] = y.astype(vb_ref.dtype)


def _rope_tables(pos, dh):
    rot = dh // 4
    inv = ROPE_THETA ** (-jnp.arange(0, rot, 2, dtype=F32) / rot)
    ang = pos.astype(F32)[:, None] * inv[None, :]
    n = pos.shape[0]

    def lanes(first, second, fill):
        sub = jnp.concatenate([first, second, jnp.full((n, dh - rot), fill, F32)], axis=1)
        return jnp.tile(sub, (1, LANES // dh))

    zeros = jnp.zeros_like(ang)
    return (lanes(jnp.cos(ang), jnp.cos(ang), 1.0),
            lanes(-jnp.sin(ang), zeros, 0.0),
            lanes(zeros, jnp.sin(ang), 0.0))


def _diff_project(x, w, pos):
    m, d = x.shape
    n = w.shape[1]
    nq = n // 3
    dh = nq // (2 * DIFF_HEADS)
    assert m % pos.shape[0] == 0
    tm = _row_tile(pos.shape[0])
    pos_tiles = pos.shape[0] // tm
    cos_t, sin_up, sin_dn = _rope_tables(pos, dh)
    kern = functools.partial(_diff_proj_kernel, tn=_col_tile(nq), q_scale=dh ** -0.5 * LOG2_E, rot_half=dh // 8)
    tab_spec = pl.BlockSpec((tm, LANES), lambda i: (i % pos_tiles, 0))
    out_spec = pl.BlockSpec((tm, nq), lambda i: (i, 0))
    return pl.pallas_call(
        kern,
        grid=(m // tm,),
        in_specs=[
            pl.BlockSpec((tm, d), lambda i: (i, 0)),
            pl.BlockSpec((d, n), lambda i: (0, 0)),
            tab_spec, tab_spec, tab_spec,
        ],
        out_specs=[out_spec] * 5,
        out_shape=[jax.ShapeDtypeStruct((m, nq), BF16),
                   jax.ShapeDtypeStruct((m, nq), F32), jax.ShapeDtypeStruct((m, nq), BF16),
                   jax.ShapeDtypeStruct((m, nq), F32), jax.ShapeDtypeStruct((m, nq), BF16)],
        compiler_params=_params("arbitrary"),
        name="diff_proj",
    )(x, w, cos_t, sin_up, sin_dn)


def _diff_attn_kernel(lam_ref, q_ref, k_ref, v_ref, g_ref, o_ref, kp_scr, vp_scr, s_scr, a_scr,
                      *, n_meta, tile, n_tiles):
    lam = _lambda_value(lam_ref[...])
    seq, dq = k_ref.shape
    first = LANES
    pad = first - n_meta
    kp_scr[0:pad, :] = jnp.zeros((pad, dq), BF16)
    kp_scr[pad:pad + seq, :] = k_ref[...]
    vp_scr[0:pad, :] = jnp.zeros((pad, v_ref.shape[1]), BF16)
    vp_scr[pad:pad + seq, :] = v_ref[...]
    first_half = lax.broadcasted_iota(jnp.int32, (1, dq), 1) < dq // 2

    def stack(q):
        zero = jnp.zeros_like(q)
        return jnp.concatenate([jnp.where(first_half, q, zero), jnp.where(first_half, zero, q)], axis=0)

    def block_coords(length, width):
        ri = lax.broadcasted_iota(jnp.int32, (2 * length, width), 0)
        ri = jnp.where(ri >= length, ri - length, ri)
        ci = lax.broadcasted_iota(jnp.int32, (2 * length, width), 1)
        return ci, ri

    def finish(weights, width):
        o = jnp.dot(weights, vp_scr[0:width, :], preferred_element_type=F32)
        return (_rms_norm(o, g_ref[...]) * (1.0 - LAM_INIT)).astype(o_ref.dtype)

    ci, ri = block_coords(n_meta, first)
    s = lax.dot_general(stack(q_ref[0:n_meta, :]), kp_scr[0:first, :], _NT, preferred_element_type=F32)
    s = jnp.where(jnp.logical_and(ci >= pad, ci - pad <= ri), s, MASK_VALUE)
    p = jnp.exp2(s - jnp.max(s, axis=1, keepdims=True))
    p = p / jnp.sum(p, axis=1, keepdims=True)
    o_ref[0:n_meta, :] = finish((p[:n_meta] - lam * p[n_meta:]).astype(BF16), first)

    ci, ri = block_coords(tile, tile)
    causal = ci <= ri
    not_pad = block_coords(tile, first)[0] >= pad
    row_chunk = 64

    def key_block(j):
        return (0, first) if j == 0 else (first + (j - 1) * tile, first + j * tile)

    def score_tile(it):
        qoff = n_meta + it * tile
        qs = stack(q_ref[qoff:qoff + tile, :])
        for j in range(it + 2):
            lo, hi = key_block(j)
            s = lax.dot_general(qs, kp_scr[lo:hi, :], _NT, preferred_element_type=F32)
            if j == 0:
                s = jnp.where(not_pad, s, MASK_VALUE)
            if j == it + 1:
                s = jnp.where(causal, s, MASK_VALUE)
            s_scr[it % 2, :, lo:hi] = s

    def softmax_tile(it):
        buf = it % 2
        qoff = n_meta + it * tile
        width = key_block(it + 1)[1]
        col_chunks = [slice(c * LANES, (c + 1) * LANES) for c in range(width // LANES)]

        def normaliser(rsl):
            mx = functools.reduce(jnp.maximum, [s_scr[buf, rsl, csl] for csl in col_chunks])
            m_b = jnp.broadcast_to(jnp.max(mx, axis=1, keepdims=True), (row_chunk, LANES))
            total = jnp.zeros((row_chunk, LANES), F32)
            for csl in col_chunks:
                p = jnp.exp2(s_scr[buf, rsl, csl] - m_b)
                s_scr[buf, rsl, csl] = p
                total = total + p
            return jnp.broadcast_to(1.0 / jnp.sum(total, axis=1, keepdims=True), (row_chunk, LANES))

        for rc in range(tile // row_chunk):
            rows1 = slice(rc * row_chunk, (rc + 1) * row_chunk)
            rows2 = slice(tile + rc * row_chunk, tile + (rc + 1) * row_chunk)
            c1 = normaliser(rows1)
            c2 = lam * normaliser(rows2)
            for csl in col_chunks:
                a_scr[buf, rows1, csl] = (s_scr[buf, rows1, csl] * c1 - s_scr[buf, rows2, csl] * c2).astype(BF16)
        o_ref[qoff:qoff + tile, :] = finish(a_scr[buf, :, 0:width], width)

    score_tile(0)
    for it in range(n_tiles):
        if it + 1 < n_tiles:
            score_tile(it + 1)
        softmax_tile(it)


def _diff_attention(q, k, v, lam_params, g_sub, batch, seq, n_meta):
    m, n = q.shape
    dv = g_sub.shape[1]
    tile = ATTN_TILE
    n_tiles = (seq - n_meta) // tile
    assert n_meta + n_tiles * tile == seq and n == DIFF_HEADS * dv
    assert n_meta <= LANES and (LANES - n_meta) % BF16_ROWS == 0
    padded = LANES + n_tiles * tile
    kern = functools.partial(_diff_attn_kernel, n_meta=n_meta, tile=tile, n_tiles=n_tiles)
    head_spec = pl.BlockSpec((seq, dv), lambda b, hh: (b, hh))
    return pl.pallas_call(
        kern,
        grid=(batch, DIFF_HEADS),
        in_specs=[
            pl.BlockSpec(lam_params.shape, lambda b, hh: (0, 0)),
            head_spec, head_spec, head_spec,
            pl.BlockSpec((1, dv), lambda b, hh: (0, 0)),
        ],
        out_specs=head_spec,
        out_shape=jax.ShapeDtypeStruct((m, n), BF16),
        scratch_shapes=[pltpu.VMEM((padded, dv), BF16), pltpu.VMEM((padded, dv), BF16),
                        pltpu.VMEM((2, 2 * tile, padded), F32), pltpu.VMEM((2, tile, padded), BF16)],
        compiler_params=_params("arbitrary", "arbitrary"),
        name="diff_attn",
    )(lam_params, q, k, v, g_sub)


def kernel(x_prompt, x_sample, state_gla, cache_k, cache_v, page_table, meta_tokens,
           gla_w_in, gla_w_gate, gla_b_gate, gla_norm, gla_w_out,
           diff_w_in, diff_lambda, diff_norm, diff_w_out,
           mlp_w1, mlp_w2, ln_mix_g, ln_mix_b, ln_mlp_g, ln_mlp_b):
    batch, seq_new, d = x_prompt.shape
    n_meta = meta_tokens.shape[0]
    seq = n_meta + seq_new
    bs = x_sample.shape[0]
    assert x_sample.shape[1] == 1 and bs % 2 == 0

    meta = jnp.broadcast_to(meta_tokens[None].astype(x_prompt.dtype), (batch, n_meta, d))
    x_p = jnp.concatenate([meta, x_prompt], axis=1).reshape(batch * seq, d)
    x_s = x_sample.reshape(bs, d)

    def row(vec):
        return vec.reshape(1, -1).astype(F32)

    w1_all, w2_all = mlp_w1.astype(BF16), mlp_w2.astype(BF16)

    def finish_layer(layer, o, x, w_out, decode=None):
        return _finish_layer(o, w_out, x, (row(ln_mix_g[layer]), row(ln_mix_b[layer])), w1_all, w2_all, layer,
                             (row(ln_mlp_g[layer]), row(ln_mlp_b[layer])), decode)

    rank = gla_w_gate.shape[1]
    n_main = gla_w_in.shape[2] - rank
    w_in0 = gla_w_in[0].astype(BF16)
    w_glow = jnp.pad(gla_w_in[0, :, n_main:], ((0, 0), (0, LANES - rank))).astype(BF16)
    w_gate = jnp.pad(gla_w_gate[0], ((0, LANES - rank), (0, 0))).astype(BF16)
    b_gate = row(gla_b_gate[0])
    g_norm = row(gla_norm[0])
    w_out0 = gla_w_out[0].astype(BF16)
    w_in1 = diff_w_in[0].astype(BF16)
    w_out1 = diff_w_out[0].astype(BF16)
    g_sub = row(diff_norm[0])
    lam_params = diff_lambda[0].astype(F32)
    dv = g_sub.shape[1]
    past = page_table.shape[1] * PAGE_SIZE

    dv_gla = g_norm.shape[1]
    qk_s, vr_s, lg_s = _gla_project(x_s, w_in0, w_glow, w_gate, b_gate, dv_gla)
    o_s, st_s = _gla_step(qk_s, vr_s.astype(F32), lg_s, g_norm, state_gla[0].astype(F32))
    x_s = finish_layer(0, o_s, x_s, w_out0)
    q_s, k_s, _, v_s, _ = _diff_project(x_s, w_in1, jnp.full((bs,), past))
    heads = (bs, DIFF_HEADS, dv)
    q_s3, k_s3, v_s3 = q_s.astype(F32).reshape(heads), k_s.reshape(heads), v_s.reshape(heads)

    def hosted(lo):
        hi = lo + bs // 2
        return (q_s3[lo:hi], k_s3[lo:hi], v_s3[lo:hi], cache_k[0], cache_v[0], page_table[lo:hi],
                lam_params, g_sub)

    qk_p, vr_p, lg_p = _gla_project(x_p, w_in0, w_glow, w_gate, b_gate, dv_gla)
    o_p, st_p = _gla_core(qk_p, vr_p, lg_p, g_norm, batch, seq, n_meta)
    x_p, a_s_lo = finish_layer(0, o_p, x_p, w_out0, hosted(0))

    q_p, k_p, kb_p, v_p, vb_p = _diff_project(x_p, w_in1, jnp.arange(seq))
    a_p = _diff_attention(q_p, kb_p, vb_p, lam_params, g_sub, batch, seq, n_meta)
    x_p, a_s_hi = finish_layer(1, a_p, x_p, w_out1, hosted(bs // 2))

    a_s = jnp.concatenate([a_s_lo, a_s_hi], axis=0).reshape(bs, DIFF_HEADS * dv).astype(BF16)
    x_s = finish_layer(1, a_s, x_s, w_out1)

    y_prompt = x_p.reshape(batch, seq, d)[:, n_meta:]
    y_sample = x_s.reshape(bs, 1, d)
    return (y_prompt, y_sample, st_p[None], st_s[None],
            k_p.reshape(1, batch, seq, DIFF_HEADS, dv), v_p.reshape(1, batch, seq, DIFF_HEADS, dv),
            k_s.reshape(1, bs, 1, DIFF_HEADS, dv), v_s.reshape(1, bs, 1, DIFF_HEADS, dv))
```

```python
import functools
import math

import jax
import jax.numpy as jnp
from jax import lax
from jax.experimental import pallas as pl
from jax.experimental.pallas import tpu as pltpu

F32 = jnp.float32
BF16 = jnp.bfloat16

DEPTH = 2
GLA_HEADS = 4
GLA_TAU = 16.0
DIFF_HEADS = 8
ROPE_THETA = 500000.0
PAGE_SIZE = 128
LN_EPS = 1e-5
DEEPNORM_ALPHA = (2 * DEPTH) ** 0.25
LAM_INIT = 0.8 - 0.6 * math.exp(-0.3 * 1)

LANES = 128
SUBLANES = 8
BF16_ROWS = 16
VMEM_LIMIT = 48 * 1024 * 1024

GLA_CHUNK = 128
GLA_HEADS_PER_STEP = 2
GLA_STEP_SEQS = 8
GLA_SAFE_LOG_SPAN = 80.0
ATTN_TILE = 256
MLP_FF_TILE = 512
HOST_TILES_PER_SEQ = 1
VMEM_LIMIT_HOST = 60000 * 1024
MASK_VALUE = -1e30
LOG2_E = 1.4426950408889634

_NT = (((1,), (1,)), ((), ()))
_TN = (((0,), (0,)), ((), ()))


def _row_tile(m):
    for cand in (688, 512, 256, 128, 64, 32, 16, 8):
        if m % cand == 0:
            return cand
    raise ValueError(f"row count {m} is not a multiple of 8")


def _col_tile(n):
    for cand in (512, 256, 128):
        if n % cand == 0:
            return cand
    raise ValueError(f"column count {n} is not a multiple of 128")


def _params(*sem, limit=VMEM_LIMIT):
    return pltpu.CompilerParams(dimension_semantics=sem, vmem_limit_bytes=limit)


def _layer_norm(y, g, b):
    mu = jnp.mean(y, axis=-1, keepdims=True)
    yc = y - mu
    var = jnp.mean(yc * yc, axis=-1, keepdims=True)
    return yc * lax.rsqrt(var + LN_EPS) * g + b


def _rms_norm(y, g):
    return y * lax.rsqrt(jnp.mean(y * y, axis=-1, keepdims=True) + LN_EPS) * g


def _lane_col(row, n):
    return jnp.transpose(jnp.broadcast_to(row, (n, n)))


def _gla_proj_kernel(x_ref, w_ref, wg_ref, wgate_ref, bgate_ref, qk_ref, vr_ref, lg_ref, *, tn):
    xb = x_ref[...].astype(BF16)
    g_low = jnp.dot(xb, wg_ref[...], preferred_element_type=F32)
    z = jnp.dot(g_low.astype(BF16), wgate_ref[...], preferred_element_type=F32) + bgate_ref[...]
    log_sig = jnp.minimum(z, 0.0) - jnp.log(1.0 + jnp.exp(-jnp.abs(z)))
    lg_ref[...] = log_sig * (1.0 / GLA_TAU)
    n_qk = qk_ref.shape[1]
    for j in range((n_qk + vr_ref.shape[1]) // tn):
        y = jnp.dot(xb, w_ref[:, j * tn:(j + 1) * tn], preferred_element_type=F32)
        if j * tn < n_qk:
            qk_ref[:, j * tn:(j + 1) * tn] = y
        else:
            vr_ref[:, j * tn - n_qk:(j + 1) * tn - n_qk] = y.astype(vr_ref.dtype)


def _gla_project(x, w_in, w_glow, w_gate, b_gate, dv):
    m, d = x.shape
    ng = w_gate.shape[1]
    n_qk, n_vr = 2 * ng, 2 * GLA_HEADS * dv
    assert w_in.shape[1] >= n_qk + n_vr
    tm = _row_tile(m)
    return pl.pallas_call(
        functools.partial(_gla_proj_kernel, tn=_col_tile(n_qk)),
        grid=(m // tm,),
        in_specs=[
            pl.BlockSpec((tm, d), lambda i: (i, 0)),
            pl.BlockSpec((d, n_qk + n_vr), lambda i: (0, 0)),
            pl.BlockSpec((d, LANES), lambda i: (0, 0)),
            pl.BlockSpec((LANES, ng), lambda i: (0, 0)),
            pl.BlockSpec((1, ng), lambda i: (0, 0)),
        ],
        out_specs=[
            pl.BlockSpec((tm, n_qk), lambda i: (i, 0)),
            pl.BlockSpec((tm, n_vr), lambda i: (i, 0)),
            pl.BlockSpec((tm, ng), lambda i: (i, 0)),
        ],
        out_shape=[jax.ShapeDtypeStruct((m, n_qk), F32), jax.ShapeDtypeStruct((m, n_vr), BF16),
                   jax.ShapeDtypeStruct((m, ng), F32)],
        compiler_params=_params("arbitrary"),
        name="gla_proj",
    )(x, w_in, w_glow, w_gate, b_gate)


def _cast_specs(w1, w2, layer, n_steps, step_of):
    _, d, ff = w1.shape
    assert d % n_steps == 0 and ff % n_steps == 0 and (d // n_steps) % BF16_ROWS == 0
    slabs = [(d // n_steps, ff), (ff // n_steps, d)]
    in_specs = [pl.BlockSpec((None,) + s, lambda *idx: (layer, step_of(*idx), 0)) for s in slabs]
    out_specs = [pl.BlockSpec(s, lambda *idx: (step_of(*idx), 0)) for s in slabs]
    return in_specs, out_specs, [jax.ShapeDtypeStruct((d, ff), BF16), jax.ShapeDtypeStruct((ff, d), BF16)]


def _gla_core_kernel(q_ref, k_ref, v_ref, r_ref, lg_ref, gn_ref, w1_ref, w2_ref,
                     o_ref, st_ref, w1b_ref, w2b_ref, s_scr, *, n_meta, chunk, n_chunks, dk, dv):
    w1b_ref[...] = w1_ref[...].astype(BF16)
    w2b_ref[...] = w2_ref[...].astype(BF16)
    wk = q_ref.shape[1]
    heads = range(wk // dk)
    scale = dk ** -0.5
    s_scr[...] = jnp.zeros_like(s_scr)

    def do_chunk(off, length):
        rows = pl.ds(off, length)
        ii = lax.broadcasted_iota(jnp.int32, (length, length), 0)
        jj = lax.broadcasted_iota(jnp.int32, (length, length), 1)
        causal = ii >= jj
        lg = lg_ref[rows, :]
        hi = lg.astype(BF16)
        rest = lg - hi.astype(F32)
        mid = rest.astype(BF16)
        lo = (rest - mid.astype(F32)).astype(BF16)
        parts = jnp.dot(causal.astype(BF16), jnp.concatenate([hi, mid, lo], axis=1),
                        preferred_element_type=F32)
        b = parts[:, 0:wk] + parts[:, wk:2 * wk] + parts[:, 2 * wk:3 * wk]
        q = q_ref[rows, :] * scale
        k = k_ref[rows, :]
        b_mid = b[length // 2:length // 2 + 1, :]
        b_last = b[length - 1:length, :]
        q_in = (q * jnp.exp(b)).astype(BF16)
        qs = (q * jnp.exp(b - b_mid)).astype(BF16)
        ks = (k * jnp.exp(b_mid - b)).astype(BF16)
        kd = (k * jnp.exp(b_last - b)).astype(BF16)
        decay = jnp.exp(b_last)
        vb = v_ref[rows, :]

        def kcols(hh):
            return slice(hh * dk, (hh + 1) * dk)

        def vcols(hh):
            return slice(hh * dv, (hh + 1) * dv)

        states = [s_scr[hh] for hh in heads]
        o_inter = [lax.dot_general(q_in[:, kcols(hh)], states[hh].astype(BF16), _NT,
                                   preferred_element_type=F32) for hh in heads]
        scores = [lax.dot_general(qs[:, kcols(hh)], ks[:, kcols(hh)], _NT, preferred_element_type=F32)
                  for hh in heads]
        scores = [jnp.where(causal, s, 0.0).astype(BF16) for s in scores]
        o_intra = [jnp.dot(scores[hh], vb[:, vcols(hh)], preferred_element_type=F32) for hh in heads]
        grown = [lax.dot_general(vb[:, vcols(hh)], kd[:, kcols(hh)], _TN, preferred_element_type=F32)
                 for hh in heads]
        for hh in heads:
            s_scr[hh] = states[hh] * decay[:, kcols(hh)] + grown[hh]
            r = r_ref[rows, vcols(hh)].astype(F32)
            gated = _rms_norm(o_inter[hh] + o_intra[hh], gn_ref[...]) * (r / (1.0 + jnp.exp(-r)))
            o_ref[rows, vcols(hh)] = gated.astype(o_ref.dtype)

    def token_by_token():
        group = BF16_ROWS
        seq = q_ref.shape[0]

        def as_rows(row):
            return jnp.concatenate([row, jnp.zeros((SUBLANES - 1, row.shape[1]), row.dtype)], axis=0)

        def body(g, carry):
            rows = pl.ds(pl.multiple_of(g * group, group), group)
            a_g = jnp.exp(lg_ref[rows, :])
            q_g = q_ref[rows, :] * scale
            k_g = k_ref[rows, :]
            v_g = v_ref[rows, :].astype(F32)
            r_g = r_ref[rows, :].astype(F32)
            outs = [[] for _ in heads]
            for t in range(group):
                a, q, k, v, r = (x[t:t + 1] for x in (a_g, q_g, k_g, v_g, r_g))
                for hh in heads:
                    kc, vc = slice(hh * dk, (hh + 1) * dk), slice(hh * dv, (hh + 1) * dv)
                    grown = lax.dot_general(as_rows(v[:, vc]).astype(BF16), as_rows(k[:, kc]).astype(BF16),
                                            _TN, preferred_element_type=F32)
                    s_new = s_scr[hh] * a[:, kc] + grown
                    s_scr[hh] = s_new
                    o = lax.dot_general(as_rows(q[:, kc]).astype(BF16), s_new.astype(BF16), _NT,
                                        preferred_element_type=F32)[0:1]
                    rr = r[:, vc]
                    outs[hh].append(_rms_norm(o, gn_ref[...]) * (rr / (1.0 + jnp.exp(-rr))))
            for hh in heads:
                o_ref[rows, hh * dv:(hh + 1) * dv] = jnp.concatenate(
                    outs[hh], axis=0).astype(o_ref.dtype)
            return carry

        lax.fori_loop(0, seq // group, body, 0)

    half_span = max(chunk, n_meta) // 2
    chunked_is_safe = jnp.min(lg_ref[...]) * half_span > -GLA_SAFE_LOG_SPAN

    @pl.when(chunked_is_safe)
    def _():
        do_chunk(0, n_meta)
        for c in range(n_chunks):
            do_chunk(n_meta + c * chunk, chunk)

    pl.when(jnp.logical_not(chunked_is_safe))(token_by_token)

    for hh in heads:
        st_ref[0, hh] = jnp.transpose(s_scr[hh])


def _gla_core(qk, vr, lg, g_norm, batch, seq, n_meta, mlp_w1, mlp_w2, layer):
    m = qk.shape[0]
    hk = lg.shape[1]
    dk = hk // GLA_HEADS
    dv = g_norm.shape[1]
    hv = GLA_HEADS * dv
    assert qk.shape[1] == 2 * hk and vr.shape[1] == 2 * hv and m == batch * seq
    chunk = GLA_CHUNK
    n_chunks = (seq - n_meta) // chunk
    assert n_meta + n_chunks * chunk == seq
    hps = GLA_HEADS_PER_STEP
    groups = GLA_HEADS // hps
    wk, wv = hps * dk, hps * dv
    kern = functools.partial(_gla_core_kernel, n_meta=n_meta, chunk=chunk, n_chunks=n_chunks, dk=dk, dv=dv)
    cast_in, cast_out, cast_shapes = _cast_specs(mlp_w1, mlp_w2, layer, batch * groups,
                                                 lambda b, g: b * groups + g)
    return pl.pallas_call(
        kern,
        grid=(batch, groups),
        in_specs=[
            pl.BlockSpec((seq, wk), lambda b, g: (b, g)),
            pl.BlockSpec((seq, wk), lambda b, g: (b, groups + g)),
            pl.BlockSpec((seq, wv), lambda b, g: (b, g)),
            pl.BlockSpec((seq, wv), lambda b, g: (b, groups + g)),
            pl.BlockSpec((seq, wk), lambda b, g: (b, g)),
            pl.BlockSpec((1, dv), lambda b, g: (0, 0)),
        ] + cast_in,
        out_specs=[
            pl.BlockSpec((seq, wv), lambda b, g: (b, g)),
            pl.BlockSpec((1, hps, dk, dv), lambda b, g: (b, g, 0, 0)),
        ] + cast_out,
        out_shape=[jax.ShapeDtypeStruct((m, hv), BF16),
                   jax.ShapeDtypeStruct((batch, GLA_HEADS, dk, dv), F32)] + cast_shapes,
        scratch_shapes=[pltpu.VMEM((hps, dv, dk), F32)],
        compiler_params=_params("arbitrary", "arbitrary"),
        name="gla_core",
    )(qk, qk, vr, vr, lg, g_norm, mlp_w1, mlp_w2)


def _gla_step_kernel(qk_ref, vr_ref, lg_ref, gn_ref, s_ref, o_ref, sn_ref):
    dk, dv = s_ref.shape[2], s_ref.shape[3]
    hk, hv = GLA_HEADS * dk, GLA_HEADS * dv
    scale = dk ** -0.5
    reps = dv // dk
    for sq in range(s_ref.shape[0]):
        for hh in range(GLA_HEADS):
            q = qk_ref[sq, :, hh * dk:(hh + 1) * dk] * scale
            k = qk_ref[sq, :, hk + hh * dk:hk + (hh + 1) * dk]
            v = vr_ref[sq, :, hh * dv:(hh + 1) * dv]
            r = vr_ref[sq, :, hv + hh * dv:hv + (hh + 1) * dv]
            a = jnp.exp(lg_ref[sq, :, hh * dk:(hh + 1) * dk])
            a_c = jnp.tile(_lane_col(a, dk), (1, reps))
            k_c = jnp.tile(_lane_col(k, dk), (1, reps))
            q_c = jnp.tile(_lane_col(q, dk), (1, reps))
            s_new = s_ref[sq, hh] * a_c + k_c * v
            sn_ref[sq, hh] = s_new
            o = jnp.sum(q_c * s_new, axis=0, keepdims=True)
            gated = _rms_norm(o, gn_ref[...]) * (r / (1.0 + jnp.exp(-r)))
            o_ref[sq, :, hh * dv:(hh + 1) * dv] = gated.astype(o_ref.dtype)


def _gla_step(qk, vr, lg, g_norm, state):
    bs = qk.shape[0]
    _, _, dk, dv = state.shape
    hv = GLA_HEADS * dv
    nb = math.gcd(bs, GLA_STEP_SEQS)
    o, s_new = pl.pallas_call(
        _gla_step_kernel,
        grid=(bs // nb,),
        in_specs=[
            pl.BlockSpec((nb, 1, qk.shape[1]), lambda b: (b, 0, 0)),
            pl.BlockSpec((nb, 1, vr.shape[1]), lambda b: (b, 0, 0)),
            pl.BlockSpec((nb, 1, lg.shape[1]), lambda b: (b, 0, 0)),
            pl.BlockSpec((1, dv), lambda b: (0, 0)),
            pl.BlockSpec((nb, GLA_HEADS, dk, dv), lambda b: (b, 0, 0, 0)),
        ],
        out_specs=[
            pl.BlockSpec((nb, 1, hv), lambda b: (b, 0, 0)),
            pl.BlockSpec((nb, GLA_HEADS, dk, dv), lambda b: (b, 0, 0, 0)),
        ],
        out_shape=[jax.ShapeDtypeStruct((bs, 1, hv), BF16), jax.ShapeDtypeStruct(state.shape, F32)],
        compiler_params=_params("arbitrary"),
        name="gla_step",
    )(qk.reshape(bs, 1, -1), vr.reshape(bs, 1, -1), lg.reshape(bs, 1, -1), g_norm, state)
    return o.reshape(bs, hv), s_new


def _lambda_value(lp):
    e1 = jnp.exp(jnp.sum(lp[0:1] * lp[1:2], axis=1, keepdims=True))
    e2 = jnp.exp(jnp.sum(lp[2:3] * lp[3:4], axis=1, keepdims=True))
    return e1 - e2 + LAM_INIT


class _Decode:
    def __init__(self, q_ref, m_scr, l_scr, acc_scr):
        self.m_scr, self.l_scr, self.acc_scr = m_scr, l_scr, acc_scr
        _, self.n_head, self.dv = q_ref.shape
        qt = q_ref[0]
        first_half = lax.broadcasted_iota(jnp.int32, qt.shape, 1) < self.dv // 2
        self.q_rows = jnp.concatenate([jnp.where(first_half, qt, 0.0), jnp.where(first_half, 0.0, qt)],
                                      axis=0).astype(BF16)

    def init(self):
        self.m_scr[...] = jnp.full(self.m_scr.shape, MASK_VALUE, F32)
        self.l_scr[...] = jnp.zeros_like(self.l_scr)
        self.acc_scr[...] = jnp.zeros_like(self.acc_scr)

    def own_head(self, n, n_valid):
        shape = (2 * self.n_head, n)
        col = lax.broadcasted_iota(jnp.int32, shape, 1)
        row = lax.broadcasted_iota(jnp.int32, shape, 0)
        return jnp.logical_and(col % self.n_head == row % self.n_head, col < n_valid)

    def scores(self, ks, own):
        return [jnp.where(own, lax.dot_general(self.q_rows, k2.astype(BF16), _NT, preferred_element_type=F32),
                          MASK_VALUE) for k2 in ks]

    def accumulate(self, ss, vs):
        m_old = self.m_scr[...]
        m_new = jnp.maximum(m_old, jnp.max(functools.reduce(jnp.maximum, ss), axis=1, keepdims=True))
        alpha = jnp.exp2(m_old - m_new)
        ps = [jnp.exp2(s - m_new) for s in ss]
        self.m_scr[...] = m_new
        self.l_scr[...] = alpha * self.l_scr[...] + jnp.sum(functools.reduce(jnp.add, ps), axis=1, keepdims=True)
        pv = [jnp.dot(p.astype(BF16), v2.astype(BF16), preferred_element_type=F32) for p, v2 in zip(ps, vs)]
        self.acc_scr[...] = alpha * self.acc_scr[...] + functools.reduce(jnp.add, pv)

    def page_scores(self, k_refs):
        n = k_refs[0].shape[1] * self.n_head
        return self.scores([r[0].reshape(n, self.dv) for r in k_refs], self.own_head(n, n))

    def accumulate_pages(self, ss, v_refs):
        n = v_refs[0].shape[1] * self.n_head
        self.accumulate(ss, [r[0].reshape(n, self.dv) for r in v_refs])

    def finish(self, kc_ref, vc_ref, lam_ref, g_ref, o_ref):
        fill = jnp.zeros((LANES - self.n_head, self.dv), F32)
        ss = self.scores([jnp.concatenate([kc_ref[0], fill], axis=0)], self.own_head(LANES, self.n_head))
        self.accumulate(ss, [jnp.concatenate([vc_ref[0], fill], axis=0)])
        ratio = self.acc_scr[...] / self.l_scr[...]
        o = ratio[:self.n_head] - _lambda_value(lam_ref[...]) * ratio[self.n_head:]
        o_ref[0] = _rms_norm(o, g_ref[...]) * (1.0 - LAM_INIT)


def _kept_row_runs(tile, tm, m, seq, n_meta):
    lo, hi = tile * tm, min((tile + 1) * tm, m)
    runs, r = [], lo
    while r < hi:
        b, t = divmod(r, seq)
        if t < n_meta:
            r = min(b * seq + n_meta, hi)
            continue
        end = min((b + 1) * seq, hi)
        runs.append((r - lo, end - r, b * (seq - n_meta) + t - n_meta))
        r = end
    return tuple(runs)


def _finish_kernel(*refs, n_pages_step, tiles_per_seq, kept_runs):
    hosting = n_pages_step > 0
    if hosting:
        refs = refs[1:]
    a_ref, wo_ref, x_ref, g1_ref, b1_ref, w1_ref, w2_ref, g2_ref, b2_ref = refs[:9]
    refs = refs[9:]
    if hosting:
        lam_ref, q_ref, kc_ref, vc_ref, gs_ref = refs[:5]
        k_refs = refs[5:5 + n_pages_step]
        v_refs = refs[5 + n_pages_step:5 + 2 * n_pages_step]
        o_ref, od_ref, xb_scr, acc_scr, m_scr, l_scr, dacc_scr, *refs = refs[5 + 2 * n_pages_step:]
        dec = _Decode(q_ref, m_scr, l_scr, dacc_scr)
    else:
        o_ref, xb_scr, acc_scr, *refs = refs
    if kept_runs is None:
        x1_ref = o_ref
    else:
        x1_ref, res_scr, out_sem = refs

        def out_copies(tile):
            return [pltpu.make_async_copy(res_scr.at[pl.ds(src, n), :], o_ref.at[pl.ds(dst, n), :], out_sem)
                    for src, n, dst in kept_runs[tile]]
    f = pl.program_id(1)
    last_f = pl.num_programs(1) - 1
    tile_in_seq = pl.program_id(0) % tiles_per_seq

    @pl.when(f == 0)
    def _():
        proj = jnp.dot(a_ref[...], wo_ref[...], preferred_element_type=F32)
        x1 = _layer_norm(DEEPNORM_ALPHA * x_ref[...] + proj, g1_ref[...], b1_ref[...])
        x1_ref[...] = x1
        xb_scr[...] = x1.astype(BF16)
        acc_scr[...] = jnp.zeros_like(acc_scr)

    if hosting:
        pl.when(jnp.logical_and(f == 0, tile_in_seq == 0))(dec.init)

    if hosting:
        page_scores = dec.page_scores(k_refs)
    hid = jnp.maximum(jnp.dot(xb_scr[...], w1_ref[...], preferred_element_type=F32), 0.0)
    if hosting:
        dec.accumulate_pages(page_scores, v_refs)
    acc_scr[...] += jnp.dot((hid * hid).astype(BF16), w2_ref[...], preferred_element_type=F32)

    @pl.when(f == last_f)
    def _():
        result = _layer_norm(DEEPNORM_ALPHA * x1_ref[...] + acc_scr[...], g2_ref[...], b2_ref[...])
        if kept_runs is None:
            o_ref[...] = result
        else:
            tile = pl.program_id(0)
            n_tiles = len(kept_runs)
            for t in range(1, n_tiles):
                @pl.when(tile == t)
                def _(t=t):
                    for copy in out_copies(t - 1):
                        copy.wait()
            res_scr[...] = result
            for t in range(n_tiles):
                @pl.when(tile == t)
                def _(t=t):
                    for copy in out_copies(t):
                        copy.start()
                    if t == n_tiles - 1:
                        for copy in out_copies(t):
                            copy.wait()

    if hosting:
        @pl.when(jnp.logical_and(f == last_f, tile_in_seq == tiles_per_seq - 1))
        def _():
            dec.finish(kc_ref, vc_ref, lam_ref, gs_ref, od_ref)


def _finish_layer(a, w_out, x, ln1, w1, w2, ln2, decode=None, drop_meta=None):
    m, kk = a.shape
    d = w_out.shape[1]
    ff = w1.shape[1]
    tf = min(MLP_FF_TILE, ff)
    n_f = ff // tf
    tps = HOST_TILES_PER_SEQ
    if decode is None:
        tm, pps = _row_tile(m), 0
        n_tiles = m // tm
    else:
        q, k_cur, v_cur, cache_k, cache_v, page_table, lam_params, g_sub = decode
        n_seqs, n_head, dv = q.shape
        n_tiles = n_seqs * tps
        tm = pl.cdiv(pl.cdiv(m, n_tiles), BF16_ROWS) * BF16_ROWS
        n_pages, page = page_table.shape[1], cache_k.shape[1]
        assert n_pages % (tps * n_f) == 0 and pl.cdiv(m, tm) == n_tiles
        pps = n_pages // (tps * n_f)
    row_spec = pl.BlockSpec((tm, d), lambda i, f, *_: (i, 0))
    vec_spec = pl.BlockSpec((1, d), lambda i, f, *_: (0, 0))
    in_specs = [
        pl.BlockSpec((tm, kk), lambda i, f, *_: (i, 0)),
        pl.BlockSpec((kk, d), lambda i, f, *_: (0, 0)),
        row_spec, vec_spec, vec_spec,
        pl.BlockSpec((d, tf), lambda i, f, *_: (0, f)),
        pl.BlockSpec((tf, d), lambda i, f, *_: (f, 0)),
        vec_spec, vec_spec,
    ]
    args = [a, w_out, x, *ln1, w1, w2, *ln2]
    out_specs = [row_spec]
    out_shape = [jax.ShapeDtypeStruct((m, d), F32)]
    scratch = [pltpu.VMEM((tm, d), BF16), pltpu.VMEM((tm, d), F32)]
    prefetch = []
    if decode is not None:
        head_spec = pl.BlockSpec((1, n_head, dv), lambda i, f, pt: (i // tps, 0, 0))

        def page_spec(s):
            return pl.BlockSpec((1, page, n_head, dv),
                                lambda i, f, pt: (pt[i // tps, ((i % tps) * n_f + f) * pps + s], 0, 0, 0))

        in_specs += [pl.BlockSpec(lam_params.shape, lambda i, f, pt: (0, 0)), head_spec, head_spec, head_spec,
                     pl.BlockSpec((1, dv), lambda i, f, pt: (0, 0))]
        in_specs += [page_spec(s) for s in range(pps)] * 2
        args += [lam_params, q, k_cur, v_cur, g_sub] + [cache_k] * pps + [cache_v] * pps
        out_specs.append(head_spec)
        out_shape.append(jax.ShapeDtypeStruct((n_seqs, n_head, dv), F32))
        scratch += [pltpu.VMEM((2 * n_head, 1), F32), pltpu.VMEM((2 * n_head, 1), F32),
                    pltpu.VMEM((2 * n_head, dv), F32)]
        prefetch = [page_table]
    kept_runs = None
    if drop_meta is not None:
        seq, n_meta = drop_meta
        kept_runs = tuple(_kept_row_runs(t, tm, m, seq, n_meta) for t in range(n_tiles))
        out_specs[0] = pl.BlockSpec(memory_space=pl.ANY)
        out_shape[0] = jax.ShapeDtypeStruct((m // seq * (seq - n_meta), d), F32)
        scratch += [pltpu.VMEM((tm, d), F32), pltpu.VMEM((tm, d), F32), pltpu.SemaphoreType.DMA(())]
    outs = pl.pallas_call(
        functools.partial(_finish_kernel, n_pages_step=pps, tiles_per_seq=tps, kept_runs=kept_runs),
        grid_spec=pltpu.PrefetchScalarGridSpec(
            num_scalar_prefetch=len(prefetch), grid=(n_tiles, n_f),
            in_specs=in_specs, out_specs=out_specs, scratch_shapes=scratch),
        out_shape=out_shape,
        compiler_params=_params("arbitrary", "arbitrary",
                                limit=VMEM_LIMIT if decode is None else VMEM_LIMIT_HOST),
        name="finish_layer" if decode is None else "finish_layer_host",
    )(*prefetch, *args)
    return outs[0] if decode is None else tuple(outs)


def _diff_proj_kernel(x_ref, w_ref, cos_ref, sin_up_ref, sin_dn_ref,
                      q_ref, k_ref, kb_ref, v_ref, vb_ref, *, tn, q_scale, rot_half):
    xb = x_ref[...].astype(BF16)
    nq = q_ref.shape[1]

    def project(seg, j):
        lo = seg * nq + j * tn
        return jnp.dot(xb, w_ref[:, lo:lo + tn], preferred_element_type=F32)

    def rope(t):
        groups = []
        for g in range(tn // LANES):
            tg = t[:, g * LANES:(g + 1) * LANES]
            groups.append(tg * cos_ref[...] + pltpu.roll(tg, LANES - rot_half, 1) * sin_up_ref[...]
                          + pltpu.roll(tg, rot_half, 1) * sin_dn_ref[...])
        return jnp.concatenate(groups, axis=1)

    for j in range(nq // tn):
        cols = slice(j * tn, (j + 1) * tn)
        q_ref[:, cols] = (rope(project(0, j)) * q_scale).astype(q_ref.dtype)
        kr = rope(project(1, j))
        k_ref[:, cols] = kr
        kb_ref[:, cols] = kr.astype(kb_ref.dtype)
        y = project(2, j)
        v_ref[:, cols] = y
        vb_ref[:, cols] = y.astype(vb_ref.dtype)


def _rope_tables(pos, dh):
    rot = dh // 4
    inv = ROPE_THETA ** (-jnp.arange(0, rot, 2, dtype=F32) / rot)
    ang = pos.astype(F32)[:, None] * inv[None, :]
    n = pos.shape[0]

    def lanes(first, second, fill):
        sub = jnp.concatenate([first, second, jnp.full((n, dh - rot), fill, F32)], axis=1)
        return jnp.tile(sub, (1, LANES // dh))

    zeros = jnp.zeros_like(ang)
    return (lanes(jnp.cos(ang), jnp.cos(ang), 1.0),
            lanes(-jnp.sin(ang), zeros, 0.0),
            lanes(zeros, jnp.sin(ang), 0.0))


def _diff_project(x, w, pos):
    m, d = x.shape
    n = w.shape[1]
    nq = n // 3
    dh = nq // (2 * DIFF_HEADS)
    assert m % pos.shape[0] == 0
    tm = _row_tile(pos.shape[0])
    pos_tiles = pos.shape[0] // tm
    cos_t, sin_up, sin_dn = _rope_tables(pos, dh)
    kern = functools.partial(_diff_proj_kernel, tn=_col_tile(nq), q_scale=dh ** -0.5 * LOG2_E, rot_half=dh // 8)
    tab_spec = pl.BlockSpec((tm, LANES), lambda i: (i % pos_tiles, 0))
    out_spec = pl.BlockSpec((tm, nq), lambda i: (i, 0))
    return pl.pallas_call(
        kern,
        grid=(m // tm,),
        in_specs=[
            pl.BlockSpec((tm, d), lambda i: (i, 0)),
            pl.BlockSpec((d, n), lambda i: (0, 0)),
            tab_spec, tab_spec, tab_spec,
        ],
        out_specs=[out_spec] * 5,
        out_shape=[jax.ShapeDtypeStruct((m, nq), BF16),
                   jax.ShapeDtypeStruct((m, nq), F32), jax.ShapeDtypeStruct((m, nq), BF16),
                   jax.ShapeDtypeStruct((m, nq), F32), jax.ShapeDtypeStruct((m, nq), BF16)],
        compiler_params=_params("arbitrary"),
        name="diff_proj",
    )(x, w, cos_t, sin_up, sin_dn)


def _diff_attn_kernel(lam_ref, q_ref, k_ref, v_ref, g_ref, w1_ref, w2_ref, o_ref, w1b_ref, w2b_ref,
                      kp_scr, vp_scr, s_scr, a_scr, *, n_meta, tile, n_tiles):
    w1b_ref[...] = w1_ref[...].astype(BF16)
    w2b_ref[...] = w2_ref[...].astype(BF16)
    lam = _lambda_value(lam_ref[...])
    seq, dq = k_ref.shape
    first = LANES
    pad = first - n_meta
    kp_scr[0:pad, :] = jnp.zeros((pad, dq), BF16)
    kp_scr[pad:pad + seq, :] = k_ref[...]
    vp_scr[0:pad, :] = jnp.zeros((pad, v_ref.shape[1]), BF16)
    vp_scr[pad:pad + seq, :] = v_ref[...]
    first_half = lax.broadcasted_iota(jnp.int32, (1, dq), 1) < dq // 2

    def stack(q):
        zero = jnp.zeros_like(q)
        return jnp.concatenate([jnp.where(first_half, q, zero), jnp.where(first_half, zero, q)], axis=0)

    def block_coords(length, width):
        ri = lax.broadcasted_iota(jnp.int32, (2 * length, width), 0)
        ri = jnp.where(ri >= length, ri - length, ri)
        ci = lax.broadcasted_iota(jnp.int32, (2 * length, width), 1)
        return ci, ri

    def finish(weights, width):
        o = jnp.dot(weights, vp_scr[0:width, :], preferred_element_type=F32)
        return (_rms_norm(o, g_ref[...]) * (1.0 - LAM_INIT)).astype(o_ref.dtype)

    ci, ri = block_coords(n_meta, first)
    s = lax.dot_general(stack(q_ref[0:n_meta, :]), kp_scr[0:first, :], _NT, preferred_element_type=F32)
    s = jnp.where(jnp.logical_and(ci >= pad, ci - pad <= ri), s, MASK_VALUE)
    p = jnp.exp2(s - jnp.max(s, axis=1, keepdims=True))
    p = p / jnp.sum(p, axis=1, keepdims=True)
    o_ref[0:n_meta, :] = finish((p[:n_meta] - lam * p[n_meta:]).astype(BF16), first)

    ci, ri = block_coords(tile, tile)
    causal = ci <= ri
    not_pad = block_coords(tile, first)[0] >= pad
    row_chunk = 64

    def key_block(j):
        return (0, first) if j == 0 else (first + (j - 1) * tile, first + j * tile)

    def score_tile(it):
        qoff = n_meta + it * tile
        qs = stack(q_ref[qoff:qoff + tile, :])
        for j in range(it + 2):
            lo, hi = key_block(j)
            s = lax.dot_general(qs, kp_scr[lo:hi, :], _NT, preferred_element_type=F32)
            if j == 0:
                s = jnp.where(not_pad, s, MASK_VALUE)
            if j == it + 1:
                s = jnp.where(causal, s, MASK_VALUE)
            s_scr[it % 2, :, lo:hi] = s

    def softmax_tile(it):
        buf = it % 2
        qoff = n_meta + it * tile
        width = key_block(it + 1)[1]
        col_chunks = [slice(c * LANES, (c + 1) * LANES) for c in range(width // LANES)]

        def normaliser(rsl):
            mx = functools.reduce(jnp.maximum, [s_scr[buf, rsl, csl] for csl in col_chunks])
            m_b = jnp.broadcast_to(jnp.max(mx, axis=1, keepdims=True), (row_chunk, LANES))
            total = jnp.zeros((row_chunk, LANES), F32)
            for csl in col_chunks:
                p = jnp.exp2(s_scr[buf, rsl, csl] - m_b)
                s_scr[buf, rsl, csl] = p
                total = total + p
            return jnp.broadcast_to(1.0 / jnp.sum(total, axis=1, keepdims=True), (row_chunk, LANES))

        for rc in range(tile // row_chunk):
            rows1 = slice(rc * row_chunk, (rc + 1) * row_chunk)
            rows2 = slice(tile + rc * row_chunk, tile + (rc + 1) * row_chunk)
            c1 = normaliser(rows1)
            c2 = lam * normaliser(rows2)
            for csl in col_chunks:
                a_scr[buf, rows1, csl] = (s_scr[buf, rows1, csl] * c1 - s_scr[buf, rows2, csl] * c2).astype(BF16)
        o_ref[qoff:qoff + tile, :] = finish(a_scr[buf, :, 0:width], width)

    score_tile(0)
    for it in range(n_tiles):
        if it + 1 < n_tiles:
            score_tile(it + 1)
        softmax_tile(it)


def _diff_attention(q, k, v, lam_params, g_sub, batch, seq, n_meta, mlp_w1, mlp_w2, layer):
    m, n = q.shape
    dv = g_sub.shape[1]
    tile = ATTN_TILE
    n_tiles = (seq - n_meta) // tile
    assert n_meta + n_tiles * tile == seq and n == DIFF_HEADS * dv
    assert n_meta <= LANES and (LANES - n_meta) % BF16_ROWS == 0
    padded = LANES + n_tiles * tile
    kern = functools.partial(_diff_attn_kernel, n_meta=n_meta, tile=tile, n_tiles=n_tiles)
    head_spec = pl.BlockSpec((seq, dv), lambda b, hh: (b, hh))
    cast_in, cast_out, cast_shapes = _cast_specs(mlp_w1, mlp_w2, layer, batch * DIFF_HEADS,
                                                 lambda b, hh: b * DIFF_HEADS + hh)
    return pl.pallas_call(
        kern,
        grid=(batch, DIFF_HEADS),
        in_specs=[
            pl.BlockSpec(lam_params.shape, lambda b, hh: (0, 0)),
            head_spec, head_spec, head_spec,
            pl.BlockSpec((1, dv), lambda b, hh: (0, 0)),
        ] + cast_in,
        out_specs=[head_spec] + cast_out,
        out_shape=[jax.ShapeDtypeStruct((m, n), BF16)] + cast_shapes,
        scratch_shapes=[pltpu.VMEM((padded, dv), BF16), pltpu.VMEM((padded, dv), BF16),
                        pltpu.VMEM((2, 2 * tile, padded), F32), pltpu.VMEM((2, tile, padded), BF16)],
        compiler_params=_params("arbitrary", "arbitrary"),
        name="diff_attn",
    )(lam_params, q, k, v, g_sub, mlp_w1, mlp_w2)


def kernel(x_prompt, x_sample, state_gla, cache_k, cache_v, page_table, meta_tokens,
           gla_w_in, gla_w_gate, gla_b_gate, gla_norm, gla_w_out,
           diff_w_in, diff_lambda, diff_norm, diff_w_out,
           mlp_w1, mlp_w2, ln_mix_g, ln_mix_b, ln_mlp_g, ln_mlp_b):
    batch, seq_new, d = x_prompt.shape
    n_meta = meta_tokens.shape[0]
    seq = n_meta + seq_new
    bs = x_sample.shape[0]
    assert x_sample.shape[1] == 1 and bs % 2 == 0

    meta = jnp.broadcast_to(meta_tokens[None].astype(x_prompt.dtype), (batch, n_meta, d))
    x_p = jnp.concatenate([meta, x_prompt], axis=1).reshape(batch * seq, d)
    x_s = x_sample.reshape(bs, d)

    def row(vec):
        return vec.reshape(1, -1).astype(F32)

    def finish_layer(layer, o, x, w_out, w1, w2, decode=None, drop_meta=None):
        return _finish_layer(o, w_out, x, (row(ln_mix_g[layer]), row(ln_mix_b[layer])), w1, w2,
                             (row(ln_mlp_g[layer]), row(ln_mlp_b[layer])), decode, drop_meta)

    rank = gla_w_gate.shape[1]
    n_main = gla_w_in.shape[2] - rank
    w_in0 = gla_w_in[0].astype(BF16)
    w_glow = jnp.pad(gla_w_in[0, :, n_main:], ((0, 0), (0, LANES - rank))).astype(BF16)
    w_gate = jnp.pad(gla_w_gate[0], ((0, LANES - rank), (0, 0))).astype(BF16)
    b_gate = row(gla_b_gate[0])
    g_norm = row(gla_norm[0])
    w_out0 = gla_w_out[0].astype(BF16)
    w_in1 = diff_w_in[0].astype(BF16)
    w_out1 = diff_w_out[0].astype(BF16)
    g_sub = row(diff_norm[0])
    lam_params = diff_lambda[0].astype(F32)
    dv = g_sub.shape[1]
    dv_gla = g_norm.shape[1]
    past = page_table.shape[1] * PAGE_SIZE

    qk_p, vr_p, lg_p = _gla_project(x_p, w_in0, w_glow, w_gate, b_gate, dv_gla)
    o_p, st_p, w1_0, w2_0 = _gla_core(qk_p, vr_p, lg_p, g_norm, batch, seq, n_meta, mlp_w1, mlp_w2, 0)

    qk_s, vr_s, lg_s = _gla_project(x_s, w_in0, w_glow, w_gate, b_gate, dv_gla)
    o_s, st_s = _gla_step(qk_s, vr_s.astype(F32), lg_s, g_norm, state_gla[0].astype(F32))
    x_s = finish_layer(0, o_s, x_s, w_out0, w1_0, w2_0)
    q_s, k_s, _, v_s, _ = _diff_project(x_s, w_in1, jnp.full((bs,), past))
    heads = (bs, DIFF_HEADS, dv)
    q_s3, k_s3, v_s3 = q_s.astype(F32).reshape(heads), k_s.reshape(heads), v_s.reshape(heads)

    def hosted(lo):
        hi = lo + bs // 2
        return (q_s3[lo:hi], k_s3[lo:hi], v_s3[lo:hi], cache_k[0], cache_v[0], page_table[lo:hi],
                lam_params, g_sub)

    x_p, a_s_lo = finish_layer(0, o_p, x_p, w_out0, w1_0, w2_0, hosted(0))
    q_p, k_p, kb_p, v_p, vb_p = _diff_project(x_p, w_in1, jnp.arange(seq))
    a_p, w1_1, w2_1 = _diff_attention(q_p, kb_p, vb_p, lam_params, g_sub, batch, seq, n_meta, mlp_w1, mlp_w2, 1)
    y_p, a_s_hi = finish_layer(1, a_p, x_p, w_out1, w1_1, w2_1, hosted(bs // 2), drop_meta=(seq, n_meta))

    a_s = jnp.concatenate([a_s_lo, a_s_hi], axis=0).reshape(bs, DIFF_HEADS * dv).astype(BF16)
    x_s = finish_layer(1, a_s, x_s, w_out1, w1_1, w2_1)

    y_prompt = y_p.reshape(batch, seq_new, d)
    y_sample = x_s.reshape(bs, 1, d)
    return (y_prompt, y_sample, st_p[None], st_s[None],
            k_p.reshape(1, batch, seq, DIFF_HEADS, dv), v_p.reshape(1, batch, seq, DIFF_HEADS, dv),
            k_s.reshape(1, bs, 1, DIFF_HEADS, dv), v_s.reshape(1, bs, 1, DIFF_HEADS, dv))
```

```python
import functools
import math

import jax
import jax.numpy as jnp
from jax import lax
from jax.experimental import pallas as pl
from jax.experimental.pallas import tpu as pltpu

F32 = jnp.float32
BF16 = jnp.bfloat16

DEPTH = 2
GLA_HEADS = 4
GLA_TAU = 16.0
DIFF_HEADS = 8
ROPE_THETA = 500000.0
PAGE_SIZE = 128
LN_EPS = 1e-5
DEEPNORM_ALPHA = (2 * DEPTH) ** 0.25
LAM_INIT = 0.8 - 0.6 * math.exp(-0.3 * 1)

LANES = 128
SUBLANES = 8
BF16_ROWS = 16
VMEM_LIMIT = 48 * 1024 * 1024

GLA_CHUNK = 128
GLA_HEADS_PER_STEP = 2
GLA_STEP_SEQS = 8
GLA_SAFE_LOG_SPAN = 80.0
ATTN_TILE = 256
MLP_FF_TILE = 512
HOST_TILES_PER_SEQ = 1
VMEM_LIMIT_HOST = 60000 * 1024
MASK_VALUE = -1e30
LOG2_E = 1.4426950408889634

_NT = (((1,), (1,)), ((), ()))
_TN = (((0,), (0,)), ((), ()))


def _row_tile(m):
    for cand in (688, 512, 256, 128, 64, 32, 16, 8):
        if m % cand == 0:
            return cand
    raise ValueError(f"row count {m} is not a multiple of 8")


def _col_tile(n):
    for cand in (512, 256, 128):
        if n % cand == 0:
            return cand
    raise ValueError(f"column count {n} is not a multiple of 128")


def _params(*sem, limit=VMEM_LIMIT):
    return pltpu.CompilerParams(dimension_semantics=sem, vmem_limit_bytes=limit)


def _layer_norm(y, g, b):
    mu = jnp.mean(y, axis=-1, keepdims=True)
    yc = y - mu
    var = jnp.mean(yc * yc, axis=-1, keepdims=True)
    return yc * lax.rsqrt(var + LN_EPS) * g + b


def _rms_norm(y, g):
    return y * lax.rsqrt(jnp.mean(y * y, axis=-1, keepdims=True) + LN_EPS) * g


def _lane_col(row, n):
    return jnp.transpose(jnp.broadcast_to(row, (n, n)))


def _gla_proj_kernel(*refs, tn, n_meta, tiles_per_seq):
    if n_meta is None:
        x_ref, w_ref, wg_ref, wgate_ref, bgate_ref, qk_ref, vr_ref, lg_ref = refs
    else:
        new_ref, meta_ref, w_ref, wg_ref, wgate_ref, bgate_ref, qk_ref, vr_ref, lg_ref, x_ref = refs
        tm = x_ref.shape[0]
        first_tile = pl.program_id(0) % tiles_per_seq == 0

        @pl.when(first_tile)
        def _():
            x_ref[0:n_meta, :] = meta_ref[...]
            if n_meta < tm:
                x_ref[n_meta:tm, :] = new_ref[0, 0:tm - n_meta, :]

        @pl.when(jnp.logical_not(first_tile))
        def _():
            x_ref[...] = new_ref[0]

    xb = x_ref[...].astype(BF16)
    g_low = jnp.dot(xb, wg_ref[...], preferred_element_type=F32)
    z = jnp.dot(g_low.astype(BF16), wgate_ref[...], preferred_element_type=F32) + bgate_ref[...]
    log_sig = jnp.minimum(z, 0.0) - jnp.log(1.0 + jnp.exp(-jnp.abs(z)))
    lg_ref[...] = log_sig * (1.0 / GLA_TAU)
    n_qk = qk_ref.shape[1]
    for j in range((n_qk + vr_ref.shape[1]) // tn):
        y = jnp.dot(xb, w_ref[:, j * tn:(j + 1) * tn], preferred_element_type=F32)
        if j * tn < n_qk:
            qk_ref[:, j * tn:(j + 1) * tn] = y
        else:
            vr_ref[:, j * tn - n_qk:(j + 1) * tn - n_qk] = y.astype(vr_ref.dtype)


def _gla_project(x, w_in, w_glow, w_gate, b_gate, dv, meta=None):
    d = x.shape[-1]
    ng = w_gate.shape[1]
    n_qk, n_vr = 2 * ng, 2 * GLA_HEADS * dv
    assert w_in.shape[1] >= n_qk + n_vr
    if meta is None:
        m = x.shape[0]
        tm = _row_tile(m)
        n_meta = tps = None
        x_specs, x_args = [pl.BlockSpec((tm, d), lambda i: (i, 0))], [x]
    else:
        batch, seq_new, _ = x.shape
        n_meta = meta.shape[0]
        seq = n_meta + seq_new
        m = batch * seq
        tm = _row_tile(seq)
        tps = seq // tm
        assert n_meta <= tm and n_meta % SUBLANES == 0
        x_specs = [pl.BlockSpec((pl.Element(1), pl.Element(tm), pl.Element(d)),
                                lambda i: (i // tps, pl.multiple_of(jnp.maximum((i % tps) * tm - n_meta, 0),
                                                                    SUBLANES), 0)),
                   pl.BlockSpec((n_meta, d), lambda i: (0, 0))]
        x_args = [x, meta]
    row_out = [pl.BlockSpec((tm, n_qk), lambda i: (i, 0)), pl.BlockSpec((tm, n_vr), lambda i: (i, 0)),
               pl.BlockSpec((tm, ng), lambda i: (i, 0))]
    out_shape = [jax.ShapeDtypeStruct((m, n_qk), F32), jax.ShapeDtypeStruct((m, n_vr), BF16),
                 jax.ShapeDtypeStruct((m, ng), F32)]
    if meta is not None:
        row_out.append(pl.BlockSpec((tm, d), lambda i: (i, 0)))
        out_shape.append(jax.ShapeDtypeStruct((m, d), F32))
    return pl.pallas_call(
        functools.partial(_gla_proj_kernel, tn=_col_tile(n_qk), n_meta=n_meta, tiles_per_seq=tps),
        grid=(m // tm,),
        in_specs=x_specs + [
            pl.BlockSpec((d, n_qk + n_vr), lambda i: (0, 0)),
            pl.BlockSpec((d, LANES), lambda i: (0, 0)),
            pl.BlockSpec((LANES, ng), lambda i: (0, 0)),
            pl.BlockSpec((1, ng), lambda i: (0, 0)),
        ],
        out_specs=row_out,
        out_shape=out_shape,
        compiler_params=_params("arbitrary"),
        name="gla_proj",
    )(*x_args, w_in, w_glow, w_gate, b_gate)


def _cast_specs(w1, w2, layer, n_steps, step_of):
    _, d, ff = w1.shape
    assert d % n_steps == 0 and ff % n_steps == 0 and (d // n_steps) % BF16_ROWS == 0
    slabs = [(d // n_steps, ff), (ff // n_steps, d)]
    in_specs = [pl.BlockSpec((None,) + s, lambda *idx: (layer, step_of(*idx), 0)) for s in slabs]
    out_specs = [pl.BlockSpec(s, lambda *idx: (step_of(*idx), 0)) for s in slabs]
    return in_specs, out_specs, [jax.ShapeDtypeStruct((d, ff), BF16), jax.ShapeDtypeStruct((ff, d), BF16)]


def _gla_core_kernel(q_ref, k_ref, v_ref, r_ref, lg_ref, gn_ref, w1_ref, w2_ref,
                     o_ref, st_ref, w1b_ref, w2b_ref, s_scr, *, n_meta, chunk, n_chunks, dk, dv):
    w1b_ref[...] = w1_ref[...].astype(BF16)
    w2b_ref[...] = w2_ref[...].astype(BF16)
    wk = q_ref.shape[1]
    heads = range(wk // dk)
    scale = dk ** -0.5
    s_scr[...] = jnp.zeros_like(s_scr)

    def do_chunk(off, length):
        rows = pl.ds(off, length)
        ii = lax.broadcasted_iota(jnp.int32, (length, length), 0)
        jj = lax.broadcasted_iota(jnp.int32, (length, length), 1)
        causal = ii >= jj
        lg = lg_ref[rows, :]
        hi = lg.astype(BF16)
        rest = lg - hi.astype(F32)
        mid = rest.astype(BF16)
        lo = (rest - mid.astype(F32)).astype(BF16)
        parts = jnp.dot(causal.astype(BF16), jnp.concatenate([hi, mid, lo], axis=1),
                        preferred_element_type=F32)
        b = parts[:, 0:wk] + parts[:, wk:2 * wk] + parts[:, 2 * wk:3 * wk]
        q = q_ref[rows, :] * scale
        k = k_ref[rows, :]
        b_mid = b[length // 2:length // 2 + 1, :]
        b_last = b[length - 1:length, :]
        q_in = (q * jnp.exp(b)).astype(BF16)
        qs = (q * jnp.exp(b - b_mid)).astype(BF16)
        ks = (k * jnp.exp(b_mid - b)).astype(BF16)
        kd = (k * jnp.exp(b_last - b)).astype(BF16)
        decay = jnp.exp(b_last)
        vb = v_ref[rows, :]

        def kcols(hh):
            return slice(hh * dk, (hh + 1) * dk)

        def vcols(hh):
            return slice(hh * dv, (hh + 1) * dv)

        states = [s_scr[hh] for hh in heads]
        o_inter = [lax.dot_general(q_in[:, kcols(hh)], states[hh].astype(BF16), _NT,
                                   preferred_element_type=F32) for hh in heads]
        scores = [lax.dot_general(qs[:, kcols(hh)], ks[:, kcols(hh)], _NT, preferred_element_type=F32)
                  for hh in heads]
        scores = [jnp.where(causal, s, 0.0).astype(BF16) for s in scores]
        o_intra = [jnp.dot(scores[hh], vb[:, vcols(hh)], preferred_element_type=F32) for hh in heads]
        grown = [lax.dot_general(vb[:, vcols(hh)], kd[:, kcols(hh)], _TN, preferred_element_type=F32)
                 for hh in heads]
        for hh in heads:
            s_scr[hh] = states[hh] * decay[:, kcols(hh)] + grown[hh]
            r = r_ref[rows, vcols(hh)].astype(F32)
            gated = _rms_norm(o_inter[hh] + o_intra[hh], gn_ref[...]) * (r / (1.0 + jnp.exp(-r)))
            o_ref[rows, vcols(hh)] = gated.astype(o_ref.dtype)

    def token_by_token():
        group = BF16_ROWS
        seq = q_ref.shape[0]

        def as_rows(row):
            return jnp.concatenate([row, jnp.zeros((SUBLANES - 1, row.shape[1]), row.dtype)], axis=0)

        def body(g, carry):
            rows = pl.ds(pl.multiple_of(g * group, group), group)
            a_g = jnp.exp(lg_ref[rows, :])
            q_g = q_ref[rows, :] * scale
            k_g = k_ref[rows, :]
            v_g = v_ref[rows, :].astype(F32)
            r_g = r_ref[rows, :].astype(F32)
            outs = [[] for _ in heads]
            for t in range(group):
                a, q, k, v, r = (x[t:t + 1] for x in (a_g, q_g, k_g, v_g, r_g))
                for hh in heads:
                    kc, vc = slice(hh * dk, (hh + 1) * dk), slice(hh * dv, (hh + 1) * dv)
                    grown = lax.dot_general(as_rows(v[:, vc]).astype(BF16), as_rows(k[:, kc]).astype(BF16),
                                            _TN, preferred_element_type=F32)
                    s_new = s_scr[hh] * a[:, kc] + grown
                    s_scr[hh] = s_new
                    o = lax.dot_general(as_rows(q[:, kc]).astype(BF16), s_new.astype(BF16), _NT,
                                        preferred_element_type=F32)[0:1]
                    rr = r[:, vc]
                    outs[hh].append(_rms_norm(o, gn_ref[...]) * (rr / (1.0 + jnp.exp(-rr))))
            for hh in heads:
                o_ref[rows, hh * dv:(hh + 1) * dv] = jnp.concatenate(
                    outs[hh], axis=0).astype(o_ref.dtype)
            return carry

        lax.fori_loop(0, seq // group, body, 0)

    half_span = max(chunk, n_meta) // 2
    chunked_is_safe = jnp.min(lg_ref[...]) * half_span > -GLA_SAFE_LOG_SPAN

    @pl.when(chunked_is_safe)
    def _():
        do_chunk(0, n_meta)
        for c in range(n_chunks):
            do_chunk(n_meta + c * chunk, chunk)

    pl.when(jnp.logical_not(chunked_is_safe))(token_by_token)

    for hh in heads:
        st_ref[0, hh] = jnp.transpose(s_scr[hh])


def _gla_core(qk, vr, lg, g_norm, batch, seq, n_meta, mlp_w1, mlp_w2, layer):
    m = qk.shape[0]
    hk = lg.shape[1]
    dk = hk // GLA_HEADS
    dv = g_norm.shape[1]
    hv = GLA_HEADS * dv
    assert qk.shape[1] == 2 * hk and vr.shape[1] == 2 * hv and m == batch * seq
    chunk = GLA_CHUNK
    n_chunks = (seq - n_meta) // chunk
    assert n_meta + n_chunks * chunk == seq
    hps = GLA_HEADS_PER_STEP
    groups = GLA_HEADS // hps
    wk, wv = hps * dk, hps * dv
    kern = functools.partial(_gla_core_kernel, n_meta=n_meta, chunk=chunk, n_chunks=n_chunks, dk=dk, dv=dv)
    cast_in, cast_out, cast_shapes = _cast_specs(mlp_w1, mlp_w2, layer, batch * groups,
                                                 lambda b, g: b * groups + g)
    return pl.pallas_call(
        kern,
        grid=(batch, groups),
        in_specs=[
            pl.BlockSpec((seq, wk), lambda b, g: (b, g)),
            pl.BlockSpec((seq, wk), lambda b, g: (b, groups + g)),
            pl.BlockSpec((seq, wv), lambda b, g: (b, g)),
            pl.BlockSpec((seq, wv), lambda b, g: (b, groups + g)),
            pl.BlockSpec((seq, wk), lambda b, g: (b, g)),
            pl.BlockSpec((1, dv), lambda b, g: (0, 0)),
        ] + cast_in,
        out_specs=[
            pl.BlockSpec((seq, wv), lambda b, g: (b, g)),
            pl.BlockSpec((1, hps, dk, dv), lambda b, g: (b, g, 0, 0)),
        ] + cast_out,
        out_shape=[jax.ShapeDtypeStruct((m, hv), BF16),
                   jax.ShapeDtypeStruct((batch, GLA_HEADS, dk, dv), F32)] + cast_shapes,
        scratch_shapes=[pltpu.VMEM((hps, dv, dk), F32)],
        compiler_params=_params("arbitrary", "arbitrary"),
        name="gla_core",
    )(qk, qk, vr, vr, lg, g_norm, mlp_w1, mlp_w2)


def _gla_step_kernel(qk_ref, vr_ref, lg_ref, gn_ref, s_ref, o_ref, sn_ref):
    dk, dv = s_ref.shape[2], s_ref.shape[3]
    hk, hv = GLA_HEADS * dk, GLA_HEADS * dv
    scale = dk ** -0.5
    reps = dv // dk
    for sq in range(s_ref.shape[0]):
        for hh in range(GLA_HEADS):
            q = qk_ref[sq, :, hh * dk:(hh + 1) * dk] * scale
            k = qk_ref[sq, :, hk + hh * dk:hk + (hh + 1) * dk]
            v = vr_ref[sq, :, hh * dv:(hh + 1) * dv]
            r = vr_ref[sq, :, hv + hh * dv:hv + (hh + 1) * dv]
            a = jnp.exp(lg_ref[sq, :, hh * dk:(hh + 1) * dk])
            a_c = jnp.tile(_lane_col(a, dk), (1, reps))
            k_c = jnp.tile(_lane_col(k, dk), (1, reps))
            q_c = jnp.tile(_lane_col(q, dk), (1, reps))
            s_new = s_ref[sq, hh] * a_c + k_c * v
            sn_ref[sq, hh] = s_new
            o = jnp.sum(q_c * s_new, axis=0, keepdims=True)
            gated = _rms_norm(o, gn_ref[...]) * (r / (1.0 + jnp.exp(-r)))
            o_ref[sq, :, hh * dv:(hh + 1) * dv] = gated.astype(o_ref.dtype)


def _gla_step(qk, vr, lg, g_norm, state):
    bs = qk.shape[0]
    _, _, dk, dv = state.shape
    hv = GLA_HEADS * dv
    nb = math.gcd(bs, GLA_STEP_SEQS)
    o, s_new = pl.pallas_call(
        _gla_step_kernel,
        grid=(bs // nb,),
        in_specs=[
            pl.BlockSpec((nb, 1, qk.shape[1]), lambda b: (b, 0, 0)),
            pl.BlockSpec((nb, 1, vr.shape[1]), lambda b: (b, 0, 0)),
            pl.BlockSpec((nb, 1, lg.shape[1]), lambda b: (b, 0, 0)),
            pl.BlockSpec((1, dv), lambda b: (0, 0)),
            pl.BlockSpec((nb, GLA_HEADS, dk, dv), lambda b: (b, 0, 0, 0)),
        ],
        out_specs=[
            pl.BlockSpec((nb, 1, hv), lambda b: (b, 0, 0)),
            pl.BlockSpec((nb, GLA_HEADS, dk, dv), lambda b: (b, 0, 0, 0)),
        ],
        out_shape=[jax.ShapeDtypeStruct((bs, 1, hv), BF16), jax.ShapeDtypeStruct(state.shape, F32)],
        compiler_params=_params("arbitrary"),
        name="gla_step",
    )(qk.reshape(bs, 1, -1), vr.reshape(bs, 1, -1), lg.reshape(bs, 1, -1), g_norm, state)
    return o.reshape(bs, hv), s_new


def _lambda_value(lp):
    e1 = jnp.exp(jnp.sum(lp[0:1] * lp[1:2], axis=1, keepdims=True))
    e2 = jnp.exp(jnp.sum(lp[2:3] * lp[3:4], axis=1, keepdims=True))
    return e1 - e2 + LAM_INIT


class _Decode:
    def __init__(self, q_ref, m_scr, l_scr, acc_scr):
        self.m_scr, self.l_scr, self.acc_scr = m_scr, l_scr, acc_scr
        _, self.n_head, self.dv = q_ref.shape
        qt = q_ref[0]
        first_half = lax.broadcasted_iota(jnp.int32, qt.shape, 1) < self.dv // 2
        self.q_rows = jnp.concatenate([jnp.where(first_half, qt, 0.0), jnp.where(first_half, 0.0, qt)],
                                      axis=0).astype(BF16)

    def init(self):
        self.m_scr[...] = jnp.full(self.m_scr.shape, MASK_VALUE, F32)
        self.l_scr[...] = jnp.zeros_like(self.l_scr)
        self.acc_scr[...] = jnp.zeros_like(self.acc_scr)

    def own_head(self, n, n_valid):
        shape = (2 * self.n_head, n)
        col = lax.broadcasted_iota(jnp.int32, shape, 1)
        row = lax.broadcasted_iota(jnp.int32, shape, 0)
        return jnp.logical_and(col % self.n_head == row % self.n_head, col < n_valid)

    def scores(self, ks, own):
        return [jnp.where(own, lax.dot_general(self.q_rows, k2.astype(BF16), _NT, preferred_element_type=F32),
                          MASK_VALUE) for k2 in ks]

    def accumulate(self, ss, vs):
        m_old = self.m_scr[...]
        m_new = jnp.maximum(m_old, jnp.max(functools.reduce(jnp.maximum, ss), axis=1, keepdims=True))
        alpha = jnp.exp2(m_old - m_new)
        ps = [jnp.exp2(s - m_new) for s in ss]
        self.m_scr[...] = m_new
        self.l_scr[...] = alpha * self.l_scr[...] + jnp.sum(functools.reduce(jnp.add, ps), axis=1, keepdims=True)
        pv = [jnp.dot(p.astype(BF16), v2.astype(BF16), preferred_element_type=F32) for p, v2 in zip(ps, vs)]
        self.acc_scr[...] = alpha * self.acc_scr[...] + functools.reduce(jnp.add, pv)

    def page_scores(self, k_refs):
        n = k_refs[0].shape[1] * self.n_head
        return self.scores([r[0].reshape(n, self.dv) for r in k_refs], self.own_head(n, n))

    def accumulate_pages(self, ss, v_refs):
        n = v_refs[0].shape[1] * self.n_head
        self.accumulate(ss, [r[0].reshape(n, self.dv) for r in v_refs])

    def finish(self, kc_ref, vc_ref, lam_ref, g_ref, o_ref):
        fill = jnp.zeros((LANES - self.n_head, self.dv), F32)
        ss = self.scores([jnp.concatenate([kc_ref[0], fill], axis=0)], self.own_head(LANES, self.n_head))
        self.accumulate(ss, [jnp.concatenate([vc_ref[0], fill], axis=0)])
        ratio = self.acc_scr[...] / self.l_scr[...]
        o = ratio[:self.n_head] - _lambda_value(lam_ref[...]) * ratio[self.n_head:]
        o_ref[0] = _rms_norm(o, g_ref[...]) * (1.0 - LAM_INIT)


def _kept_row_runs(tile, tm, m, seq, n_meta):
    lo, hi = tile * tm, min((tile + 1) * tm, m)
    runs, r = [], lo
    while r < hi:
        b, t = divmod(r, seq)
        if t < n_meta:
            r = min(b * seq + n_meta, hi)
            continue
        end = min((b + 1) * seq, hi)
        runs.append((r - lo, end - r, b * (seq - n_meta) + t - n_meta))
        r = end
    return tuple(runs)


def _finish_kernel(*refs, n_pages_step, tiles_per_seq, kept_runs):
    hosting = n_pages_step > 0
    if hosting:
        refs = refs[1:]
    a_ref, wo_ref, x_ref, g1_ref, b1_ref, w1_ref, w2_ref, g2_ref, b2_ref = refs[:9]
    refs = refs[9:]
    if hosting:
        lam_ref, q_ref, kc_ref, vc_ref, gs_ref = refs[:5]
        k_refs = refs[5:5 + n_pages_step]
        v_refs = refs[5 + n_pages_step:5 + 2 * n_pages_step]
        o_ref, od_ref, xb_scr, acc_scr, m_scr, l_scr, dacc_scr, *refs = refs[5 + 2 * n_pages_step:]
        dec = _Decode(q_ref, m_scr, l_scr, dacc_scr)
    else:
        o_ref, xb_scr, acc_scr, *refs = refs
    if kept_runs is None:
        x1_ref = o_ref
    else:
        x1_ref, res_scr, out_sem = refs

        def out_copies(tile):
            return [pltpu.make_async_copy(res_scr.at[pl.ds(src, n), :], o_ref.at[pl.ds(dst, n), :], out_sem)
                    for src, n, dst in kept_runs[tile]]
    f = pl.program_id(1)
    last_f = pl.num_programs(1) - 1
    tile_in_seq = pl.program_id(0) % tiles_per_seq

    @pl.when(f == 0)
    def _():
        proj = jnp.dot(a_ref[...], wo_ref[...], preferred_element_type=F32)
        x1 = _layer_norm(DEEPNORM_ALPHA * x_ref[...] + proj, g1_ref[...], b1_ref[...])
        x1_ref[...] = x1
        xb_scr[...] = x1.astype(BF16)
        acc_scr[...] = jnp.zeros_like(acc_scr)

    if hosting:
        pl.when(jnp.logical_and(f == 0, tile_in_seq == 0))(dec.init)

    if hosting:
        page_scores = dec.page_scores(k_refs)
    hid = jnp.maximum(jnp.dot(xb_scr[...], w1_ref[...], preferred_element_type=F32), 0.0)
    if hosting:
        dec.accumulate_pages(page_scores, v_refs)
    acc_scr[...] += jnp.dot((hid * hid).astype(BF16), w2_ref[...], preferred_element_type=F32)

    @pl.when(f == last_f)
    def _():
        result = _layer_norm(DEEPNORM_ALPHA * x1_ref[...] + acc_scr[...], g2_ref[...], b2_ref[...])
        if kept_runs is None:
            o_ref[...] = result
        else:
            tile = pl.program_id(0)
            n_tiles = len(kept_runs)
            for t in range(1, n_tiles):
                @pl.when(tile == t)
                def _(t=t):
                    for copy in out_copies(t - 1):
                        copy.wait()
            res_scr[...] = result
            for t in range(n_tiles):
                @pl.when(tile == t)
                def _(t=t):
                    for copy in out_copies(t):
                        copy.start()
                    if t == n_tiles - 1:
                        for copy in out_copies(t):
                            copy.wait()

    if hosting:
        @pl.when(jnp.logical_and(f == last_f, tile_in_seq == tiles_per_seq - 1))
        def _():
            dec.finish(kc_ref, vc_ref, lam_ref, gs_ref, od_ref)


def _finish_layer(a, w_out, x, ln1, w1, w2, ln2, decode=None, drop_meta=None):
    m, kk = a.shape
    d = w_out.shape[1]
    ff = w1.shape[1]
    tf = min(MLP_FF_TILE, ff)
    n_f = ff // tf
    tps = HOST_TILES_PER_SEQ
    if decode is None:
        tm, pps = _row_tile(m), 0
        n_tiles = m // tm
    else:
        q, k_cur, v_cur, cache_k, cache_v, page_table, lam_params, g_sub = decode
        n_seqs, n_head, dv = q.shape
        n_tiles = n_seqs * tps
        tm = pl.cdiv(pl.cdiv(m, n_tiles), BF16_ROWS) * BF16_ROWS
        n_pages, page = page_table.shape[1], cache_k.shape[1]
        assert n_pages % (tps * n_f) == 0 and pl.cdiv(m, tm) == n_tiles
        pps = n_pages // (tps * n_f)
    row_spec = pl.BlockSpec((tm, d), lambda i, f, *_: (i, 0))
    vec_spec = pl.BlockSpec((1, d), lambda i, f, *_: (0, 0))
    in_specs = [
        pl.BlockSpec((tm, kk), lambda i, f, *_: (i, 0)),
        pl.BlockSpec((kk, d), lambda i, f, *_: (0, 0)),
        row_spec, vec_spec, vec_spec,
        pl.BlockSpec((d, tf), lambda i, f, *_: (0, f)),
        pl.BlockSpec((tf, d), lambda i, f, *_: (f, 0)),
        vec_spec, vec_spec,
    ]
    args = [a, w_out, x, *ln1, w1, w2, *ln2]
    out_specs = [row_spec]
    out_shape = [jax.ShapeDtypeStruct((m, d), F32)]
    scratch = [pltpu.VMEM((tm, d), BF16), pltpu.VMEM((tm, d), F32)]
    prefetch = []
    if decode is not None:
        head_spec = pl.BlockSpec((1, n_head, dv), lambda i, f, pt: (i // tps, 0, 0))

        def page_spec(s):
            return pl.BlockSpec((1, page, n_head, dv),
                                lambda i, f, pt: (pt[i // tps, ((i % tps) * n_f + f) * pps + s], 0, 0, 0))

        in_specs += [pl.BlockSpec(lam_params.shape, lambda i, f, pt: (0, 0)), head_spec, head_spec, head_spec,
                     pl.BlockSpec((1, dv), lambda i, f, pt: (0, 0))]
        in_specs += [page_spec(s) for s in range(pps)] * 2
        args += [lam_params, q, k_cur, v_cur, g_sub] + [cache_k] * pps + [cache_v] * pps
        out_specs.append(head_spec)
        out_shape.append(jax.ShapeDtypeStruct((n_seqs, n_head, dv), F32))
        scratch += [pltpu.VMEM((2 * n_head, 1), F32), pltpu.VMEM((2 * n_head, 1), F32),
                    pltpu.VMEM((2 * n_head, dv), F32)]
        prefetch = [page_table]
    kept_runs = None
    if drop_meta is not None:
        seq, n_meta = drop_meta
        kept_runs = tuple(_kept_row_runs(t, tm, m, seq, n_meta) for t in range(n_tiles))
        out_specs[0] = pl.BlockSpec(memory_space=pl.ANY)
        out_shape[0] = jax.ShapeDtypeStruct((m // seq * (seq - n_meta), d), F32)
        scratch += [pltpu.VMEM((tm, d), F32), pltpu.VMEM((tm, d), F32), pltpu.SemaphoreType.DMA(())]
    outs = pl.pallas_call(
        functools.partial(_finish_kernel, n_pages_step=pps, tiles_per_seq=tps, kept_runs=kept_runs),
        grid_spec=pltpu.PrefetchScalarGridSpec(
            num_scalar_prefetch=len(prefetch), grid=(n_tiles, n_f),
            in_specs=in_specs, out_specs=out_specs, scratch_shapes=scratch),
        out_shape=out_shape,
        compiler_params=_params("arbitrary", "arbitrary",
                                limit=VMEM_LIMIT if decode is None else VMEM_LIMIT_HOST),
        name="finish_layer" if decode is None else "finish_layer_host",
    )(*prefetch, *args)
    return outs[0] if decode is None else tuple(outs)


def _diff_proj_kernel(x_ref, w_ref, cos_ref, sin_up_ref, sin_dn_ref,
                      q_ref, k_ref, kb_ref, v_ref, vb_ref, *, tn, q_scale, rot_half):
    xb = x_ref[...].astype(BF16)
    nq = q_ref.shape[1]

    def project(seg, j):
        lo = seg * nq + j * tn
        return jnp.dot(xb, w_ref[:, lo:lo + tn], preferred_element_type=F32)

    def rope(t):
        groups = []
        for g in range(tn // LANES):
            tg = t[:, g * LANES:(g + 1) * LANES]
            groups.append(tg * cos_ref[...] + pltpu.roll(tg, LANES - rot_half, 1) * sin_up_ref[...]
                          + pltpu.roll(tg, rot_half, 1) * sin_dn_ref[...])
        return jnp.concatenate(groups, axis=1)

    for j in range(nq // tn):
        cols = slice(j * tn, (j + 1) * tn)
        q_ref[:, cols] = (rope(project(0, j)) * q_scale).astype(q_ref.dtype)
        kr = rope(project(1, j))
        k_ref[:, cols] = kr
        kb_ref[:, cols] = kr.astype(kb_ref.dtype)
        y = project(2, j)
        v_ref[:, cols] = y
        vb_ref[:, cols] = y.astype(vb_ref.dtype)


def _rope_tables(pos, dh):
    rot = dh // 4
    inv = ROPE_THETA ** (-jnp.arange(0, rot, 2, dtype=F32) / rot)
    ang = pos.astype(F32)[:, None] * inv[None, :]
    n = pos.shape[0]

    def lanes(first, second, fill):
        sub = jnp.concatenate([first, second, jnp.full((n, dh - rot), fill, F32)], axis=1)
        return jnp.tile(sub, (1, LANES // dh))

    zeros = jnp.zeros_like(ang)
    return (lanes(jnp.cos(ang), jnp.cos(ang), 1.0),
            lanes(-jnp.sin(ang), zeros, 0.0),
            lanes(zeros, jnp.sin(ang), 0.0))


def _diff_project(x, w, pos):
    m, d = x.shape
    n = w.shape[1]
    nq = n // 3
    dh = nq // (2 * DIFF_HEADS)
    assert m % pos.shape[0] == 0
    tm = _row_tile(pos.shape[0])
    pos_tiles = pos.shape[0] // tm
    cos_t, sin_up, sin_dn = _rope_tables(pos, dh)
    kern = functools.partial(_diff_proj_kernel, tn=_col_tile(nq), q_scale=dh ** -0.5 * LOG2_E, rot_half=dh // 8)
    tab_spec = pl.BlockSpec((tm, LANES), lambda i: (i % pos_tiles, 0))
    out_spec = pl.BlockSpec((tm, nq), lambda i: (i, 0))
    return pl.pallas_call(
        kern,
        grid=(m // tm,),
        in_specs=[
            pl.BlockSpec((tm, d), lambda i: (i, 0)),
            pl.BlockSpec((d, n), lambda i: (0, 0)),
            tab_spec, tab_spec, tab_spec,
        ],
        out_specs=[out_spec] * 5,
        out_shape=[jax.ShapeDtypeStruct((m, nq), BF16),
                   jax.ShapeDtypeStruct((m, nq), F32), jax.ShapeDtypeStruct((m, nq), BF16),
                   jax.ShapeDtypeStruct((m, nq), F32), jax.ShapeDtypeStruct((m, nq), BF16)],
        compiler_params=_params("arbitrary"),
        name="diff_proj",
    )(x, w, cos_t, sin_up, sin_dn)


def _diff_attn_kernel(lam_ref, q_ref, k_ref, v_ref, g_ref, w1_ref, w2_ref, o_ref, w1b_ref, w2b_ref,
                      kp_scr, vp_scr, s_scr, a_scr, *, n_meta, tile, n_tiles):
    w1b_ref[...] = w1_ref[...].astype(BF16)
    w2b_ref[...] = w2_ref[...].astype(BF16)
    lam = _lambda_value(lam_ref[...])
    seq, dq = k_ref.shape
    first = LANES
    pad = first - n_meta
    kp_scr[0:pad, :] = jnp.zeros((pad, dq), BF16)
    kp_scr[pad:pad + seq, :] = k_ref[...]
    vp_scr[0:pad, :] = jnp.zeros((pad, v_ref.shape[1]), BF16)
    vp_scr[pad:pad + seq, :] = v_ref[...]
    first_half = lax.broadcasted_iota(jnp.int32, (1, dq), 1) < dq // 2

    def stack(q):
        zero = jnp.zeros_like(q)
        return jnp.concatenate([jnp.where(first_half, q, zero), jnp.where(first_half, zero, q)], axis=0)

    def block_coords(length, width):
        ri = lax.broadcasted_iota(jnp.int32, (2 * length, width), 0)
        ri = jnp.where(ri >= length, ri - length, ri)
        ci = lax.broadcasted_iota(jnp.int32, (2 * length, width), 1)
        return ci, ri

    def finish(weights, width):
        o = jnp.dot(weights, vp_scr[0:width, :], preferred_element_type=F32)
        return (_rms_norm(o, g_ref[...]) * (1.0 - LAM_INIT)).astype(o_ref.dtype)

    ci, ri = block_coords(n_meta, first)
    s = lax.dot_general(stack(q_ref[0:n_meta, :]), kp_scr[0:first, :], _NT, preferred_element_type=F32)
    s = jnp.where(jnp.logical_and(ci >= pad, ci - pad <= ri), s, MASK_VALUE)
    p = jnp.exp2(s - jnp.max(s, axis=1, keepdims=True))
    p = p / jnp.sum(p, axis=1, keepdims=True)
    o_ref[0:n_meta, :] = finish((p[:n_meta] - lam * p[n_meta:]).astype(BF16), first)

    ci, ri = block_coords(tile, tile)
    causal = ci <= ri
    not_pad = block_coords(tile, first)[0] >= pad
    row_chunk = 64

    def key_block(j):
        return (0, first) if j == 0 else (first + (j - 1) * tile, first + j * tile)

    def score_tile(it):
        qoff = n_meta + it * tile
        qs = stack(q_ref[qoff:qoff + tile, :])
        for j in range(it + 2):
            lo, hi = key_block(j)
            s = lax.dot_general(qs, kp_scr[lo:hi, :], _NT, preferred_element_type=F32)
            if j == 0:
                s = jnp.where(not_pad, s, MASK_VALUE)
            if j == it + 1:
                s = jnp.where(causal, s, MASK_VALUE)
            s_scr[it % 2, :, lo:hi] = s

    def softmax_tile(it):
        buf = it % 2
        qoff = n_meta + it * tile
        width = key_block(it + 1)[1]
        col_chunks = [slice(c * LANES, (c + 1) * LANES) for c in range(width // LANES)]

        def normaliser(rsl):
            mx = functools.reduce(jnp.maximum, [s_scr[buf, rsl, csl] for csl in col_chunks])
            m_b = jnp.broadcast_to(jnp.max(mx, axis=1, keepdims=True), (row_chunk, LANES))
            total = jnp.zeros((row_chunk, LANES), F32)
            for csl in col_chunks:
                p = jnp.exp2(s_scr[buf, rsl, csl] - m_b)
                s_scr[buf, rsl, csl] = p
                total = total + p
            return jnp.broadcast_to(1.0 / jnp.sum(total, axis=1, keepdims=True), (row_chunk, LANES))

        for rc in range(tile // row_chunk):
            rows1 = slice(rc * row_chunk, (rc + 1) * row_chunk)
            rows2 = slice(tile + rc * row_chunk, tile + (rc + 1) * row_chunk)
            c1 = normaliser(rows1)
            c2 = lam * normaliser(rows2)
            for csl in col_chunks:
                a_scr[buf, rows1, csl] = (s_scr[buf, rows1, csl] * c1 - s_scr[buf, rows2, csl] * c2).astype(BF16)
        o_ref[qoff:qoff + tile, :] = finish(a_scr[buf, :, 0:width], width)

    score_tile(0)
    for it in range(n_tiles):
        if it + 1 < n_tiles:
            score_tile(it + 1)
        softmax_tile(it)


def _diff_attention(q, k, v, lam_params, g_sub, batch, seq, n_meta, mlp_w1, mlp_w2, layer):
    m, n = q.shape
    dv = g_sub.shape[1]
    tile = ATTN_TILE
    n_tiles = (seq - n_meta) // tile
    assert n_meta + n_tiles * tile == seq and n == DIFF_HEADS * dv
    assert n_meta <= LANES and (LANES - n_meta) % BF16_ROWS == 0
    padded = LANES + n_tiles * tile
    kern = functools.partial(_diff_attn_kernel, n_meta=n_meta, tile=tile, n_tiles=n_tiles)
    head_spec = pl.BlockSpec((seq, dv), lambda b, hh: (b, hh))
    cast_in, cast_out, cast_shapes = _cast_specs(mlp_w1, mlp_w2, layer, batch * DIFF_HEADS,
                                                 lambda b, hh: b * DIFF_HEADS + hh)
    return pl.pallas_call(
        kern,
        grid=(batch, DIFF_HEADS),
        in_specs=[
            pl.BlockSpec(lam_params.shape, lambda b, hh: (0, 0)),
            head_spec, head_spec, head_spec,
            pl.BlockSpec((1, dv), lambda b, hh: (0, 0)),
        ] + cast_in,
        out_specs=[head_spec] + cast_out,
        out_shape=[jax.ShapeDtypeStruct((m, n), BF16)] + cast_shapes,
        scratch_shapes=[pltpu.VMEM((padded, dv), BF16), pltpu.VMEM((padded, dv), BF16),
                        pltpu.VMEM((2, 2 * tile, padded), F32), pltpu.VMEM((2, tile, padded), BF16)],
        compiler_params=_params("arbitrary", "arbitrary"),
        name="diff_attn",
    )(lam_params, q, k, v, g_sub, mlp_w1, mlp_w2)


def kernel(x_prompt, x_sample, state_gla, cache_k, cache_v, page_table, meta_tokens,
           gla_w_in, gla_w_gate, gla_b_gate, gla_norm, gla_w_out,
           diff_w_in, diff_lambda, diff_norm, diff_w_out,
           mlp_w1, mlp_w2, ln_mix_g, ln_mix_b, ln_mlp_g, ln_mlp_b):
    batch, seq_new, d = x_prompt.shape
    n_meta = meta_tokens.shape[0]
    seq = n_meta + seq_new
    bs = x_sample.shape[0]
    assert x_sample.shape[1] == 1 and bs % 2 == 0

    x_s = x_sample.reshape(bs, d)

    def row(vec):
        return vec.reshape(1, -1).astype(F32)

    def finish_layer(layer, o, x, w_out, w1, w2, decode=None, drop_meta=None):
        return _finish_layer(o, w_out, x, (row(ln_mix_g[layer]), row(ln_mix_b[layer])), w1, w2,
                             (row(ln_mlp_g[layer]), row(ln_mlp_b[layer])), decode, drop_meta)

    rank = gla_w_gate.shape[1]
    n_main = gla_w_in.shape[2] - rank
    w_in0 = gla_w_in[0].astype(BF16)
    w_glow = jnp.pad(gla_w_in[0, :, n_main:], ((0, 0), (0, LANES - rank))).astype(BF16)
    w_gate = jnp.pad(gla_w_gate[0], ((0, LANES - rank), (0, 0))).astype(BF16)
    b_gate = row(gla_b_gate[0])
    g_norm = row(gla_norm[0])
    w_out0 = gla_w_out[0].astype(BF16)
    w_in1 = diff_w_in[0].astype(BF16)
    w_out1 = diff_w_out[0].astype(BF16)
    g_sub = row(diff_norm[0])
    lam_params = diff_lambda[0].astype(F32)
    dv = g_sub.shape[1]
    dv_gla = g_norm.shape[1]
    past = page_table.shape[1] * PAGE_SIZE

    qk_p, vr_p, lg_p, x_p = _gla_project(x_prompt, w_in0, w_glow, w_gate, b_gate, dv_gla,
                                         meta=meta_tokens.astype(x_prompt.dtype))
    o_p, st_p, w1_0, w2_0 = _gla_core(qk_p, vr_p, lg_p, g_norm, batch, seq, n_meta, mlp_w1, mlp_w2, 0)

    qk_s, vr_s, lg_s = _gla_project(x_s, w_in0, w_glow, w_gate, b_gate, dv_gla)
    o_s, st_s = _gla_step(qk_s, vr_s.astype(F32), lg_s, g_norm, state_gla[0].astype(F32))
    x_s = finish_layer(0, o_s, x_s, w_out0, w1_0, w2_0)
    q_s, k_s, _, v_s, _ = _diff_project(x_s, w_in1, jnp.full((bs,), past))
    heads = (bs, DIFF_HEADS, dv)
    q_s3, k_s3, v_s3 = q_s.astype(F32).reshape(heads), k_s.reshape(heads), v_s.reshape(heads)

    def hosted(lo):
        hi = lo + bs // 2
        return (q_s3[lo:hi], k_s3[lo:hi], v_s3[lo:hi], cache_k[0], cache_v[0], page_table[lo:hi],
                lam_params, g_sub)

    x_p, a_s_lo = finish_layer(0, o_p, x_p, w_out0, w1_0, w2_0, hosted(0))
    q_p, k_p, kb_p, v_p, vb_p = _diff_project(x_p, w_in1, jnp.arange(seq))
    a_p, w1_1, w2_1 = _diff_attention(q_p, kb_p, vb_p, lam_params, g_sub, batch, seq, n_meta, mlp_w1, mlp_w2, 1)
    y_p, a_s_hi = finish_layer(1, a_p, x_p, w_out1, w1_1, w2_1, hosted(bs // 2), drop_meta=(seq, n_meta))

    a_s = jnp.concatenate([a_s_lo, a_s_hi], axis=0).reshape(bs, DIFF_HEADS * dv).astype(BF16)
    x_s = finish_layer(1, a_s, x_s, w_out1, w1_1, w2_1)

    y_prompt = y_p.reshape(batch, seq_new, d)
    y_sample = x_s.reshape(bs, 1, d)
    return (y_prompt, y_sample, st_p[None], st_s[None],
            k_p.reshape(1, batch, seq, DIFF_HEADS, dv), v_p.reshape(1, batch, seq, DIFF_HEADS, dv),
            k_s.reshape(1, bs, 1, DIFF_HEADS, dv), v_s.reshape(1, bs, 1, DIFF_HEADS, dv))
```

```python
import functools
import math

import jax
import jax.numpy as jnp
from jax import lax
from jax.experimental import pallas as pl
from jax.experimental.pallas import tpu as pltpu

F32 = jnp.float32
BF16 = jnp.bfloat16

DEPTH = 2
GLA_HEADS = 4
GLA_TAU = 16.0
DIFF_HEADS = 8
ROPE_THETA = 500000.0
PAGE_SIZE = 128
LN_EPS = 1e-5
DEEPNORM_ALPHA = (2 * DEPTH) ** 0.25
LAM_INIT = 0.8 - 0.6 * math.exp(-0.3 * 1)

LANES = 128
SUBLANES = 8
BF16_ROWS = 16
VMEM_LIMIT = 48 * 1024 * 1024

GLA_CHUNK = 128
GLA_HEADS_PER_STEP = 2
GLA_STEP_SEQS = 8
GLA_SAFE_LOG_SPAN = 80.0
ATTN_TILE = 256
MLP_FF_TILE = 512
HOST_TILES_PER_SEQ = 1
VMEM_LIMIT_HOST = 60000 * 1024
MASK_VALUE = -1e30
LOG2_E = 1.4426950408889634

_NT = (((1,), (1,)), ((), ()))
_TN = (((0,), (0,)), ((), ()))


def _row_tile(m):
    for cand in (688, 512, 256, 128, 64, 32, 16, 8):
        if m % cand == 0:
            return cand
    raise ValueError(f"row count {m} is not a multiple of 8")


def _col_tile(n):
    for cand in (512, 256, 128):
        if n % cand == 0:
            return cand
    raise ValueError(f"column count {n} is not a multiple of 128")


def _params(*sem, limit=VMEM_LIMIT):
    return pltpu.CompilerParams(dimension_semantics=sem, vmem_limit_bytes=limit)


def _layer_norm(y, g, b):
    mu = jnp.mean(y, axis=-1, keepdims=True)
    yc = y - mu
    var = jnp.mean(yc * yc, axis=-1, keepdims=True)
    return yc * lax.rsqrt(var + LN_EPS) * g + b


def _rms_norm(y, g):
    return y * lax.rsqrt(jnp.mean(y * y, axis=-1, keepdims=True) + LN_EPS) * g


def _lane_col(row, n):
    return jnp.transpose(jnp.broadcast_to(row, (n, n)))


def _gla_proj_kernel(*refs, tn, n_meta, tiles_per_seq):
    if n_meta is None:
        x_ref, w_ref, wg_ref, wgate_ref, bgate_ref, qk_ref, vr_ref, lg_ref = refs
    else:
        new_ref, meta_ref, w_ref, wg_ref, wgate_ref, bgate_ref, qk_ref, vr_ref, lg_ref, x_ref = refs
        tm = x_ref.shape[0]
        first_tile = pl.program_id(0) % tiles_per_seq == 0

        @pl.when(first_tile)
        def _():
            x_ref[0:n_meta, :] = meta_ref[...]
            if n_meta < tm:
                x_ref[n_meta:tm, :] = new_ref[0, 0:tm - n_meta, :]

        @pl.when(jnp.logical_not(first_tile))
        def _():
            x_ref[...] = new_ref[0]

    xb = x_ref[...].astype(BF16)
    g_low = jnp.dot(xb, wg_ref[...], preferred_element_type=F32)
    z = jnp.dot(g_low.astype(BF16), wgate_ref[...], preferred_element_type=F32) + bgate_ref[...]
    log_sig = jnp.minimum(z, 0.0) - jnp.log(1.0 + jnp.exp(-jnp.abs(z)))
    lg_ref[...] = log_sig * (1.0 / GLA_TAU)
    n_qk = qk_ref.shape[1]
    for j in range((n_qk + vr_ref.shape[1]) // tn):
        y = jnp.dot(xb, w_ref[:, j * tn:(j + 1) * tn], preferred_element_type=F32)
        if j * tn < n_qk:
            qk_ref[:, j * tn:(j + 1) * tn] = y
        else:
            vr_ref[:, j * tn - n_qk:(j + 1) * tn - n_qk] = y.astype(vr_ref.dtype)


def _gla_project(x, w_in, w_glow, w_gate, b_gate, dv, meta=None):
    d = x.shape[-1]
    ng = w_gate.shape[1]
    n_qk, n_vr = 2 * ng, 2 * GLA_HEADS * dv
    assert w_in.shape[1] >= n_qk + n_vr
    if meta is None:
        m = x.shape[0]
        tm = _row_tile(m)
        n_meta = tps = None
        x_specs, x_args = [pl.BlockSpec((tm, d), lambda i: (i, 0))], [x]
    else:
        batch, seq_new, _ = x.shape
        n_meta = meta.shape[0]
        seq = n_meta + seq_new
        m = batch * seq
        tm = _row_tile(seq)
        tps = seq // tm
        assert n_meta <= tm and n_meta % SUBLANES == 0
        x_specs = [pl.BlockSpec((pl.Element(1), pl.Element(tm), pl.Element(d)),
                                lambda i: (i // tps, pl.multiple_of(jnp.maximum((i % tps) * tm - n_meta, 0),
                                                                    SUBLANES), 0)),
                   pl.BlockSpec((n_meta, d), lambda i: (0, 0))]
        x_args = [x, meta]
    row_out = [pl.BlockSpec((tm, n_qk), lambda i: (i, 0)), pl.BlockSpec((tm, n_vr), lambda i: (i, 0)),
               pl.BlockSpec((tm, ng), lambda i: (i, 0))]
    out_shape = [jax.ShapeDtypeStruct((m, n_qk), F32), jax.ShapeDtypeStruct((m, n_vr), BF16),
                 jax.ShapeDtypeStruct((m, ng), F32)]
    if meta is not None:
        row_out.append(pl.BlockSpec((tm, d), lambda i: (i, 0)))
        out_shape.append(jax.ShapeDtypeStruct((m, d), F32))
    return pl.pallas_call(
        functools.partial(_gla_proj_kernel, tn=_col_tile(n_qk), n_meta=n_meta, tiles_per_seq=tps),
        grid=(m // tm,),
        in_specs=x_specs + [
            pl.BlockSpec((d, n_qk + n_vr), lambda i: (0, 0)),
            pl.BlockSpec((d, LANES), lambda i: (0, 0)),
            pl.BlockSpec((LANES, ng), lambda i: (0, 0)),
            pl.BlockSpec((1, ng), lambda i: (0, 0)),
        ],
        out_specs=row_out,
        out_shape=out_shape,
        compiler_params=_params("arbitrary"),
        name="gla_proj",
    )(*x_args, w_in, w_glow, w_gate, b_gate)


def _cast_specs(w1, w2, layer, n_steps, step_of):
    _, d, ff = w1.shape
    assert d % n_steps == 0 and ff % n_steps == 0 and (d // n_steps) % BF16_ROWS == 0
    slabs = [(d // n_steps, ff), (ff // n_steps, d)]
    in_specs = [pl.BlockSpec((None,) + s, lambda *idx: (layer, step_of(*idx), 0)) for s in slabs]
    out_specs = [pl.BlockSpec(s, lambda *idx: (step_of(*idx), 0)) for s in slabs]
    return in_specs, out_specs, [jax.ShapeDtypeStruct((d, ff), BF16), jax.ShapeDtypeStruct((ff, d), BF16)]


def _gla_core_kernel(q_ref, k_ref, v_ref, r_ref, lg_ref, gn_ref, w1_ref, w2_ref,
                     o_ref, st_ref, w1b_ref, w2b_ref, s_scr, *, n_meta, chunk, n_chunks, dk, dv):
    w1b_ref[...] = w1_ref[...].astype(BF16)
    w2b_ref[...] = w2_ref[...].astype(BF16)
    wk = q_ref.shape[1]
    heads = range(wk // dk)
    scale = dk ** -0.5
    s_scr[...] = jnp.zeros_like(s_scr)

    def do_chunk(off, length):
        rows = pl.ds(off, length)
        ii = lax.broadcasted_iota(jnp.int32, (length, length), 0)
        jj = lax.broadcasted_iota(jnp.int32, (length, length), 1)
        causal = ii >= jj
        lg = lg_ref[rows, :]
        hi = lg.astype(BF16)
        rest = lg - hi.astype(F32)
        mid = rest.astype(BF16)
        lo = (rest - mid.astype(F32)).astype(BF16)
        parts = jnp.dot(causal.astype(BF16), jnp.concatenate([hi, mid, lo], axis=1),
                        preferred_element_type=F32)
        b = parts[:, 0:wk] + parts[:, wk:2 * wk] + parts[:, 2 * wk:3 * wk]
        q = q_ref[rows, :] * scale
        k = k_ref[rows, :]
        b_mid = b[length // 2:length // 2 + 1, :]
        b_last = b[length - 1:length, :]
        q_in = (q * jnp.exp(b)).astype(BF16)
        qs = (q * jnp.exp(b - b_mid)).astype(BF16)
        ks = (k * jnp.exp(b_mid - b)).astype(BF16)
        kd = (k * jnp.exp(b_last - b)).astype(BF16)
        decay = jnp.exp(b_last)
        vb = v_ref[rows, :]

        def kcols(hh):
            return slice(hh * dk, (hh + 1) * dk)

        def vcols(hh):
            return slice(hh * dv, (hh + 1) * dv)

        states = [s_scr[hh] for hh in heads]
        o_inter = [lax.dot_general(q_in[:, kcols(hh)], states[hh].astype(BF16), _NT,
                                   preferred_element_type=F32) for hh in heads]
        scores = [lax.dot_general(qs[:, kcols(hh)], ks[:, kcols(hh)], _NT, preferred_element_type=F32)
                  for hh in heads]
        scores = [jnp.where(causal, s, 0.0).astype(BF16) for s in scores]
        o_intra = [jnp.dot(scores[hh], vb[:, vcols(hh)], preferred_element_type=F32) for hh in heads]
        grown = [lax.dot_general(vb[:, vcols(hh)], kd[:, kcols(hh)], _TN, preferred_element_type=F32)
                 for hh in heads]
        for hh in heads:
            s_scr[hh] = states[hh] * decay[:, kcols(hh)] + grown[hh]
            r = r_ref[rows, vcols(hh)].astype(F32)
            gated = _rms_norm(o_inter[hh] + o_intra[hh], gn_ref[...]) * (r / (1.0 + jnp.exp(-r)))
            o_ref[rows, vcols(hh)] = gated.astype(o_ref.dtype)

    def token_by_token():
        group = BF16_ROWS
        seq = q_ref.shape[0]

        def as_rows(row):
            return jnp.concatenate([row, jnp.zeros((SUBLANES - 1, row.shape[1]), row.dtype)], axis=0)

        def body(g, carry):
            rows = pl.ds(pl.multiple_of(g * group, group), group)
            a_g = jnp.exp(lg_ref[rows, :])
            q_g = q_ref[rows, :] * scale
            k_g = k_ref[rows, :]
            v_g = v_ref[rows, :].astype(F32)
            r_g = r_ref[rows, :].astype(F32)
            outs = [[] for _ in heads]
            for t in range(group):
                a, q, k, v, r = (x[t:t + 1] for x in (a_g, q_g, k_g, v_g, r_g))
                for hh in heads:
                    kc, vc = slice(hh * dk, (hh + 1) * dk), slice(hh * dv, (hh + 1) * dv)
                    grown = lax.dot_general(as_rows(v[:, vc]).astype(BF16), as_rows(k[:, kc]).astype(BF16),
                                            _TN, preferred_element_type=F32)
                    s_new = s_scr[hh] * a[:, kc] + grown
                    s_scr[hh] = s_new
                    o = lax.dot_general(as_rows(q[:, kc]).astype(BF16), s_new.astype(BF16), _NT,
                                        preferred_element_type=F32)[0:1]
                    rr = r[:, vc]
                    outs[hh].append(_rms_norm(o, gn_ref[...]) * (rr / (1.0 + jnp.exp(-rr))))
            for hh in heads:
                o_ref[rows, hh * dv:(hh + 1) * dv] = jnp.concatenate(
                    outs[hh], axis=0).astype(o_ref.dtype)
            return carry

        lax.fori_loop(0, seq // group, body, 0)

    half_span = max(chunk, n_meta) // 2
    chunked_is_safe = jnp.min(lg_ref[...]) * half_span > -GLA_SAFE_LOG_SPAN

    @pl.when(chunked_is_safe)
    def _():
        do_chunk(0, n_meta)
        for c in range(n_chunks):
            do_chunk(n_meta + c * chunk, chunk)

    pl.when(jnp.logical_not(chunked_is_safe))(token_by_token)

    for hh in heads:
        st_ref[0, hh] = jnp.transpose(s_scr[hh])


def _gla_core(qk, vr, lg, g_norm, batch, seq, n_meta, mlp_w1, mlp_w2, layer):
    m = qk.shape[0]
    hk = lg.shape[1]
    dk = hk // GLA_HEADS
    dv = g_norm.shape[1]
    hv = GLA_HEADS * dv
    assert qk.shape[1] == 2 * hk and vr.shape[1] == 2 * hv and m == batch * seq
    chunk = GLA_CHUNK
    n_chunks = (seq - n_meta) // chunk
    assert n_meta + n_chunks * chunk == seq
    hps = GLA_HEADS_PER_STEP
    groups = GLA_HEADS // hps
    wk, wv = hps * dk, hps * dv
    kern = functools.partial(_gla_core_kernel, n_meta=n_meta, chunk=chunk, n_chunks=n_chunks, dk=dk, dv=dv)
    cast_in, cast_out, cast_shapes = _cast_specs(mlp_w1, mlp_w2, layer, batch * groups,
                                                 lambda b, g: b * groups + g)
    return pl.pallas_call(
        kern,
        grid=(batch, groups),
        in_specs=[
            pl.BlockSpec((seq, wk), lambda b, g: (b, g)),
            pl.BlockSpec((seq, wk), lambda b, g: (b, groups + g)),
            pl.BlockSpec((seq, wv), lambda b, g: (b, g)),
            pl.BlockSpec((seq, wv), lambda b, g: (b, groups + g)),
            pl.BlockSpec((seq, wk), lambda b, g: (b, g)),
            pl.BlockSpec((1, dv), lambda b, g: (0, 0)),
        ] + cast_in,
        out_specs=[
            pl.BlockSpec((seq, wv), lambda b, g: (b, g)),
            pl.BlockSpec((1, hps, dk, dv), lambda b, g: (b, g, 0, 0)),
        ] + cast_out,
        out_shape=[jax.ShapeDtypeStruct((m, hv), BF16),
                   jax.ShapeDtypeStruct((batch, GLA_HEADS, dk, dv), F32)] + cast_shapes,
        scratch_shapes=[pltpu.VMEM((hps, dv, dk), F32)],
        compiler_params=_params("arbitrary", "arbitrary"),
        name="gla_core",
    )(qk, qk, vr, vr, lg, g_norm, mlp_w1, mlp_w2)


def _gla_step_kernel(qk_ref, vr_ref, lg_ref, gn_ref, s_ref, o_ref, sn_ref):
    dk, dv = s_ref.shape[2], s_ref.shape[3]
    hk, hv = GLA_HEADS * dk, GLA_HEADS * dv
    scale = dk ** -0.5
    reps = dv // dk
    for sq in range(s_ref.shape[0]):
        for hh in range(GLA_HEADS):
            q = qk_ref[sq, :, hh * dk:(hh + 1) * dk] * scale
            k = qk_ref[sq, :, hk + hh * dk:hk + (hh + 1) * dk]
            v = vr_ref[sq, :, hh * dv:(hh + 1) * dv]
            r = vr_ref[sq, :, hv + hh * dv:hv + (hh + 1) * dv]
            a = jnp.exp(lg_ref[sq, :, hh * dk:(hh + 1) * dk])
            a_c = jnp.tile(_lane_col(a, dk), (1, reps))
            k_c = jnp.tile(_lane_col(k, dk), (1, reps))
            q_c = jnp.tile(_lane_col(q, dk), (1, reps))
            s_new = s_ref[sq, hh] * a_c + k_c * v
            sn_ref[sq, hh] = s_new
            o = jnp.sum(q_c * s_new, axis=0, keepdims=True)
            gated = _rms_norm(o, gn_ref[...]) * (r / (1.0 + jnp.exp(-r)))
            o_ref[sq, :, hh * dv:(hh + 1) * dv] = gated.astype(o_ref.dtype)


def _gla_step(qk, vr, lg, g_norm, state):
    bs = qk.shape[0]
    _, _, dk, dv = state.shape
    hv = GLA_HEADS * dv
    nb = math.gcd(bs, GLA_STEP_SEQS)
    o, s_new = pl.pallas_call(
        _gla_step_kernel,
        grid=(bs // nb,),
        in_specs=[
            pl.BlockSpec((nb, 1, qk.shape[1]), lambda b: (b, 0, 0)),
            pl.BlockSpec((nb, 1, vr.shape[1]), lambda b: (b, 0, 0)),
            pl.BlockSpec((nb, 1, lg.shape[1]), lambda b: (b, 0, 0)),
            pl.BlockSpec((1, dv), lambda b: (0, 0)),
            pl.BlockSpec((nb, GLA_HEADS, dk, dv), lambda b: (b, 0, 0, 0)),
        ],
        out_specs=[
            pl.BlockSpec((nb, 1, hv), lambda b: (b, 0, 0)),
            pl.BlockSpec((nb, GLA_HEADS, dk, dv), lambda b: (b, 0, 0, 0)),
        ],
        out_shape=[jax.ShapeDtypeStruct((bs, 1, hv), BF16), jax.ShapeDtypeStruct(state.shape, F32)],
        compiler_params=_params("arbitrary"),
        name="gla_step",
    )(qk.reshape(bs, 1, -1), vr.reshape(bs, 1, -1), lg.reshape(bs, 1, -1), g_norm, state)
    return o.reshape(bs, hv), s_new


def _lambda_value(lp):
    e1 = jnp.exp(jnp.sum(lp[0:1] * lp[1:2], axis=1, keepdims=True))
    e2 = jnp.exp(jnp.sum(lp[2:3] * lp[3:4], axis=1, keepdims=True))
    return e1 - e2 + LAM_INIT


class _Decode:
    def __init__(self, q_ref, m_scr, l_scr, acc_scr):
        self.m_scr, self.l_scr, self.acc_scr = m_scr, l_scr, acc_scr
        _, self.n_head, self.dv = q_ref.shape
        qt = q_ref[0]
        first_half = lax.broadcasted_iota(jnp.int32, qt.shape, 1) < self.dv // 2
        self.q_rows = jnp.concatenate([jnp.where(first_half, qt, 0.0), jnp.where(first_half, 0.0, qt)],
                                      axis=0).astype(BF16)

    def init(self):
        self.m_scr[...] = jnp.full(self.m_scr.shape, MASK_VALUE, F32)
        self.l_scr[...] = jnp.zeros_like(self.l_scr)
        self.acc_scr[...] = jnp.zeros_like(self.acc_scr)

    def own_head(self, n, n_valid):
        shape = (2 * self.n_head, n)
        col = lax.broadcasted_iota(jnp.int32, shape, 1)
        row = lax.broadcasted_iota(jnp.int32, shape, 0)
        return jnp.logical_and(col % self.n_head == row % self.n_head, col < n_valid)

    def scores(self, ks, own):
        return [jnp.where(own, lax.dot_general(self.q_rows, k2.astype(BF16), _NT, preferred_element_type=F32),
                          MASK_VALUE) for k2 in ks]

    def accumulate(self, ss, vs):
        m_old = self.m_scr[...]
        m_new = jnp.maximum(m_old, jnp.max(functools.reduce(jnp.maximum, ss), axis=1, keepdims=True))
        alpha = jnp.exp2(m_old - m_new)
        ps = [jnp.exp2(s - m_new) for s in ss]
        self.m_scr[...] = m_new
        self.l_scr[...] = alpha * self.l_scr[...] + jnp.sum(functools.reduce(jnp.add, ps), axis=1, keepdims=True)
        pv = [jnp.dot(p.astype(BF16), v2.astype(BF16), preferred_element_type=F32) for p, v2 in zip(ps, vs)]
        self.acc_scr[...] = alpha * self.acc_scr[...] + functools.reduce(jnp.add, pv)

    def page_scores(self, k_refs):
        n = k_refs[0].shape[1] * self.n_head
        return self.scores([r[0].reshape(n, self.dv) for r in k_refs], self.own_head(n, n))

    def accumulate_pages(self, ss, v_refs):
        n = v_refs[0].shape[1] * self.n_head
        self.accumulate(ss, [r[0].reshape(n, self.dv) for r in v_refs])

    def finish(self, kc_ref, vc_ref, lam_ref, g_ref, o_ref):
        fill = jnp.zeros((LANES - self.n_head, self.dv), F32)
        ss = self.scores([jnp.concatenate([kc_ref[0], fill], axis=0)], self.own_head(LANES, self.n_head))
        self.accumulate(ss, [jnp.concatenate([vc_ref[0], fill], axis=0)])
        ratio = self.acc_scr[...] / self.l_scr[...]
        o = ratio[:self.n_head] - _lambda_value(lam_ref[...]) * ratio[self.n_head:]
        o_ref[0] = _rms_norm(o, g_ref[...]) * (1.0 - LAM_INIT)


def _kept_row_runs(tile, tm, m, seq, n_meta):
    lo, hi = tile * tm, min((tile + 1) * tm, m)
    runs, r = [], lo
    while r < hi:
        b, t = divmod(r, seq)
        if t < n_meta:
            r = min(b * seq + n_meta, hi)
            continue
        end = min((b + 1) * seq, hi)
        runs.append((r - lo, end - r, b * (seq - n_meta) + t - n_meta))
        r = end
    return tuple(runs)


def _finish_kernel(*refs, n_pages_step, tiles_per_seq, kept_runs):
    hosting = n_pages_step > 0
    if hosting:
        refs = refs[1:]
    a_ref, wo_ref, x_ref, g1_ref, b1_ref, w1_ref, w2_ref, g2_ref, b2_ref = refs[:9]
    refs = refs[9:]
    if hosting:
        lam_ref, q_ref, kc_ref, vc_ref, gs_ref = refs[:5]
        k_refs = refs[5:5 + n_pages_step]
        v_refs = refs[5 + n_pages_step:5 + 2 * n_pages_step]
        o_ref, od_ref, xb_scr, acc_scr, m_scr, l_scr, dacc_scr, *refs = refs[5 + 2 * n_pages_step:]
        dec = _Decode(q_ref, m_scr, l_scr, dacc_scr)
    else:
        o_ref, xb_scr, acc_scr, *refs = refs
    if kept_runs is None:
        x1_ref = o_ref
    else:
        x1_ref, res_scr, out_sem = refs

        def out_copies(tile):
            return [pltpu.make_async_copy(res_scr.at[pl.ds(src, n), :], o_ref.at[pl.ds(dst, n), :], out_sem)
                    for src, n, dst in kept_runs[tile]]
    f = pl.program_id(1)
    last_f = pl.num_programs(1) - 1
    tile_in_seq = pl.program_id(0) % tiles_per_seq

    if hosting:
        pl.when(jnp.logical_and(f == 0, tile_in_seq == 0))(dec.init)

    def mlp_step(first, with_pages):
        if with_pages:
            page_scores = dec.page_scores(k_refs)
        hid = jnp.maximum(jnp.dot(xb_scr[...], w1_ref[...], preferred_element_type=F32), 0.0)
        if with_pages:
            dec.accumulate_pages(page_scores, v_refs)
        part = jnp.dot((hid * hid).astype(BF16), w2_ref[...], preferred_element_type=F32)
        acc_scr[...] = part if first else acc_scr[...] + part

    @pl.when(f == 0)
    def _():
        if hosting:
            page_scores = dec.page_scores(k_refs)
        proj = jnp.dot(a_ref[...], wo_ref[...], preferred_element_type=F32)
        x1 = _layer_norm(DEEPNORM_ALPHA * x_ref[...] + proj, g1_ref[...], b1_ref[...])
        x1_ref[...] = x1
        xb_scr[...] = x1.astype(BF16)
        if hosting:
            dec.accumulate_pages(page_scores, v_refs)
        mlp_step(True, False)

    @pl.when(f > 0)
    def _():
        mlp_step(False, hosting)

    @pl.when(f == last_f)
    def _():
        result = _layer_norm(DEEPNORM_ALPHA * x1_ref[...] + acc_scr[...], g2_ref[...], b2_ref[...])
        if kept_runs is None:
            o_ref[...] = result
        else:
            tile = pl.program_id(0)
            n_tiles = len(kept_runs)
            for t in range(1, n_tiles):
                @pl.when(tile == t)
                def _(t=t):
                    for copy in out_copies(t - 1):
                        copy.wait()
            res_scr[...] = result
            for t in range(n_tiles):
                @pl.when(tile == t)
                def _(t=t):
                    for copy in out_copies(t):
                        copy.start()
                    if t == n_tiles - 1:
                        for copy in out_copies(t):
                            copy.wait()

    if hosting:
        @pl.when(jnp.logical_and(f == last_f, tile_in_seq == tiles_per_seq - 1))
        def _():
            dec.finish(kc_ref, vc_ref, lam_ref, gs_ref, od_ref)


def _finish_layer(a, w_out, x, ln1, w1, w2, ln2, decode=None, drop_meta=None):
    m, kk = a.shape
    d = w_out.shape[1]
    ff = w1.shape[1]
    tf = min(MLP_FF_TILE, ff)
    n_f = ff // tf
    tps = HOST_TILES_PER_SEQ
    if decode is None:
        tm, pps = _row_tile(m), 0
        n_tiles = m // tm
    else:
        q, k_cur, v_cur, cache_k, cache_v, page_table, lam_params, g_sub = decode
        n_seqs, n_head, dv = q.shape
        n_tiles = n_seqs * tps
        tm = pl.cdiv(pl.cdiv(m, n_tiles), BF16_ROWS) * BF16_ROWS
        n_pages, page = page_table.shape[1], cache_k.shape[1]
        assert n_pages % (tps * n_f) == 0 and pl.cdiv(m, tm) == n_tiles
        pps = n_pages // (tps * n_f)
    row_spec = pl.BlockSpec((tm, d), lambda i, f, *_: (i, 0))
    vec_spec = pl.BlockSpec((1, d), lambda i, f, *_: (0, 0))
    in_specs = [
        pl.BlockSpec((tm, kk), lambda i, f, *_: (i, 0)),
        pl.BlockSpec((kk, d), lambda i, f, *_: (0, 0)),
        row_spec, vec_spec, vec_spec,
        pl.BlockSpec((d, tf), lambda i, f, *_: (0, f)),
        pl.BlockSpec((tf, d), lambda i, f, *_: (f, 0)),
        vec_spec, vec_spec,
    ]
    args = [a, w_out, x, *ln1, w1, w2, *ln2]
    out_specs = [row_spec]
    out_shape = [jax.ShapeDtypeStruct((m, d), F32)]
    scratch = [pltpu.VMEM((tm, d), BF16), pltpu.VMEM((tm, d), F32)]
    prefetch = []
    if decode is not None:
        head_spec = pl.BlockSpec((1, n_head, dv), lambda i, f, pt: (i // tps, 0, 0))

        def page_spec(s):
            return pl.BlockSpec((1, page, n_head, dv),
                                lambda i, f, pt: (pt[i // tps, ((i % tps) * n_f + f) * pps + s], 0, 0, 0))

        in_specs += [pl.BlockSpec(lam_params.shape, lambda i, f, pt: (0, 0)), head_spec, head_spec, head_spec,
                     pl.BlockSpec((1, dv), lambda i, f, pt: (0, 0))]
        in_specs += [page_spec(s) for s in range(pps)] * 2
        args += [lam_params, q, k_cur, v_cur, g_sub] + [cache_k] * pps + [cache_v] * pps
        out_specs.append(head_spec)
        out_shape.append(jax.ShapeDtypeStruct((n_seqs, n_head, dv), F32))
        scratch += [pltpu.VMEM((2 * n_head, 1), F32), pltpu.VMEM((2 * n_head, 1), F32),
                    pltpu.VMEM((2 * n_head, dv), F32)]
        prefetch = [page_table]
    kept_runs = None
    if drop_meta is not None:
        seq, n_meta = drop_meta
        kept_runs = tuple(_kept_row_runs(t, tm, m, seq, n_meta) for t in range(n_tiles))
        out_specs[0] = pl.BlockSpec(memory_space=pl.ANY)
        out_shape[0] = jax.ShapeDtypeStruct((m // seq * (seq - n_meta), d), F32)
        scratch += [pltpu.VMEM((tm, d), F32), pltpu.VMEM((tm, d), F32), pltpu.SemaphoreType.DMA(())]
    outs = pl.pallas_call(
        functools.partial(_finish_kernel, n_pages_step=pps, tiles_per_seq=tps, kept_runs=kept_runs),
        grid_spec=pltpu.PrefetchScalarGridSpec(
            num_scalar_prefetch=len(prefetch), grid=(n_tiles, n_f),
            in_specs=in_specs, out_specs=out_specs, scratch_shapes=scratch),
        out_shape=out_shape,
        compiler_params=_params("arbitrary", "arbitrary",
                                limit=VMEM_LIMIT if decode is None else VMEM_LIMIT_HOST),
        name="finish_layer" if decode is None else "finish_layer_host",
    )(*prefetch, *args)
    return outs[0] if decode is None else tuple(outs)


def _diff_proj_kernel(x_ref, w_ref, cos_ref, sin_up_ref, sin_dn_ref,
                      q_ref, k_ref, kb_ref, v_ref, vb_ref, *, tn, q_scale, rot_half):
    xb = x_ref[...].astype(BF16)
    nq = q_ref.shape[1]

    def project(seg, j):
        lo = seg * nq + j * tn
        return jnp.dot(xb, w_ref[:, lo:lo + tn], preferred_element_type=F32)

    def rope(t):
        groups = []
        for g in range(tn // LANES):
            tg = t[:, g * LANES:(g + 1) * LANES]
            groups.append(tg * cos_ref[...] + pltpu.roll(tg, LANES - rot_half, 1) * sin_up_ref[...]
                          + pltpu.roll(tg, rot_half, 1) * sin_dn_ref[...])
        return jnp.concatenate(groups, axis=1)

    for j in range(nq // tn):
        cols = slice(j * tn, (j + 1) * tn)
        q_ref[:, cols] = (rope(project(0, j)) * q_scale).astype(q_ref.dtype)
        kr = rope(project(1, j))
        k_ref[:, cols] = kr
        kb_ref[:, cols] = kr.astype(kb_ref.dtype)
        y = project(2, j)
        v_ref[:, cols] = y
        vb_ref[:, cols] = y.astype(vb_ref.dtype)


def _rope_tables(pos, dh):
    rot = dh // 4
    inv = ROPE_THETA ** (-jnp.arange(0, rot, 2, dtype=F32) / rot)
    ang = pos.astype(F32)[:, None] * inv[None, :]
    n = pos.shape[0]

    def lanes(first, second, fill):
        sub = jnp.concatenate([first, second, jnp.full((n, dh - rot), fill, F32)], axis=1)
        return jnp.tile(sub, (1, LANES // dh))

    zeros = jnp.zeros_like(ang)
    return (lanes(jnp.cos(ang), jnp.cos(ang), 1.0),
            lanes(-jnp.sin(ang), zeros, 0.0),
            lanes(zeros, jnp.sin(ang), 0.0))


def _diff_project(x, w, pos):
    m, d = x.shape
    n = w.shape[1]
    nq = n // 3
    dh = nq // (2 * DIFF_HEADS)
    assert m % pos.shape[0] == 0
    tm = _row_tile(pos.shape[0])
    pos_tiles = pos.shape[0] // tm
    cos_t, sin_up, sin_dn = _rope_tables(pos, dh)
    kern = functools.partial(_diff_proj_kernel, tn=_col_tile(nq), q_scale=dh ** -0.5 * LOG2_E, rot_half=dh // 8)
    tab_spec = pl.BlockSpec((tm, LANES), lambda i: (i % pos_tiles, 0))
    out_spec = pl.BlockSpec((tm, nq), lambda i: (i, 0))
    return pl.pallas_call(
        kern,
        grid=(m // tm,),
        in_specs=[
            pl.BlockSpec((tm, d), lambda i: (i, 0)),
            pl.BlockSpec((d, n), lambda i: (0, 0)),
            tab_spec, tab_spec, tab_spec,
        ],
        out_specs=[out_spec] * 5,
        out_shape=[jax.ShapeDtypeStruct((m, nq), BF16),
                   jax.ShapeDtypeStruct((m, nq), F32), jax.ShapeDtypeStruct((m, nq), BF16),
                   jax.ShapeDtypeStruct((m, nq), F32), jax.ShapeDtypeStruct((m, nq), BF16)],
        compiler_params=_params("arbitrary"),
        name="diff_proj",
    )(x, w, cos_t, sin_up, sin_dn)


def _diff_attn_kernel(lam_ref, q_ref, k_ref, v_ref, g_ref, w1_ref, w2_ref, o_ref, w1b_ref, w2b_ref,
                      kp_scr, vp_scr, s_scr, a_scr, *, n_meta, tile, n_tiles):
    w1b_ref[...] = w1_ref[...].astype(BF16)
    w2b_ref[...] = w2_ref[...].astype(BF16)
    lam = _lambda_value(lam_ref[...])
    seq, dq = k_ref.shape
    first = LANES
    pad = first - n_meta
    kp_scr[0:pad, :] = jnp.zeros((pad, dq), BF16)
    kp_scr[pad:pad + seq, :] = k_ref[...]
    vp_scr[0:pad, :] = jnp.zeros((pad, v_ref.shape[1]), BF16)
    vp_scr[pad:pad + seq, :] = v_ref[...]
    first_half = lax.broadcasted_iota(jnp.int32, (1, dq), 1) < dq // 2

    def stack(q):
        zero = jnp.zeros_like(q)
        return jnp.concatenate([jnp.where(first_half, q, zero), jnp.where(first_half, zero, q)], axis=0)

    def block_coords(length, width):
        ri = lax.broadcasted_iota(jnp.int32, (2 * length, width), 0)
        ri = jnp.where(ri >= length, ri - length, ri)
        ci = lax.broadcasted_iota(jnp.int32, (2 * length, width), 1)
        return ci, ri

    def finish(weights, width):
        o = jnp.dot(weights, vp_scr[0:width, :], preferred_element_type=F32)
        return (_rms_norm(o, g_ref[...]) * (1.0 - LAM_INIT)).astype(o_ref.dtype)

    ci, ri = block_coords(n_meta, first)
    s = lax.dot_general(stack(q_ref[0:n_meta, :]), kp_scr[0:first, :], _NT, preferred_element_type=F32)
    s = jnp.where(jnp.logical_and(ci >= pad, ci - pad <= ri), s, MASK_VALUE)
    p = jnp.exp2(s - jnp.max(s, axis=1, keepdims=True))
    p = p / jnp.sum(p, axis=1, keepdims=True)
    o_ref[0:n_meta, :] = finish((p[:n_meta] - lam * p[n_meta:]).astype(BF16), first)

    ci, ri = block_coords(tile, tile)
    causal = ci <= ri
    not_pad = block_coords(tile, first)[0] >= pad
    row_chunk = 64

    def key_block(j):
        return (0, first) if j == 0 else (first + (j - 1) * tile, first + j * tile)

    def score_tile(it):
        qoff = n_meta + it * tile
        qs = stack(q_ref[qoff:qoff + tile, :])
        for j in range(it + 2):
            lo, hi = key_block(j)
            s = lax.dot_general(qs, kp_scr[lo:hi, :], _NT, preferred_element_type=F32)
            if j == 0:
                s = jnp.where(not_pad, s, MASK_VALUE)
            if j == it + 1:
                s = jnp.where(causal, s, MASK_VALUE)
            s_scr[it % 2, :, lo:hi] = s

    def softmax_tile(it):
        buf = it % 2
        qoff = n_meta + it * tile
        width = key_block(it + 1)[1]
        col_chunks = [slice(c * LANES, (c + 1) * LANES) for c in range(width // LANES)]

        def normaliser(rsl):
            mx = functools.reduce(jnp.maximum, [s_scr[buf, rsl, csl] for csl in col_chunks])
            m_b = jnp.broadcast_to(jnp.max(mx, axis=1, keepdims=True), (row_chunk, LANES))
            total = jnp.zeros((row_chunk, LANES), F32)
            for csl in col_chunks:
                p = jnp.exp2(s_scr[buf, rsl, csl] - m_b)
                s_scr[buf, rsl, csl] = p
                total = total + p
            return jnp.broadcast_to(1.0 / jnp.sum(total, axis=1, keepdims=True), (row_chunk, LANES))

        for rc in range(tile // row_chunk):
            rows1 = slice(rc * row_chunk, (rc + 1) * row_chunk)
            rows2 = slice(tile + rc * row_chunk, tile + (rc + 1) * row_chunk)
            c1 = normaliser(rows1)
            c2 = lam * normaliser(rows2)
            for csl in col_chunks:
                a_scr[buf, rows1, csl] = (s_scr[buf, rows1, csl] * c1 - s_scr[buf, rows2, csl] * c2).astype(BF16)
        o_ref[qoff:qoff + tile, :] = finish(a_scr[buf, :, 0:width], width)

    score_tile(0)
    for it in range(n_tiles):
        if it + 1 < n_tiles:
            score_tile(it + 1)
        softmax_tile(it)


def _diff_attention(q, k, v, lam_params, g_sub, batch, seq, n_meta, mlp_w1, mlp_w2, layer):
    m, n = q.shape
    dv = g_sub.shape[1]
    tile = ATTN_TILE
    n_tiles = (seq - n_meta) // tile
    assert n_meta + n_tiles * tile == seq and n == DIFF_HEADS * dv
    assert n_meta <= LANES and (LANES - n_meta) % BF16_ROWS == 0
    padded = LANES + n_tiles * tile
    kern = functools.partial(_diff_attn_kernel, n_meta=n_meta, tile=tile, n_tiles=n_tiles)
    head_spec = pl.BlockSpec((seq, dv), lambda b, hh: (b, hh))
    cast_in, cast_out, cast_shapes = _cast_specs(mlp_w1, mlp_w2, layer, batch * DIFF_HEADS,
                                                 lambda b, hh: b * DIFF_HEADS + hh)
    return pl.pallas_call(
        kern,
        grid=(batch, DIFF_HEADS),
        in_specs=[
            pl.BlockSpec(lam_params.shape, lambda b, hh: (0, 0)),
            head_spec, head_spec, head_spec,
            pl.BlockSpec((1, dv), lambda b, hh: (0, 0)),
        ] + cast_in,
        out_specs=[head_spec] + cast_out,
        out_shape=[jax.ShapeDtypeStruct((m, n), BF16)] + cast_shapes,
        scratch_shapes=[pltpu.VMEM((padded, dv), BF16), pltpu.VMEM((padded, dv), BF16),
                        pltpu.VMEM((2, 2 * tile, padded), F32), pltpu.VMEM((2, tile, padded), BF16)],
        compiler_params=_params("arbitrary", "arbitrary"),
        name="diff_attn",
    )(lam_params, q, k, v, g_sub, mlp_w1, mlp_w2)


def kernel(x_prompt, x_sample, state_gla, cache_k, cache_v, page_table, meta_tokens,
           gla_w_in, gla_w_gate, gla_b_gate, gla_norm, gla_w_out,
           diff_w_in, diff_lambda, diff_norm, diff_w_out,
           mlp_w1, mlp_w2, ln_mix_g, ln_mix_b, ln_mlp_g, ln_mlp_b):
    batch, seq_new, d = x_prompt.shape
    n_meta = meta_tokens.shape[0]
    seq = n_meta + seq_new
    bs = x_sample.shape[0]
    assert x_sample.shape[1] == 1 and bs % 2 == 0

    x_s = x_sample.reshape(bs, d)

    def row(vec):
        return vec.reshape(1, -1).astype(F32)

    def finish_layer(layer, o, x, w_out, w1, w2, decode=None, drop_meta=None):
        return _finish_layer(o, w_out, x, (row(ln_mix_g[layer]), row(ln_mix_b[layer])), w1, w2,
                             (row(ln_mlp_g[layer]), row(ln_mlp_b[layer])), decode, drop_meta)

    rank = gla_w_gate.shape[1]
    n_main = gla_w_in.shape[2] - rank
    w_in0 = gla_w_in[0].astype(BF16)
    w_glow = jnp.pad(gla_w_in[0, :, n_main:], ((0, 0), (0, LANES - rank))).astype(BF16)
    w_gate = jnp.pad(gla_w_gate[0], ((0, LANES - rank), (0, 0))).astype(BF16)
    b_gate = row(gla_b_gate[0])
    g_norm = row(gla_norm[0])
    w_out0 = gla_w_out[0].astype(BF16)
    w_in1 = diff_w_in[0].astype(BF16)
    w_out1 = diff_w_out[0].astype(BF16)
    g_sub = row(diff_norm[0])
    lam_params = diff_lambda[0].astype(F32)
    dv = g_sub.shape[1]
    dv_gla = g_norm.shape[1]
    past = page_table.shape[1] * PAGE_SIZE

    qk_p, vr_p, lg_p, x_p = _gla_project(x_prompt, w_in0, w_glow, w_gate, b_gate, dv_gla,
                                         meta=meta_tokens.astype(x_prompt.dtype))
    o_p, st_p, w1_0, w2_0 = _gla_core(qk_p, vr_p, lg_p, g_norm, batch, seq, n_meta, mlp_w1, mlp_w2, 0)

    qk_s, vr_s, lg_s = _gla_project(x_s, w_in0, w_glow, w_gate, b_gate, dv_gla)
    o_s, st_s = _gla_step(qk_s, vr_s.astype(F32), lg_s, g_norm, state_gla[0].astype(F32))
    x_s = finish_layer(0, o_s, x_s, w_out0, w1_0, w2_0)
    q_s, k_s, _, v_s, _ = _diff_project(x_s, w_in1, jnp.full((bs,), past))
    heads = (bs, DIFF_HEADS, dv)
    q_s3, k_s3, v_s3 = q_s.astype(F32).reshape(heads), k_s.reshape(heads), v_s.reshape(heads)

    def hosted(lo):
        hi = lo + bs // 2
        return (q_s3[lo:hi], k_s3[lo:hi], v_s3[lo:hi], cache_k[0], cache_v[0], page_table[lo:hi],
                lam_params, g_sub)

    x_p, a_s_lo = finish_layer(0, o_p, x_p, w_out0, w1_0, w2_0, hosted(0))
    q_p, k_p, kb_p, v_p, vb_p = _diff_project(x_p, w_in1, jnp.arange(seq))
    a_p, w1_1, w2_1 = _diff_attention(q_p, kb_p, vb_p, lam_params, g_sub, batch, seq, n_meta, mlp_w1, mlp_w2, 1)
    y_p, a_s_hi = finish_layer(1, a_p, x_p, w_out1, w1_1, w2_1, hosted(bs // 2), drop_meta=(seq, n_meta))

    a_s = jnp.concatenate([a_s_lo, a_s_hi], axis=0).reshape(bs, DIFF_HEADS * dv).astype(BF16)
    x_s = finish_layer(1, a_s, x_s, w_out1, w1_1, w2_1)

    y_prompt = y_p.reshape(batch, seq_new, d)
    y_sample = x_s.reshape(bs, 1, d)
    return (y_prompt, y_sample, st_p[None], st_s[None],
            k_p.reshape(1, batch, seq, DIFF_HEADS, dv), v_p.reshape(1, batch, seq, DIFF_HEADS, dv),
            k_s.reshape(1, bs, 1, DIFF_HEADS, dv), v_s.reshape(1, bs, 1, DIFF_HEADS, dv))
```

```python
import functools
import math

import jax
import jax.numpy as jnp
from jax import lax
from jax.experimental import pallas as pl
from jax.experimental.pallas import tpu as pltpu

F32 = jnp.float32
BF16 = jnp.bfloat16

DEPTH = 2
GLA_HEADS = 4
GLA_TAU = 16.0
DIFF_HEADS = 8
ROPE_THETA = 500000.0
PAGE_SIZE = 128
LN_EPS = 1e-5
DEEPNORM_ALPHA = (2 * DEPTH) ** 0.25
LAM_INIT = 0.8 - 0.6 * math.exp(-0.3 * 1)

LANES = 128
SUBLANES = 8
BF16_ROWS = 16
VMEM_LIMIT = 48 * 1024 * 1024

GLA_CHUNK = 128
GLA_HEADS_PER_STEP = 2
GLA_STEP_SEQS = 8
GLA_SAFE_LOG_SPAN = 80.0
ATTN_TILE = 256
MLP_FF_TILE = 512
HOST_TILES_PER_SEQ = 1
VMEM_LIMIT_HOST = 60000 * 1024
MASK_VALUE = -1e30
LOG2_E = 1.4426950408889634

_NT = (((1,), (1,)), ((), ()))
_TN = (((0,), (0,)), ((), ()))


def _row_tile(m):
    for cand in (688, 512, 256, 128, 64, 32, 16, 8):
        if m % cand == 0:
            return cand
    raise ValueError(f"row count {m} is not a multiple of 8")


def _col_tile(n):
    for cand in (512, 256, 128):
        if n % cand == 0:
            return cand
    raise ValueError(f"column count {n} is not a multiple of 128")


def _params(*sem, limit=VMEM_LIMIT):
    return pltpu.CompilerParams(dimension_semantics=sem, vmem_limit_bytes=limit)


def _layer_norm(y, g, b):
    mu = jnp.mean(y, axis=-1, keepdims=True)
    yc = y - mu
    var = jnp.mean(yc * yc, axis=-1, keepdims=True)
    return yc * lax.rsqrt(var + LN_EPS) * g + b


def _rms_norm(y, g):
    return y * lax.rsqrt(jnp.mean(y * y, axis=-1, keepdims=True) + LN_EPS) * g


def _lane_col(row, n):
    return jnp.transpose(jnp.broadcast_to(row, (n, n)))


def _gla_proj_kernel(*refs, tn, n_meta, tiles_per_seq):
    if n_meta is None:
        x_ref, w_ref, wg_ref, wgate_ref, bgate_ref, qk_ref, vr_ref, lg_ref = refs
    else:
        new_ref, meta_ref, w_ref, wg_ref, wgate_ref, bgate_ref, qk_ref, vr_ref, lg_ref, x_ref = refs
        tm = x_ref.shape[0]
        first_tile = pl.program_id(0) % tiles_per_seq == 0

        @pl.when(first_tile)
        def _():
            x_ref[0:n_meta, :] = meta_ref[...]
            if n_meta < tm:
                x_ref[n_meta:tm, :] = new_ref[0, 0:tm - n_meta, :]

        @pl.when(jnp.logical_not(first_tile))
        def _():
            x_ref[...] = new_ref[0]

    xb = x_ref[...].astype(BF16)
    g_low = jnp.dot(xb, wg_ref[...], preferred_element_type=F32)
    z = jnp.dot(g_low.astype(BF16), wgate_ref[...], preferred_element_type=F32) + bgate_ref[...]
    log_sig = jnp.minimum(z, 0.0) - jnp.log(1.0 + jnp.exp(-jnp.abs(z)))
    lg_ref[...] = log_sig * (1.0 / GLA_TAU)
    n_qk = qk_ref.shape[1]
    for j in range((n_qk + vr_ref.shape[1]) // tn):
        y = jnp.dot(xb, w_ref[:, j * tn:(j + 1) * tn], preferred_element_type=F32)
        if j * tn < n_qk:
            qk_ref[:, j * tn:(j + 1) * tn] = y
        else:
            vr_ref[:, j * tn - n_qk:(j + 1) * tn - n_qk] = y.astype(vr_ref.dtype)


def _gla_project(x, w_in, w_glow, w_gate, b_gate, dv, meta=None):
    d = x.shape[-1]
    ng = w_gate.shape[1]
    n_qk, n_vr = 2 * ng, 2 * GLA_HEADS * dv
    assert w_in.shape[1] >= n_qk + n_vr
    if meta is None:
        m = x.shape[0]
        tm = _row_tile(m)
        n_meta = tps = None
        x_specs, x_args = [pl.BlockSpec((tm, d), lambda i: (i, 0))], [x]
    else:
        batch, seq_new, _ = x.shape
        n_meta = meta.shape[0]
        seq = n_meta + seq_new
        m = batch * seq
        tm = _row_tile(seq)
        tps = seq // tm
        assert n_meta <= tm and n_meta % SUBLANES == 0
        x_specs = [pl.BlockSpec((pl.Element(1), pl.Element(tm), pl.Element(d)),
                                lambda i: (i // tps, pl.multiple_of(jnp.maximum((i % tps) * tm - n_meta, 0),
                                                                    SUBLANES), 0)),
                   pl.BlockSpec((n_meta, d), lambda i: (0, 0))]
        x_args = [x, meta]
    row_out = [pl.BlockSpec((tm, n_qk), lambda i: (i, 0)), pl.BlockSpec((tm, n_vr), lambda i: (i, 0)),
               pl.BlockSpec((tm, ng), lambda i: (i, 0))]
    out_shape = [jax.ShapeDtypeStruct((m, n_qk), F32), jax.ShapeDtypeStruct((m, n_vr), BF16),
                 jax.ShapeDtypeStruct((m, ng), F32)]
    if meta is not None:
        row_out.append(pl.BlockSpec((tm, d), lambda i: (i, 0)))
        out_shape.append(jax.ShapeDtypeStruct((m, d), F32))
    return pl.pallas_call(
        functools.partial(_gla_proj_kernel, tn=_col_tile(n_qk), n_meta=n_meta, tiles_per_seq=tps),
        grid=(m // tm,),
        in_specs=x_specs + [
            pl.BlockSpec((d, n_qk + n_vr), lambda i: (0, 0)),
            pl.BlockSpec((d, LANES), lambda i: (0, 0)),
            pl.BlockSpec((LANES, ng), lambda i: (0, 0)),
            pl.BlockSpec((1, ng), lambda i: (0, 0)),
        ],
        out_specs=row_out,
        out_shape=out_shape,
        compiler_params=_params("arbitrary"),
        name="gla_proj",
    )(*x_args, w_in, w_glow, w_gate, b_gate)


def _cast_specs(w1, w2, layer, n_steps, step_of):
    _, d, ff = w1.shape
    assert d % n_steps == 0 and ff % n_steps == 0 and (d // n_steps) % BF16_ROWS == 0
    slabs = [(d // n_steps, ff), (ff // n_steps, d)]
    in_specs = [pl.BlockSpec((None,) + s, lambda *idx: (layer, step_of(*idx), 0)) for s in slabs]
    out_specs = [pl.BlockSpec(s, lambda *idx: (step_of(*idx), 0)) for s in slabs]
    return in_specs, out_specs, [jax.ShapeDtypeStruct((d, ff), BF16), jax.ShapeDtypeStruct((ff, d), BF16)]


def _gla_core_kernel(q_ref, k_ref, v_ref, r_ref, lg_ref, gn_ref, w1_ref, w2_ref,
                     o_ref, st_ref, w1b_ref, w2b_ref, s_scr, *, n_meta, chunk, n_chunks, dk, dv):
    w1b_ref[...] = w1_ref[...].astype(BF16)
    w2b_ref[...] = w2_ref[...].astype(BF16)
    wk = q_ref.shape[1]
    heads = range(wk // dk)
    scale = dk ** -0.5
    s_scr[...] = jnp.zeros_like(s_scr)

    def prepare_chunk(off, length):
        rows = pl.ds(off, length)
        ii = lax.broadcasted_iota(jnp.int32, (length, length), 0)
        jj = lax.broadcasted_iota(jnp.int32, (length, length), 1)
        causal = ii >= jj
        lg = lg_ref[rows, :]
        hi = lg.astype(BF16)
        rest = lg - hi.astype(F32)
        mid = rest.astype(BF16)
        lo = (rest - mid.astype(F32)).astype(BF16)
        parts = jnp.dot(causal.astype(BF16), jnp.concatenate([hi, mid, lo], axis=1),
                        preferred_element_type=F32)
        b = parts[:, 0:wk] + parts[:, wk:2 * wk] + parts[:, 2 * wk:3 * wk]
        q = q_ref[rows, :] * scale
        k = k_ref[rows, :]
        b_mid = b[length // 2:length // 2 + 1, :]
        b_last = b[length - 1:length, :]
        q_in = (q * jnp.exp(b)).astype(BF16)
        qs = (q * jnp.exp(b - b_mid)).astype(BF16)
        ks = (k * jnp.exp(b_mid - b)).astype(BF16)
        kd = (k * jnp.exp(b_last - b)).astype(BF16)
        decay = jnp.exp(b_last)
        vb = v_ref[rows, :]

        def kcols(hh):
            return slice(hh * dk, (hh + 1) * dk)

        def vcols(hh):
            return slice(hh * dv, (hh + 1) * dv)

        scores = [lax.dot_general(qs[:, kcols(hh)], ks[:, kcols(hh)], _NT, preferred_element_type=F32)
                  for hh in heads]
        scores = [jnp.where(causal, s, 0.0).astype(BF16) for s in scores]
        o_intra = [jnp.dot(scores[hh], vb[:, vcols(hh)], preferred_element_type=F32) for hh in heads]
        grown = [lax.dot_general(vb[:, vcols(hh)], kd[:, kcols(hh)], _TN, preferred_element_type=F32)
                 for hh in heads]
        return rows, [q_in[:, kcols(hh)] for hh in heads], o_intra, grown, [decay[:, kcols(hh)] for hh in heads]

    def apply_chunk(prepared):
        rows, q_in, o_intra, grown, decay = prepared
        for hh in heads:
            state = s_scr[hh]
            o_inter = lax.dot_general(q_in[hh], state.astype(BF16), _NT, preferred_element_type=F32)
            s_scr[hh] = state * decay[hh] + grown[hh]
            r = r_ref[rows, hh * dv:(hh + 1) * dv].astype(F32)
            gated = _rms_norm(o_inter + o_intra[hh], gn_ref[...]) * (r / (1.0 + jnp.exp(-r)))
            o_ref[rows, hh * dv:(hh + 1) * dv] = gated.astype(o_ref.dtype)

    def token_by_token():
        group = BF16_ROWS
        seq = q_ref.shape[0]

        def as_rows(row):
            return jnp.concatenate([row, jnp.zeros((SUBLANES - 1, row.shape[1]), row.dtype)], axis=0)

        def body(g, carry):
            rows = pl.ds(pl.multiple_of(g * group, group), group)
            a_g = jnp.exp(lg_ref[rows, :])
            q_g = q_ref[rows, :] * scale
            k_g = k_ref[rows, :]
            v_g = v_ref[rows, :].astype(F32)
            r_g = r_ref[rows, :].astype(F32)
            outs = [[] for _ in heads]
            for t in range(group):
                a, q, k, v, r = (x[t:t + 1] for x in (a_g, q_g, k_g, v_g, r_g))
                for hh in heads:
                    kc, vc = slice(hh * dk, (hh + 1) * dk), slice(hh * dv, (hh + 1) * dv)
                    grown = lax.dot_general(as_rows(v[:, vc]).astype(BF16), as_rows(k[:, kc]).astype(BF16),
                                            _TN, preferred_element_type=F32)
                    s_new = s_scr[hh] * a[:, kc] + grown
                    s_scr[hh] = s_new
                    o = lax.dot_general(as_rows(q[:, kc]).astype(BF16), s_new.astype(BF16), _NT,
                                        preferred_element_type=F32)[0:1]
                    rr = r[:, vc]
                    outs[hh].append(_rms_norm(o, gn_ref[...]) * (rr / (1.0 + jnp.exp(-rr))))
            for hh in heads:
                o_ref[rows, hh * dv:(hh + 1) * dv] = jnp.concatenate(
                    outs[hh], axis=0).astype(o_ref.dtype)
            return carry

        lax.fori_loop(0, seq // group, body, 0)

    half_span = max(chunk, n_meta) // 2
    chunked_is_safe = jnp.min(lg_ref[...]) * half_span > -GLA_SAFE_LOG_SPAN

    @pl.when(chunked_is_safe)
    def _():
        spans = [(0, n_meta)] + [(n_meta + c * chunk, chunk) for c in range(n_chunks)]
        prepared = prepare_chunk(*spans[0])
        for nxt in spans[1:] + [None]:
            following = prepare_chunk(*nxt) if nxt is not None else None
            apply_chunk(prepared)
            prepared = following

    pl.when(jnp.logical_not(chunked_is_safe))(token_by_token)

    for hh in heads:
        st_ref[0, hh] = jnp.transpose(s_scr[hh])


def _gla_core(qk, vr, lg, g_norm, batch, seq, n_meta, mlp_w1, mlp_w2, layer):
    m = qk.shape[0]
    hk = lg.shape[1]
    dk = hk // GLA_HEADS
    dv = g_norm.shape[1]
    hv = GLA_HEADS * dv
    assert qk.shape[1] == 2 * hk and vr.shape[1] == 2 * hv and m == batch * seq
    chunk = GLA_CHUNK
    n_chunks = (seq - n_meta) // chunk
    assert n_meta + n_chunks * chunk == seq
    hps = GLA_HEADS_PER_STEP
    groups = GLA_HEADS // hps
    wk, wv = hps * dk, hps * dv
    kern = functools.partial(_gla_core_kernel, n_meta=n_meta, chunk=chunk, n_chunks=n_chunks, dk=dk, dv=dv)
    cast_in, cast_out, cast_shapes = _cast_specs(mlp_w1, mlp_w2, layer, batch * groups,
                                                 lambda b, g: b * groups + g)
    return pl.pallas_call(
        kern,
        grid=(batch, groups),
        in_specs=[
            pl.BlockSpec((seq, wk), lambda b, g: (b, g)),
            pl.BlockSpec((seq, wk), lambda b, g: (b, groups + g)),
            pl.BlockSpec((seq, wv), lambda b, g: (b, g)),
            pl.BlockSpec((seq, wv), lambda b, g: (b, groups + g)),
            pl.BlockSpec((seq, wk), lambda b, g: (b, g)),
            pl.BlockSpec((1, dv), lambda b, g: (0, 0)),
        ] + cast_in,
        out_specs=[
            pl.BlockSpec((seq, wv), lambda b, g: (b, g)),
            pl.BlockSpec((1, hps, dk, dv), lambda b, g: (b, g, 0, 0)),
        ] + cast_out,
        out_shape=[jax.ShapeDtypeStruct((m, hv), BF16),
                   jax.ShapeDtypeStruct((batch, GLA_HEADS, dk, dv), F32)] + cast_shapes,
        scratch_shapes=[pltpu.VMEM((hps, dv, dk), F32)],
        compiler_params=_params("arbitrary", "arbitrary"),
        name="gla_core",
    )(qk, qk, vr, vr, lg, g_norm, mlp_w1, mlp_w2)


def _gla_step_kernel(qk_ref, vr_ref, lg_ref, gn_ref, s_ref, o_ref, sn_ref):
    dk, dv = s_ref.shape[2], s_ref.shape[3]
    hk, hv = GLA_HEADS * dk, GLA_HEADS * dv
    scale = dk ** -0.5
    reps = dv // dk
    for sq in range(s_ref.shape[0]):
        for hh in range(GLA_HEADS):
            q = qk_ref[sq, :, hh * dk:(hh + 1) * dk] * scale
            k = qk_ref[sq, :, hk + hh * dk:hk + (hh + 1) * dk]
            v = vr_ref[sq, :, hh * dv:(hh + 1) * dv]
            r = vr_ref[sq, :, hv + hh * dv:hv + (hh + 1) * dv]
            a = jnp.exp(lg_ref[sq, :, hh * dk:(hh + 1) * dk])
            a_c = jnp.tile(_lane_col(a, dk), (1, reps))
            k_c = jnp.tile(_lane_col(k, dk), (1, reps))
            q_c = jnp.tile(_lane_col(q, dk), (1, reps))
            s_new = s_ref[sq, hh] * a_c + k_c * v
            sn_ref[sq, hh] = s_new
            o = jnp.sum(q_c * s_new, axis=0, keepdims=True)
            gated = _rms_norm(o, gn_ref[...]) * (r / (1.0 + jnp.exp(-r)))
            o_ref[sq, :, hh * dv:(hh + 1) * dv] = gated.astype(o_ref.dtype)


def _gla_step(qk, vr, lg, g_norm, state):
    bs = qk.shape[0]
    _, _, dk, dv = state.shape
    hv = GLA_HEADS * dv
    nb = math.gcd(bs, GLA_STEP_SEQS)
    o, s_new = pl.pallas_call(
        _gla_step_kernel,
        grid=(bs // nb,),
        in_specs=[
            pl.BlockSpec((nb, 1, qk.shape[1]), lambda b: (b, 0, 0)),
            pl.BlockSpec((nb, 1, vr.shape[1]), lambda b: (b, 0, 0)),
            pl.BlockSpec((nb, 1, lg.shape[1]), lambda b: (b, 0, 0)),
            pl.BlockSpec((1, dv), lambda b: (0, 0)),
            pl.BlockSpec((nb, GLA_HEADS, dk, dv), lambda b: (b, 0, 0, 0)),
        ],
        out_specs=[
            pl.BlockSpec((nb, 1, hv), lambda b: (b, 0, 0)),
            pl.BlockSpec((nb, GLA_HEADS, dk, dv), lambda b: (b, 0, 0, 0)),
        ],
        out_shape=[jax.ShapeDtypeStruct((bs, 1, hv), BF16), jax.ShapeDtypeStruct(state.shape, F32)],
        compiler_params=_params("arbitrary"),
        name="gla_step",
    )(qk.reshape(bs, 1, -1), vr.reshape(bs, 1, -1), lg.reshape(bs, 1, -1), g_norm, state)
    return o.reshape(bs, hv), s_new


def _lambda_value(lp):
    e1 = jnp.exp(jnp.sum(lp[0:1] * lp[1:2], axis=1, keepdims=True))
    e2 = jnp.exp(jnp.sum(lp[2:3] * lp[3:4], axis=1, keepdims=True))
    return e1 - e2 + LAM_INIT


class _Decode:
    def __init__(self, q_ref, m_scr, l_scr, acc_scr):
        self.m_scr, self.l_scr, self.acc_scr = m_scr, l_scr, acc_scr
        _, self.n_head, self.dv = q_ref.shape
        qt = q_ref[0]
        first_half = lax.broadcasted_iota(jnp.int32, qt.shape, 1) < self.dv // 2
        self.q_rows = jnp.concatenate([jnp.where(first_half, qt, 0.0), jnp.where(first_half, 0.0, qt)],
                                      axis=0).astype(BF16)

    def init(self):
        self.m_scr[...] = jnp.full(self.m_scr.shape, MASK_VALUE, F32)
        self.l_scr[...] = jnp.zeros_like(self.l_scr)
        self.acc_scr[...] = jnp.zeros_like(self.acc_scr)

    def own_head(self, n, n_valid):
        shape = (2 * self.n_head, n)
        col = lax.broadcasted_iota(jnp.int32, shape, 1)
        row = lax.broadcasted_iota(jnp.int32, shape, 0)
        return jnp.logical_and(col % self.n_head == row % self.n_head, col < n_valid)

    def scores(self, ks, own):
        return [jnp.where(own, lax.dot_general(self.q_rows, k2.astype(BF16), _NT, preferred_element_type=F32),
                          MASK_VALUE) for k2 in ks]

    def accumulate(self, ss, vs):
        m_old = self.m_scr[...]
        m_new = jnp.maximum(m_old, jnp.max(functools.reduce(jnp.maximum, ss), axis=1, keepdims=True))
        alpha = jnp.exp2(m_old - m_new)
        ps = [jnp.exp2(s - m_new) for s in ss]
        self.m_scr[...] = m_new
        self.l_scr[...] = alpha * self.l_scr[...] + jnp.sum(functools.reduce(jnp.add, ps), axis=1, keepdims=True)
        pv = [jnp.dot(p.astype(BF16), v2.astype(BF16), preferred_element_type=F32) for p, v2 in zip(ps, vs)]
        self.acc_scr[...] = alpha * self.acc_scr[...] + functools.reduce(jnp.add, pv)

    def page_scores(self, k_refs):
        n = k_refs[0].shape[1] * self.n_head
        return self.scores([r[0].reshape(n, self.dv) for r in k_refs], self.own_head(n, n))

    def accumulate_pages(self, ss, v_refs):
        n = v_refs[0].shape[1] * self.n_head
        self.accumulate(ss, [r[0].reshape(n, self.dv) for r in v_refs])

    def finish(self, kc_ref, vc_ref, lam_ref, g_ref, o_ref):
        fill = jnp.zeros((LANES - self.n_head, self.dv), F32)
        ss = self.scores([jnp.concatenate([kc_ref[0], fill], axis=0)], self.own_head(LANES, self.n_head))
        self.accumulate(ss, [jnp.concatenate([vc_ref[0], fill], axis=0)])
        ratio = self.acc_scr[...] / self.l_scr[...]
        o = ratio[:self.n_head] - _lambda_value(lam_ref[...]) * ratio[self.n_head:]
        o_ref[0] = _rms_norm(o, g_ref[...]) * (1.0 - LAM_INIT)


def _kept_row_runs(tile, tm, m, seq, n_meta):
    lo, hi = tile * tm, min((tile + 1) * tm, m)
    runs, r = [], lo
    while r < hi:
        b, t = divmod(r, seq)
        if t < n_meta:
            r = min(b * seq + n_meta, hi)
            continue
        end = min((b + 1) * seq, hi)
        runs.append((r - lo, end - r, b * (seq - n_meta) + t - n_meta))
        r = end
    return tuple(runs)


def _finish_kernel(*refs, n_pages_step, tiles_per_seq, kept_runs):
    hosting = n_pages_step > 0
    if hosting:
        refs = refs[1:]
    a_ref, wo_ref, x_ref, g1_ref, b1_ref, w1_ref, w2_ref, g2_ref, b2_ref = refs[:9]
    refs = refs[9:]
    if hosting:
        lam_ref, q_ref, kc_ref, vc_ref, gs_ref = refs[:5]
        k_refs = refs[5:5 + n_pages_step]
        v_refs = refs[5 + n_pages_step:5 + 2 * n_pages_step]
        o_ref, od_ref, xb_scr, acc_scr, m_scr, l_scr, dacc_scr, *refs = refs[5 + 2 * n_pages_step:]
        dec = _Decode(q_ref, m_scr, l_scr, dacc_scr)
    else:
        o_ref, xb_scr, acc_scr, *refs = refs
    if kept_runs is None:
        x1_ref = o_ref
    else:
        x1_ref, res_scr, out_sem = refs

        def out_copies(tile):
            return [pltpu.make_async_copy(res_scr.at[pl.ds(src, n), :], o_ref.at[pl.ds(dst, n), :], out_sem)
                    for src, n, dst in kept_runs[tile]]
    f = pl.program_id(1)
    last_f = pl.num_programs(1) - 1
    tile_in_seq = pl.program_id(0) % tiles_per_seq

    if hosting:
        pl.when(jnp.logical_and(f == 0, tile_in_seq == 0))(dec.init)

    def mlp_step(first, with_pages):
        if with_pages:
            page_scores = dec.page_scores(k_refs)
        hid = jnp.maximum(jnp.dot(xb_scr[...], w1_ref[...], preferred_element_type=F32), 0.0)
        if with_pages:
            dec.accumulate_pages(page_scores, v_refs)
        part = jnp.dot((hid * hid).astype(BF16), w2_ref[...], preferred_element_type=F32)
        acc_scr[...] = part if first else acc_scr[...] + part

    @pl.when(f == 0)
    def _():
        if hosting:
            page_scores = dec.page_scores(k_refs)
        proj = jnp.dot(a_ref[...], wo_ref[...], preferred_element_type=F32)
        x1 = _layer_norm(DEEPNORM_ALPHA * x_ref[...] + proj, g1_ref[...], b1_ref[...])
        x1_ref[...] = x1
        xb_scr[...] = x1.astype(BF16)
        if hosting:
            dec.accumulate_pages(page_scores, v_refs)
        mlp_step(True, False)

    @pl.when(f > 0)
    def _():
        mlp_step(False, hosting)

    @pl.when(f == last_f)
    def _():
        result = _layer_norm(DEEPNORM_ALPHA * x1_ref[...] + acc_scr[...], g2_ref[...], b2_ref[...])
        if kept_runs is None:
            o_ref[...] = result
        else:
            tile = pl.program_id(0)
            n_tiles = len(kept_runs)
            for t in range(1, n_tiles):
                @pl.when(tile == t)
                def _(t=t):
                    for copy in out_copies(t - 1):
                        copy.wait()
            res_scr[...] = result
            for t in range(n_tiles):
                @pl.when(tile == t)
                def _(t=t):
                    for copy in out_copies(t):
                        copy.start()
                    if t == n_tiles - 1:
                        for copy in out_copies(t):
                            copy.wait()

    if hosting:
        @pl.when(jnp.logical_and(f == last_f, tile_in_seq == tiles_per_seq - 1))
        def _():
            dec.finish(kc_ref, vc_ref, lam_ref, gs_ref, od_ref)


def _finish_layer(a, w_out, x, ln1, w1, w2, ln2, decode=None, drop_meta=None):
    m, kk = a.shape
    d = w_out.shape[1]
    ff = w1.shape[1]
    tf = min(MLP_FF_TILE, ff)
    n_f = ff // tf
    tps = HOST_TILES_PER_SEQ
    if decode is None:
        tm, pps = _row_tile(m), 0
        n_tiles = m // tm
    else:
        q, k_cur, v_cur, cache_k, cache_v, page_table, lam_params, g_sub = decode
        n_seqs, n_head, dv = q.shape
        n_tiles = n_seqs * tps
        tm = pl.cdiv(pl.cdiv(m, n_tiles), BF16_ROWS) * BF16_ROWS
        n_pages, page = page_table.shape[1], cache_k.shape[1]
        assert n_pages % (tps * n_f) == 0 and pl.cdiv(m, tm) == n_tiles
        pps = n_pages // (tps * n_f)
    row_spec = pl.BlockSpec((tm, d), lambda i, f, *_: (i, 0))
    vec_spec = pl.BlockSpec((1, d), lambda i, f, *_: (0, 0))
    in_specs = [
        pl.BlockSpec((tm, kk), lambda i, f, *_: (i, 0)),
        pl.BlockSpec((kk, d), lambda i, f, *_: (0, 0)),
        row_spec, vec_spec, vec_spec,
        pl.BlockSpec((d, tf), lambda i, f, *_: (0, f)),
        pl.BlockSpec((tf, d), lambda i, f, *_: (f, 0)),
        vec_spec, vec_spec,
    ]
    args = [a, w_out, x, *ln1, w1, w2, *ln2]
    out_specs = [row_spec]
    out_shape = [jax.ShapeDtypeStruct((m, d), F32)]
    scratch = [pltpu.VMEM((tm, d), BF16), pltpu.VMEM((tm, d), F32)]
    prefetch = []
    if decode is not None:
        head_spec = pl.BlockSpec((1, n_head, dv), lambda i, f, pt: (i // tps, 0, 0))

        def page_spec(s):
            return pl.BlockSpec((1, page, n_head, dv),
                                lambda i, f, pt: (pt[i // tps, ((i % tps) * n_f + f) * pps + s], 0, 0, 0))

        in_specs += [pl.BlockSpec(lam_params.shape, lambda i, f, pt: (0, 0)), head_spec, head_spec, head_spec,
                     pl.BlockSpec((1, dv), lambda i, f, pt: (0, 0))]
        in_specs += [page_spec(s) for s in range(pps)] * 2
        args += [lam_params, q, k_cur, v_cur, g_sub] + [cache_k] * pps + [cache_v] * pps
        out_specs.append(head_spec)
        out_shape.append(jax.ShapeDtypeStruct((n_seqs, n_head, dv), F32))
        scratch += [pltpu.VMEM((2 * n_head, 1), F32), pltpu.VMEM((2 * n_head, 1), F32),
                    pltpu.VMEM((2 * n_head, dv), F32)]
        prefetch = [page_table]
    kept_runs = None
    if drop_meta is not None:
        seq, n_meta = drop_meta
        kept_runs = tuple(_kept_row_runs(t, tm, m, seq, n_meta) for t in range(n_tiles))
        out_specs[0] = pl.BlockSpec(memory_space=pl.ANY)
        out_shape[0] = jax.ShapeDtypeStruct((m // seq * (seq - n_meta), d), F32)
        scratch += [pltpu.VMEM((tm, d), F32), pltpu.VMEM((tm, d), F32), pltpu.SemaphoreType.DMA(())]
    outs = pl.pallas_call(
        functools.partial(_finish_kernel, n_pages_step=pps, tiles_per_seq=tps, kept_runs=kept_runs),
        grid_spec=pltpu.PrefetchScalarGridSpec(
            num_scalar_prefetch=len(prefetch), grid=(n_tiles, n_f),
            in_specs=in_specs, out_specs=out_specs, scratch_shapes=scratch),
        out_shape=out_shape,
        compiler_params=_params("arbitrary", "arbitrary",
                                limit=VMEM_LIMIT if decode is None else VMEM_LIMIT_HOST),
        name="finish_layer" if decode is None else "finish_layer_host",
    )(*prefetch, *args)
    return outs[0] if decode is None else tuple(outs)


def _diff_proj_kernel(x_ref, w_ref, cos_ref, sin_up_ref, sin_dn_ref,
                      q_ref, k_ref, kb_ref, v_ref, vb_ref, *, tn, q_scale, rot_half):
    xb = x_ref[...].astype(BF16)
    nq = q_ref.shape[1]

    def project(seg, j):
        lo = seg * nq + j * tn
        return jnp.dot(xb, w_ref[:, lo:lo + tn], preferred_element_type=F32)

    def rope(t):
        groups = []
        for g in range(tn // LANES):
            tg = t[:, g * LANES:(g + 1) * LANES]
            groups.append(tg * cos_ref[...] + pltpu.roll(tg, LANES - rot_half, 1) * sin_up_ref[...]
                          + pltpu.roll(tg, rot_half, 1) * sin_dn_ref[...])
        return jnp.concatenate(groups, axis=1)

    for j in range(nq // tn):
        cols = slice(j * tn, (j + 1) * tn)
        q_ref[:, cols] = (rope(project(0, j)) * q_scale).astype(q_ref.dtype)
        kr = rope(project(1, j))
        k_ref[:, cols] = kr
        kb_ref[:, cols] = kr.astype(kb_ref.dtype)
        y = project(2, j)
        v_ref[:, cols] = y
        vb_ref[:, cols] = y.astype(vb_ref.dtype)


def _rope_tables(pos, dh):
    rot = dh // 4
    inv = ROPE_THETA ** (-jnp.arange(0, rot, 2, dtype=F32) / rot)
    ang = pos.astype(F32)[:, None] * inv[None, :]
    n = pos.shape[0]

    def lanes(first, second, fill):
        sub = jnp.concatenate([first, second, jnp.full((n, dh - rot), fill, F32)], axis=1)
        return jnp.tile(sub, (1, LANES // dh))

    zeros = jnp.zeros_like(ang)
    return (lanes(jnp.cos(ang), jnp.cos(ang), 1.0),
            lanes(-jnp.sin(ang), zeros, 0.0),
            lanes(zeros, jnp.sin(ang), 0.0))


def _diff_project(x, w, pos):
    m, d = x.shape
    n = w.shape[1]
    nq = n // 3
    dh = nq // (2 * DIFF_HEADS)
    assert m % pos.shape[0] == 0
    tm = _row_tile(pos.shape[0])
    pos_tiles = pos.shape[0] // tm
    cos_t, sin_up, sin_dn = _rope_tables(pos, dh)
    kern = functools.partial(_diff_proj_kernel, tn=_col_tile(nq), q_scale=dh ** -0.5 * LOG2_E, rot_half=dh // 8)
    tab_spec = pl.BlockSpec((tm, LANES), lambda i: (i % pos_tiles, 0))
    out_spec = pl.BlockSpec((tm, nq), lambda i: (i, 0))
    return pl.pallas_call(
        kern,
        grid=(m // tm,),
        in_specs=[
            pl.BlockSpec((tm, d), lambda i: (i, 0)),
            pl.BlockSpec((d, n), lambda i: (0, 0)),
            tab_spec, tab_spec, tab_spec,
        ],
        out_specs=[out_spec] * 5,
        out_shape=[jax.ShapeDtypeStruct((m, nq), BF16),
                   jax.ShapeDtypeStruct((m, nq), F32), jax.ShapeDtypeStruct((m, nq), BF16),
                   jax.ShapeDtypeStruct((m, nq), F32), jax.ShapeDtypeStruct((m, nq), BF16)],
        compiler_params=_params("arbitrary"),
        name="diff_proj",
    )(x, w, cos_t, sin_up, sin_dn)


def _diff_attn_kernel(lam_ref, q_ref, k_ref, v_ref, g_ref, w1_ref, w2_ref, o_ref, w1b_ref, w2b_ref,
                      kp_scr, vp_scr, s_scr, a_scr, *, n_meta, tile, n_tiles):
    w1b_ref[...] = w1_ref[...].astype(BF16)
    w2b_ref[...] = w2_ref[...].astype(BF16)
    lam = _lambda_value(lam_ref[...])
    seq, dq = k_ref.shape
    first = LANES
    pad = first - n_meta
    kp_scr[0:pad, :] = jnp.zeros((pad, dq), BF16)
    kp_scr[pad:pad + seq, :] = k_ref[...]
    vp_scr[0:pad, :] = jnp.zeros((pad, v_ref.shape[1]), BF16)
    vp_scr[pad:pad + seq, :] = v_ref[...]
    first_half = lax.broadcasted_iota(jnp.int32, (1, dq), 1) < dq // 2

    def stack(q):
        zero = jnp.zeros_like(q)
        return jnp.concatenate([jnp.where(first_half, q, zero), jnp.where(first_half, zero, q)], axis=0)

    def block_coords(length, width):
        ri = lax.broadcasted_iota(jnp.int32, (2 * length, width), 0)
        ri = jnp.where(ri >= length, ri - length, ri)
        ci = lax.broadcasted_iota(jnp.int32, (2 * length, width), 1)
        return ci, ri

    def finish(weights, width):
        o = jnp.dot(weights, vp_scr[0:width, :], preferred_element_type=F32)
        return (_rms_norm(o, g_ref[...]) * (1.0 - LAM_INIT)).astype(o_ref.dtype)

    ci, ri = block_coords(n_meta, first)
    s = lax.dot_general(stack(q_ref[0:n_meta, :]), kp_scr[0:first, :], _NT, preferred_element_type=F32)
    s = jnp.where(jnp.logical_and(ci >= pad, ci - pad <= ri), s, MASK_VALUE)
    p = jnp.exp2(s - jnp.max(s, axis=1, keepdims=True))
    p = p / jnp.sum(p, axis=1, keepdims=True)
    o_ref[0:n_meta, :] = finish((p[:n_meta] - lam * p[n_meta:]).astype(BF16), first)

    ci, ri = block_coords(tile, tile)
    causal = ci <= ri
    not_pad = block_coords(tile, first)[0] >= pad
    row_chunk = 64

    def key_block(j):
        return (0, first) if j == 0 else (first + (j - 1) * tile, first + j * tile)

    def score_tile(it):
        qoff = n_meta + it * tile
        qs = stack(q_ref[qoff:qoff + tile, :])
        for j in range(it + 2):
            lo, hi = key_block(j)
            s = lax.dot_general(qs, kp_scr[lo:hi, :], _NT, preferred_element_type=F32)
            if j == 0:
                s = jnp.where(not_pad, s, MASK_VALUE)
            if j == it + 1:
                s = jnp.where(causal, s, MASK_VALUE)
            s_scr[it % 2, :, lo:hi] = s

    def softmax_tile(it):
        buf = it % 2
        qoff = n_meta + it * tile
        width = key_block(it + 1)[1]
        col_chunks = [slice(c * LANES, (c + 1) * LANES) for c in range(width // LANES)]

        def normaliser(rsl):
            mx = functools.reduce(jnp.maximum, [s_scr[buf, rsl, csl] for csl in col_chunks])
            m_b = jnp.broadcast_to(jnp.max(mx, axis=1, keepdims=True), (row_chunk, LANES))
            total = jnp.zeros((row_chunk, LANES), F32)
            for csl in col_chunks:
                p = jnp.exp2(s_scr[buf, rsl, csl] - m_b)
                s_scr[buf, rsl, csl] = p
                total = total + p
            return jnp.broadcast_to(1.0 / jnp.sum(total, axis=1, keepdims=True), (row_chunk, LANES))

        for rc in range(tile // row_chunk):
            rows1 = slice(rc * row_chunk, (rc + 1) * row_chunk)
            rows2 = slice(tile + rc * row_chunk, tile + (rc + 1) * row_chunk)
            c1 = normaliser(rows1)
            c2 = lam * normaliser(rows2)
            for csl in col_chunks:
                a_scr[buf, rows1, csl] = (s_scr[buf, rows1, csl] * c1 - s_scr[buf, rows2, csl] * c2).astype(BF16)
        o_ref[qoff:qoff + tile, :] = finish(a_scr[buf, :, 0:width], width)

    score_tile(0)
    for it in range(n_tiles):
        if it + 1 < n_tiles:
            score_tile(it + 1)
        softmax_tile(it)


def _diff_attention(q, k, v, lam_params, g_sub, batch, seq, n_meta, mlp_w1, mlp_w2, layer):
    m, n = q.shape
    dv = g_sub.shape[1]
    tile = ATTN_TILE
    n_tiles = (seq - n_meta) // tile
    assert n_meta + n_tiles * tile == seq and n == DIFF_HEADS * dv
    assert n_meta <= LANES and (LANES - n_meta) % BF16_ROWS == 0
    padded = LANES + n_tiles * tile
    kern = functools.partial(_diff_attn_kernel, n_meta=n_meta, tile=tile, n_tiles=n_tiles)
    head_spec = pl.BlockSpec((seq, dv), lambda b, hh: (b, hh))
    cast_in, cast_out, cast_shapes = _cast_specs(mlp_w1, mlp_w2, layer, batch * DIFF_HEADS,
                                                 lambda b, hh: b * DIFF_HEADS + hh)
    return pl.pallas_call(
        kern,
        grid=(batch, DIFF_HEADS),
        in_specs=[
            pl.BlockSpec(lam_params.shape, lambda b, hh: (0, 0)),
            head_spec, head_spec, head_spec,
            pl.BlockSpec((1, dv), lambda b, hh: (0, 0)),
        ] + cast_in,
        out_specs=[head_spec] + cast_out,
        out_shape=[jax.ShapeDtypeStruct((m, n), BF16)] + cast_shapes,
        scratch_shapes=[pltpu.VMEM((padded, dv), BF16), pltpu.VMEM((padded, dv), BF16),
                        pltpu.VMEM((2, 2 * tile, padded), F32), pltpu.VMEM((2, tile, padded), BF16)],
        compiler_params=_params("arbitrary", "arbitrary"),
        name="diff_attn",
    )(lam_params, q, k, v, g_sub, mlp_w1, mlp_w2)


def kernel(x_prompt, x_sample, state_gla, cache_k, cache_v, page_table, meta_tokens,
           gla_w_in, gla_w_gate, gla_b_gate, gla_norm, gla_w_out,
           diff_w_in, diff_lambda, diff_norm, diff_w_out,
           mlp_w1, mlp_w2, ln_mix_g, ln_mix_b, ln_mlp_g, ln_mlp_b):
    batch, seq_new, d = x_prompt.shape
    n_meta = meta_tokens.shape[0]
    seq = n_meta + seq_new
    bs = x_sample.shape[0]
    assert x_sample.shape[1] == 1 and bs % 2 == 0

    x_s = x_sample.reshape(bs, d)

    def row(vec):
        return vec.reshape(1, -1).astype(F32)

    def finish_layer(layer, o, x, w_out, w1, w2, decode=None, drop_meta=None):
        return _finish_layer(o, w_out, x, (row(ln_mix_g[layer]), row(ln_mix_b[layer])), w1, w2,
                             (row(ln_mlp_g[layer]), row(ln_mlp_b[layer])), decode, drop_meta)

    rank = gla_w_gate.shape[1]
    n_main = gla_w_in.shape[2] - rank
    w_in0 = gla_w_in[0].astype(BF16)
    w_glow = jnp.pad(gla_w_in[0, :, n_main:], ((0, 0), (0, LANES - rank))).astype(BF16)
    w_gate = jnp.pad(gla_w_gate[0], ((0, LANES - rank), (0, 0))).astype(BF16)
    b_gate = row(gla_b_gate[0])
    g_norm = row(gla_norm[0])
    w_out0 = gla_w_out[0].astype(BF16)
    w_in1 = diff_w_in[0].astype(BF16)
    w_out1 = diff_w_out[0].astype(BF16)
    g_sub = row(diff_norm[0])
    lam_params = diff_lambda[0].astype(F32)
    dv = g_sub.shape[1]
    dv_gla = g_norm.shape[1]
    past = page_table.shape[1] * PAGE_SIZE

    qk_p, vr_p, lg_p, x_p = _gla_project(x_prompt, w_in0, w_glow, w_gate, b_gate, dv_gla,
                                         meta=meta_tokens.astype(x_prompt.dtype))
    o_p, st_p, w1_0, w2_0 = _gla_core(qk_p, vr_p, lg_p, g_norm, batch, seq, n_meta, mlp_w1, mlp_w2, 0)

    qk_s, vr_s, lg_s = _gla_project(x_s, w_in0, w_glow, w_gate, b_gate, dv_gla)
    o_s, st_s = _gla_step(qk_s, vr_s.astype(F32), lg_s, g_norm, state_gla[0].astype(F32))
    x_s = finish_layer(0, o_s, x_s, w_out0, w1_0, w2_0)
    q_s, k_s, _, v_s, _ = _diff_project(x_s, w_in1, jnp.full((bs,), past))
    heads = (bs, DIFF_HEADS, dv)
    q_s3, k_s3, v_s3 = q_s.astype(F32).reshape(heads), k_s.reshape(heads), v_s.reshape(heads)

    def hosted(lo):
        hi = lo + bs // 2
        return (q_s3[lo:hi], k_s3[lo:hi], v_s3[lo:hi], cache_k[0], cache_v[0], page_table[lo:hi],
                lam_params, g_sub)

    x_p, a_s_lo = finish_layer(0, o_p, x_p, w_out0, w1_0, w2_0, hosted(0))
    q_p, k_p, kb_p, v_p, vb_p = _diff_project(x_p, w_in1, jnp.arange(seq))
    a_p, w1_1, w2_1 = _diff_attention(q_p, kb_p, vb_p, lam_params, g_sub, batch, seq, n_meta, mlp_w1, mlp_w2, 1)
    y_p, a_s_hi = finish_layer(1, a_p, x_p, w_out1, w1_1, w2_1, hosted(bs // 2), drop_meta=(seq, n_meta))

    a_s = jnp.concatenate([a_s_lo, a_s_hi], axis=0).reshape(bs, DIFF_HEADS * dv).astype(BF16)
    x_s = finish_layer(1, a_s, x_s, w_out1, w1_1, w2_1)

    y_prompt = y_p.reshape(batch, seq_new, d)
    y_sample = x_s.reshape(bs, 1, d)
    return (y_prompt, y_sample, st_p[None], st_s[None],
            k_p.reshape(1, batch, seq, DIFF_HEADS, dv), v_p.reshape(1, batch, seq, DIFF_HEADS, dv),
            k_s.reshape(1, bs, 1, DIFF_HEADS, dv), v_s.reshape(1, bs, 1, DIFF_HEADS, dv))
```
